```python
import math
import jax, jax.numpy as jnp
from jax import lax
import numpy as np

D_MODEL = 4096
BATCH = 2
SEQ = 8192
DEPTH = 2

HEAD_DIM = 128
RET_HEADS = D_MODEL // (2 * HEAD_DIM)
ATT_HEADS = D_MODEL // (2 * HEAD_DIM)
RET_WIDTH = RET_HEADS * HEAD_DIM
ATT_WIDTH = ATT_HEADS * HEAD_DIM
AB_IN = 4 * RET_WIDTH + 3 * ATT_WIDTH
AB_OUT = RET_WIDTH + ATT_WIDTH
AB_SPLITS = [RET_WIDTH, 2 * RET_WIDTH, 3 * RET_WIDTH, 4 * RET_WIDTH,
             4 * RET_WIDTH + ATT_WIDTH, 4 * RET_WIDTH + 2 * ATT_WIDTH]
RET_CHUNK = 128
ROPE_BASE = 10000.0
SWA_PATTERNS = ((128, 1), (512, 4), (2048, 16))
SWA_BLOCK = 128
DN_HEADS = D_MODEL // HEAD_DIM
DN_WIDTH = DN_HEADS * HEAD_DIM
DN_IN = 4 * DN_WIDTH + 2 * DN_HEADS
DN_SPLITS = [3 * DN_WIDTH, 4 * DN_WIDTH, 4 * DN_WIDTH + DN_HEADS]
DN_CONV = 4
DN_CHUNK = 64
D_FF = 256 * (-(-(8 * D_MODEL) // (3 * 256)))
LN_EPS = 1e-5
NORM_EPS = 1e-6
DEEPNORM_ALPHA = (2.0 * DEPTH) ** 0.25
DEEPNORM_BETA = (8.0 * DEPTH) ** -0.25
N_EVEN = (DEPTH + 1) // 2
N_ODD = DEPTH // 2

kernel_name = "hybrid_retention_dilatedswa_gdn_macaron_deepnorm"


def _layer_norm(x, w, b):
    x32 = x.astype(jnp.float32)
    mu = x32.mean(-1, keepdims=True)
    var = jnp.square(x32 - mu).mean(-1, keepdims=True)
    return ((x32 - mu) * lax.rsqrt(var + LN_EPS) * w + b).astype(x.dtype)


def _swiglu(x, w_gate, w_up, w_down):
    return (jax.nn.silu(x @ w_gate) * (x @ w_up)) @ w_down


def _heads(t, n):
    b, s, _ = t.shape
    return t.reshape(b, s, n, HEAD_DIM).transpose(0, 2, 1, 3)


def _merge(t):
    b, n, s, d = t.shape
    return t.transpose(0, 2, 1, 3).reshape(b, s, n * d)


def _rotate(x, cos, sin):
    x1, x2 = x[..., ::2], x[..., 1::2]
    return jnp.stack([x1 * cos - x2 * sin, x1 * sin + x2 * cos], axis=-1).reshape(x.shape)


def _retention(q, k, v):
    b, h, s, d = q.shape
    c = RET_CHUNK
    n = s // c
    log_g = jnp.log1p(-jnp.exp2(-5.0 - jnp.arange(h, dtype=jnp.float32)))
    idx = jnp.arange(c, dtype=jnp.float32)
    rel = idx[:, None] - idx[None, :]
    dmask = jnp.where(rel >= 0, jnp.exp(log_g[:, None, None] * jnp.maximum(rel, 0.0)), 0.0)
    qc, kc, vc = (t.reshape(b, h, n, c, d) for t in (q, k, v))
    scores = jnp.einsum("bhnid,bhnjd->bhnij", qc, kc) * dmask[:, None]
    o_intra = jnp.einsum("bhnij,bhnje->bhnie", scores, vc)
    zeta = jnp.exp(log_g[:, None] * (c - 1 - idx))
    xi = jnp.exp(log_g[:, None] * (idx + 1))
    kv = jnp.einsum("bhnjd,bhnje->bhnde", kc * zeta[:, None, :, None], vc)
    chunk_decay = jnp.exp(log_g * c)[:, None, None]

    def step(state, kv_n):
        return state * chunk_decay + kv_n, state

    _, prev = lax.scan(step, jnp.zeros((b, h, d, d), jnp.float32), jnp.moveaxis(kv, 2, 0))
    prev = jnp.moveaxis(prev, 0, 2)
    o_cross = jnp.einsum("bhnid,bhnde->bhnie", qc, prev) * xi[:, None, :, None]
    return (o_intra + o_cross).reshape(b, h, s, d)


def _dilated_window_attention(q, k, v, window, dilation):
    b, h, s, d = q.shape
    n_keys = window // dilation
    blk = SWA_BLOCK
    span = blk * dilation
    padded = -(-s // span) * span
    length = padded // dilation
    nb = length // blk

    def by_residue(t):
        t = jnp.pad(t, ((0, 0), (0, 0), (0, padded - s), (0, 0)))
        t = t.reshape(b, h, length, dilation, d).transpose(0, 1, 3, 2, 4)
        return t.reshape(b, h, dilation, nb, blk, d)

    def with_prev(t):
        prev = jnp.concatenate([jnp.zeros_like(t[:, :, :, :1]), t[:, :, :, :-1]], axis=3)
        return jnp.concatenate([prev, t], axis=4)

    qb = by_residue(q)
    kk = with_prev(by_residue(k))
    vv = with_prev(by_residue(v))
    scores = jnp.einsum("bhrnid,bhrnjd->bhrnij", qb, kk).astype(jnp.float32) * (HEAD_DIM ** -0.5)
    i = jnp.arange(blk)[:, None] + blk
    j = jnp.arange(2 * blk)[None, :]
    dist = i - j
    valid = (dist >= 0) & (dist <= n_keys)
    first = (jnp.arange(nb) == 0)[:, None, None] & (j < blk)[None]
    valid = valid[None] & ~first
    scores = jnp.where(valid, scores, -jnp.inf)
    lse = jax.nn.logsumexp(scores, axis=-1)
    p = jnp.exp(scores - lse[..., None])
    o = jnp.einsum("bhrnij,bhrnjd->bhrnid", p, vv.astype(jnp.float32))

    def back(t):
        rest = t.shape[5:]
        t = t.reshape((b, h, dilation, length) + rest)
        t = jnp.moveaxis(t, 2, 3).reshape((b, h, padded) + rest)
        return t[:, :, :s]

    return back(o), back(lse)


def _mixer_retention_swa(x, w_in, gn_w, w_out):
    b, s, _ = x.shape
    f32 = jnp.float32
    proj = x @ w_in
    rq, rk, rv, rg, aq, ak, av = jnp.split(proj, AB_SPLITS, axis=-1)
    pos = jnp.arange(s, dtype=f32)
    inv_freq = 1.0 / (ROPE_BASE ** jnp.linspace(0.0, 1.0, HEAD_DIM // 2, dtype=f32))
    ang = pos[:, None] * inv_freq[None, :]
    cos, sin = jnp.cos(ang), jnp.sin(ang)
    q = _rotate(_heads(rq, RET_HEADS).astype(f32), cos, sin)
    k = _rotate(_heads(rk, RET_HEADS).astype(f32), cos, sin) * (HEAD_DIM ** -0.5)
    v = _heads(rv, RET_HEADS).astype(f32)
    y = _retention(q, k, v)
    mu = y.mean(-1, keepdims=True)
    var = jnp.square(y - mu).mean(-1, keepdims=True)
    y = (y - mu) * lax.rsqrt(var + LN_EPS) * gn_w.astype(f32).reshape(RET_HEADS, 1, HEAD_DIM)
    y_ret = _merge(y).astype(x.dtype) * jax.nn.silu(rg)
    q = _heads(aq, ATT_HEADS)
    k = _heads(ak, ATT_HEADS)
    v = _heads(av, ATT_HEADS)
    outs, lses = [], []
    for window, dilation in SWA_PATTERNS:
        o, lse = _dilated_window_attention(q, k, v, window, dilation)
        outs.append(o)
        lses.append(lse)
    wts = jax.nn.softmax(jnp.stack(lses), axis=0)
    y_att = jnp.sum(wts[..., None] * jnp.stack(outs), axis=0)
    y_att = _merge(y_att).astype(x.dtype)
    return jnp.concatenate([y_ret, y_att], axis=-1) @ w_out


def _causal_conv(x, w):
    kw = w.shape[0]
    s = x.shape[1]
    xp = jnp.pad(x, ((0, 0), (kw - 1, 0), (0, 0)))
    out = xp[:, 0:s] * w[0]
    for i in range(1, kw):
        out = out + xp[:, i:i + s] * w[i]
    return out


def _l2norm(t):
    return t * lax.rsqrt(jnp.sum(t * t, axis=-1, keepdims=True) + NORM_EPS)


def _gated_delta_rule(q, k, v, g, beta):
    b, h, s, d = q.shape
    c = DN_CHUNK
    n = s // c
    f32 = jnp.float32
    q, k, v = (t.reshape(b, h, n, c, d) for t in (q, k, v))
    g = jnp.cumsum(g.reshape(b, h, n, c), axis=-1)
    beta = beta.reshape(b, h, n, c)
    idx = jnp.arange(c)
    lower = idx[:, None] >= idx[None, :]
    strict = idx[:, None] > idx[None, :]
    decay = jnp.exp(jnp.where(lower, g[..., :, None] - g[..., None, :], -jnp.inf))
    k_beta = k * beta[..., None]
    a_mat = jnp.where(strict, jnp.einsum("bhnid,bhnjd->bhnij", k_beta, k) * decay, 0.0)
    eye = jnp.eye(c, dtype=f32)
    t_mat = lax.linalg.triangular_solve(a_mat + eye, jnp.broadcast_to(eye, a_mat.shape),
                                        left_side=True, lower=True, unit_diagonal=True)
    u = t_mat @ (v * beta[..., None])
    w = t_mat @ (k_beta * jnp.exp(g)[..., None])
    qk = jnp.einsum("bhnid,bhnjd->bhnij", q, k) * decay
    q_dec = q * jnp.exp(g)[..., None]
    g_last = g[..., -1:]
    k_dec = k * jnp.exp(g_last - g)[..., None]
    last = jnp.exp(g_last)[..., None]

    def step(state, inp):
        u_n, w_n, qd_n, kd_n, qk_n, last_n = inp
        v_new = u_n - w_n @ state
        o_n = qd_n @ state + qk_n @ v_new
        state = state * last_n + jnp.swapaxes(kd_n, -1, -2) @ v_new
        return state, o_n

    xs = tuple(jnp.moveaxis(t, 2, 0) for t in (u, w, q_dec, k_dec, qk, last))
    _, o = lax.scan(step, jnp.zeros((b, h, d, d), f32), xs)
    return jnp.moveaxis(o, 0, 2).reshape(b, h, s, d)


def _mixer_gated_deltanet(x, w_in, conv_w, a_log, dt_bias, norm_w, w_out):
    f32 = jnp.float32
    proj = x @ w_in
    qkv, gate, a, beta_logit = jnp.split(proj, DN_SPLITS, axis=-1)
    qkv = jax.nn.silu(_causal_conv(qkv, conv_w))
    q, k, v = jnp.split(qkv, 3, axis=-1)
    q = _l2norm(_heads(q, DN_HEADS).astype(f32)) * (HEAD_DIM ** -0.5)
    k = _l2norm(_heads(k, DN_HEADS).astype(f32))
    v = _heads(v, DN_HEADS).astype(f32)
    beta = jax.nn.sigmoid(beta_logit.astype(f32)).transpose(0, 2, 1)
    g = (-jnp.exp(a_log.astype(f32)) * jax.nn.softplus(a.astype(f32) + dt_bias.astype(f32))).transpose(0, 2, 1)
    o = _gated_delta_rule(q, k, v, g, beta)
    o = o * lax.rsqrt(jnp.mean(o * o, axis=-1, keepdims=True) + NORM_EPS) * norm_w.astype(f32)
    o = _merge(o).astype(x.dtype) * jax.nn.silu(gate)
    return o @ w_out


def setup_inputs(seed: int = 0) -> dict:
    key = jax.random.key(seed)
    ks = jax.random.split(key, 16)
    f32 = jnp.float32

    def normal(k, shape, scale):
        return jax.random.normal(k, shape, f32) * scale

    x = normal(ks[0], (BATCH, SEQ, D_MODEL), 1.0)
    ffn_w_gate = normal(ks[1], (DEPTH, 2, D_MODEL, D_FF), D_MODEL ** -0.5)
    ffn_w_up = normal(ks[2], (DEPTH, 2, D_MODEL, D_FF), D_MODEL ** -0.5 * DEEPNORM_BETA)
    ffn_w_down = normal(ks[3], (DEPTH, 2, D_FF, D_MODEL), D_FF ** -0.5 * DEEPNORM_BETA)
    ln_w = 1.0 + normal(ks[4], (DEPTH, 3, D_MODEL), 0.02)
    ln_b = normal(ks[5], (DEPTH, 3, D_MODEL), 0.02)
    ab_scale = jnp.concatenate([
        jnp.ones((2 * RET_WIDTH,), f32), jnp.full((RET_WIDTH,), DEEPNORM_BETA, f32),
        jnp.ones((RET_WIDTH + 2 * ATT_WIDTH,), f32), jnp.full((ATT_WIDTH,), DEEPNORM_BETA, f32)])
    ab_w_in = normal(ks[6], (N_EVEN, D_MODEL, AB_IN), D_MODEL ** -0.5) * ab_scale
    ab_gn_w = 1.0 + normal(ks[7], (N_EVEN, RET_WIDTH), 0.02)
    ab_w_out = normal(ks[8], (N_EVEN, AB_OUT, D_MODEL), AB_OUT ** -0.5 * DEEPNORM_BETA)
    dn_scale = jnp.concatenate([
        jnp.ones((2 * DN_WIDTH,), f32), jnp.full((DN_WIDTH,), DEEPNORM_BETA, f32),
        jnp.ones((DN_WIDTH + 2 * DN_HEADS,), f32)])
    dn_w_in = normal(ks[9], (N_ODD, D_MODEL, DN_IN), D_MODEL ** -0.5) * dn_scale
    dn_conv_w = normal(ks[10], (N_ODD, DN_CONV, 3 * DN_WIDTH), DN_CONV ** -0.5)
    dn_a_log = jnp.log(jax.random.uniform(ks[11], (N_ODD, DN_HEADS), f32, 1.0, 16.0))
    dt = jnp.exp(jax.random.uniform(ks[12], (N_ODD, DN_HEADS), f32, math.log(1e-3), math.log(1e-1)))
    dn_dt_bias = dt + jnp.log(-jnp.expm1(-dt))
    dn_norm_w = 1.0 + normal(ks[13], (N_ODD, HEAD_DIM), 0.02)
    dn_w_out = normal(ks[14], (N_ODD, DN_WIDTH, D_MODEL), DN_WIDTH ** -0.5 * DEEPNORM_BETA)
    return {"x": x, "ffn_w_gate": ffn_w_gate, "ffn_w_up": ffn_w_up, "ffn_w_down": ffn_w_down,
            "ln_w": ln_w, "ln_b": ln_b, "ab_w_in": ab_w_in, "ab_gn_w": ab_gn_w, "ab_w_out": ab_w_out,
            "dn_w_in": dn_w_in, "dn_conv_w": dn_conv_w, "dn_a_log": dn_a_log, "dn_dt_bias": dn_dt_bias,
            "dn_norm_w": dn_norm_w, "dn_w_out": dn_w_out}


def reference(x, ffn_w_gate, ffn_w_up, ffn_w_down, ln_w, ln_b, ab_w_in, ab_gn_w, ab_w_out,
              dn_w_in, dn_conv_w, dn_a_log, dn_dt_bias, dn_norm_w, dn_w_out):
    for l in range(DEPTH):
        h = _swiglu(x, ffn_w_gate[l, 0], ffn_w_up[l, 0], ffn_w_down[l, 0])
        x = _layer_norm(DEEPNORM_ALPHA * x + 0.5 * h, ln_w[l, 0], ln_b[l, 0])
        if l % 2 == 0:
            m = _mixer_retention_swa(x, ab_w_in[l // 2], ab_gn_w[l // 2], ab_w_out[l // 2])
        else:
            i = l // 2
            m = _mixer_gated_deltanet(x, dn_w_in[i], dn_conv_w[i], dn_a_log[i], dn_dt_bias[i],
                                      dn_norm_w[i], dn_w_out[i])
        x = _layer_norm(DEEPNORM_ALPHA * x + m, ln_w[l, 1], ln_b[l, 1])
        h = _swiglu(x, ffn_w_gate[l, 1], ffn_w_up[l, 1], ffn_w_down[l, 1])
        x = _layer_norm(DEEPNORM_ALPHA * x + 0.5 * h, ln_w[l, 2], ln_b[l, 2])
    return x
```

```python
import functools
import math

import jax
import jax.numpy as jnp
from jax import lax
from jax.experimental import pallas as pl
from jax.experimental.pallas import tpu as pltpu

F32 = jnp.float32
MXU_DTYPE = jnp.bfloat16
HEAD_DIM = 128
LN_EPS = 1e-5
NORM_EPS = 1e-6
ROPE_BASE = 10000.0
SWA_PATTERNS = ((128, 1), (512, 4), (2048, 16))
SWA_FAR = max(w for w, _ in SWA_PATTERNS)
DN_CHUNK = 128
RET_CHUNK = 128
MASKED = -1e30
V7X_VMEM_LIMIT_BYTES = 56 * 1024 * 1024


def _cparams(sem):
    return pltpu.CompilerParams(dimension_semantics=sem, vmem_limit_bytes=V7X_VMEM_LIMIT_BYTES)


def _mm(a, b):
    return jnp.dot(a.astype(MXU_DTYPE), b.astype(MXU_DTYPE), preferred_element_type=F32)


def _mm_nt(a, b):
    return lax.dot_general(a.astype(MXU_DTYPE), b.astype(MXU_DTYPE), (((1,), (1,)), ((), ())),
                           preferred_element_type=F32)


def _mm_tn(a, b):
    return lax.dot_general(a.astype(MXU_DTYPE), b.astype(MXU_DTYPE), (((0,), (0,)), ((), ())),
                           preferred_element_type=F32)


def _silu(x):
    return x * jax.nn.sigmoid(x)


def _layer_norm(y, w, b):
    mu = jnp.mean(y, axis=-1, keepdims=True)
    yc = y - mu
    var = jnp.mean(yc * yc, axis=-1, keepdims=True)
    return yc * lax.rsqrt(var + LN_EPS) * w + b


def _tile(n, pref, unit):
    if n <= pref:
        return n
    t = (pref // unit) * unit
    while t > unit and n % t:
        t -= unit
    assert n % t == 0, (n, pref, unit)
    return t


def _residual_ln_rows(x_ref, o_ref, lnw_ref, lnb_ref, alpha, branch_scale, rows):
    tm = o_ref.shape[0]

    def body(r, carry):
        sl = pl.ds(pl.multiple_of(r * rows, rows), rows)
        y = alpha * x_ref[sl, :] + branch_scale * o_ref[sl, :]
        o_ref[sl, :] = _layer_norm(y, lnw_ref[...], lnb_ref[...])
        return carry

    lax.fori_loop(0, tm // rows, body, 0)


def _ffn_kernel(x_ref, wgu_ref, wd_ref, lnw_ref, lnb_ref, o_ref, xb_ref, *, alpha, tf, n_split, rows):
    j = pl.program_id(1)

    @pl.when(j == 0)
    def _():
        xb_ref[...] = x_ref[...].astype(xb_ref.dtype)
        o_ref[...] = jnp.zeros_like(o_ref)

    gu = jnp.dot(xb_ref[...], wgu_ref[...], preferred_element_type=F32)
    h = (_silu(gu[:, :tf]) * gu[:, tf:]).astype(xb_ref.dtype)
    dn = o_ref.shape[1] // n_split
    for n in range(n_split):
        cols = slice(n * dn, (n + 1) * dn)
        o_ref[:, cols] += jnp.dot(h, wd_ref[:, cols], preferred_element_type=F32)

    @pl.when(j == pl.num_programs(1) - 1)
    def _():
        _residual_ln_rows(x_ref, o_ref, lnw_ref, lnb_ref, alpha, 0.5, rows)


def _ffn_ln(x, wg, wu, wd, lnw, lnb, alpha):
    T, D = x.shape
    F = wg.shape[1]
    tm = _tile(T, 512, 8)
    tf = _tile(F, 256, 128)
    nf = F // tf
    wgu = jnp.stack([wg.reshape(D, nf, tf), wu.reshape(D, nf, tf)], axis=2).reshape(D, 2 * F).astype(MXU_DTYPE)
    wdb = wd.astype(MXU_DTYPE)
    kern = functools.partial(_ffn_kernel, alpha=alpha, tf=tf, n_split=max(D // 512, 1), rows=min(tm, 32))
    return pl.pallas_call(
        kern,
        grid=(T // tm, nf),
        in_specs=[
            pl.BlockSpec((tm, D), lambda i, j: (i, 0)),
            pl.BlockSpec((D, 2 * tf), lambda i, j: (0, j)),
            pl.BlockSpec((tf, D), lambda i, j: (j, 0)),
            pl.BlockSpec((1, D), lambda i, j: (0, 0)),
            pl.BlockSpec((1, D), lambda i, j: (0, 0)),
        ],
        out_specs=pl.BlockSpec((tm, D), lambda i, j: (i, 0)),
        out_shape=jax.ShapeDtypeStruct((T, D), F32),
        scratch_shapes=[pltpu.VMEM((tm, D), MXU_DTYPE)],
        compiler_params=_cparams(("parallel", "arbitrary")),
        name="ffn_ln",
    )(x, wgu, wdb, lnw.reshape(1, D), lnb.reshape(1, D))


def _proj_kernel(x_ref, w_ref, o_ref, xb_ref):
    @pl.when(pl.program_id(1) == 0)
    def _():
        xb_ref[...] = x_ref[...].astype(xb_ref.dtype)

    o_ref[...] = jnp.dot(xb_ref[...], w_ref[...], preferred_element_type=F32).astype(o_ref.dtype)


def _proj(x, w, out_dtype):
    T, K = x.shape
    N = w.shape[1]
    tm = _tile(T, 512, 8)
    tn = _tile(N, 1024, 128)
    return pl.pallas_call(
        _proj_kernel,
        grid=(T // tm, N // tn),
        in_specs=[pl.BlockSpec((tm, K), lambda i, j: (i, 0)),
                  pl.BlockSpec((K, tn), lambda i, j: (0, j))],
        out_specs=pl.BlockSpec((tm, tn), lambda i, j: (i, j)),
        out_shape=jax.ShapeDtypeStruct((T, N), out_dtype),
        scratch_shapes=[pltpu.VMEM((tm, K), MXU_DTYPE)],
        compiler_params=_cparams(("parallel", "arbitrary")),
        name="in_proj",
    )(x, w.astype(MXU_DTYPE))


def _proj_ln_kernel(a_ref, w_ref, x_ref, lnw_ref, lnb_ref, o_ref, *, alpha, rows):
    k = pl.program_id(1)

    @pl.when(k == 0)
    def _():
        o_ref[...] = jnp.zeros_like(o_ref)

    o_ref[...] += jnp.dot(a_ref[...], w_ref[...], preferred_element_type=F32)

    @pl.when(k == pl.num_programs(1) - 1)
    def _():
        _residual_ln_rows(x_ref, o_ref, lnw_ref, lnb_ref, alpha, 1.0, rows)


def _proj_ln(a, w, x, lnw, lnb, alpha):
    T, K = a.shape
    D = w.shape[1]
    tm = _tile(T, 512, 8)
    tk = _tile(K, 512, 128)
    kern = functools.partial(_proj_ln_kernel, alpha=alpha, rows=min(tm, 32))
    return pl.pallas_call(
        kern,
        grid=(T // tm, K // tk),
        in_specs=[
            pl.BlockSpec((tm, tk), lambda i, k: (i, k)),
            pl.BlockSpec((tk, D), lambda i, k: (k, 0)),
            pl.BlockSpec((tm, D), lambda i, k: (i, 0)),
            pl.BlockSpec((1, D), lambda i, k: (0, 0)),
            pl.BlockSpec((1, D), lambda i, k: (0, 0)),
        ],
        out_specs=pl.BlockSpec((tm, D), lambda i, k: (i, 0)),
        out_shape=jax.ShapeDtypeStruct((T, D), F32),
        compiler_params=_cparams(("parallel", "arbitrary")),
        name="out_proj_ln",
    )(a, w.astype(MXU_DTYPE), x, lnw.reshape(1, D), lnb.reshape(1, D))


def _retention_kernel(q_ref, k_ref, v_ref, g_ref, cos_ref, sin_ref, lg_ref, gnw_ref, o_ref, st_ref, *, n_chunks):
    c = RET_CHUNK

    @pl.when(pl.program_id(2) == 0)
    def _():
        st_ref[...] = jnp.zeros_like(st_ref)

    lg = lg_ref[0]
    ii = lax.broadcasted_iota(jnp.int32, (c, c), 0)
    jj = lax.broadcasted_iota(jnp.int32, (c, c), 1)
    rel = (ii - jj).astype(F32)
    dmask = jnp.where(rel >= 0, jnp.exp(lg * jnp.maximum(rel, 0.0)), 0.0)
    row = ii.astype(F32)
    zeta = jnp.exp(lg * (c - 1 - row))
    xi = jnp.exp(lg * (row + 1.0))
    chunk_decay = jnp.exp(lg * c)
    for n in range(n_chunks):
        sl = slice(n * c, (n + 1) * c)
        cs, sn = cos_ref[sl, :], sin_ref[sl, :]
        q, k, v = q_ref[sl, :], k_ref[sl, :], v_ref[sl, :]
        qr = q * cs + pltpu.roll(q, HEAD_DIM // 2, 1) * sn
        kr = (k * cs + pltpu.roll(k, HEAD_DIM // 2, 1) * sn) * (HEAD_DIM ** -0.5)
        st = st_ref[...]
        y = _mm(_mm_nt(qr, kr) * dmask, v) + _mm(qr, st) * xi
        st_ref[...] = st * chunk_decay + _mm_tn(kr * zeta, v)
        mu = jnp.mean(y, axis=-1, keepdims=True)
        yc = y - mu
        var = jnp.mean(yc * yc, axis=-1, keepdims=True)
        yn = yc * lax.rsqrt(var + LN_EPS) * gnw_ref[...]
        o_ref[sl, :] = (yn * _silu(g_ref[sl, :])).astype(o_ref.dtype)


def _retention(proj, cos2, sin2, log_g, gn_w, batch, seq, heads):
    T = batch * seq
    ts = _tile(seq, 1024, RET_CHUNK)
    ns = seq // ts
    blk = lambda off: pl.BlockSpec((ts, HEAD_DIM), lambda b, h, s: (b * ns + s, off * heads + h))
    tab = pl.BlockSpec((ts, HEAD_DIM), lambda b, h, s: (s, 0))
    kern = functools.partial(_retention_kernel, n_chunks=ts // RET_CHUNK)
    return pl.pallas_call(
        kern,
        grid=(batch, heads, ns),
        in_specs=[blk(0), blk(1), blk(2), blk(3), tab, tab,
                  pl.BlockSpec((1, 1, HEAD_DIM), lambda b, h, s: (h, 0, 0)),
                  pl.BlockSpec((1, HEAD_DIM), lambda b, h, s: (0, h))],
        out_specs=pl.BlockSpec((ts, HEAD_DIM), lambda b, h, s: (b * ns + s, h)),
        out_shape=jax.ShapeDtypeStruct((T, heads * HEAD_DIM), MXU_DTYPE),
        scratch_shapes=[pltpu.VMEM((HEAD_DIM, HEAD_DIM), F32)],
        compiler_params=_cparams(("parallel", "parallel", "arbitrary")),
        name="retention",
    )(proj, proj, proj, proj, cos2, sin2, log_g, gn_w.reshape(1, heads * HEAD_DIM))


def _swa_kernel(q_ref, k_ref, v_ref, bias_ref, o_ref, *, tq, win):
    qi = pl.program_id(2)
    start = pl.multiple_of(jnp.maximum(qi * tq - SWA_FAR, 0), tq)
    kw = k_ref[pl.ds(start, win), :]
    vw = v_ref[pl.ds(start, win), :]
    s = _mm_nt(q_ref[...], kw) * (HEAD_DIM ** -0.5) + bias_ref[0]
    m = jnp.max(s, axis=-1, keepdims=True)
    p = jnp.exp(s - m)
    l = jnp.sum(p, axis=-1, keepdims=True)
    o_ref[...] = (_mm(p, vw) / l).astype(o_ref.dtype)


def _swa_bias(tq, win):
    nvar = SWA_FAR // tq + 1
    v = lax.broadcasted_iota(jnp.int32, (nvar, tq, win), 0)
    i = lax.broadcasted_iota(jnp.int32, (nvar, tq, win), 1)
    c = lax.broadcasted_iota(jnp.int32, (nvar, tq, win), 2)
    dist = SWA_FAR - v * tq + i - c
    mult = jnp.zeros((nvar, tq, win), F32)
    for window, dilation in SWA_PATTERNS:
        mult += ((dist >= 0) & (dist <= window) & (dist % dilation == 0)).astype(F32)
    return jnp.where(mult > 0, jnp.log(jnp.maximum(mult, 1.0)), MASKED)


def _swa(proj, batch, seq, heads):
    T = batch * seq
    tq = _tile(seq, 256, 128)
    assert SWA_FAR % tq == 0 and seq >= SWA_FAR + tq
    win = SWA_FAR + tq
    nq = seq // tq
    nfar = SWA_FAR // tq
    kern = functools.partial(_swa_kernel, tq=tq, win=win)
    return pl.pallas_call(
        kern,
        grid=(batch, heads, nq),
        in_specs=[
            pl.BlockSpec((tq, HEAD_DIM), lambda b, h, i: (b * nq + i, h)),
            pl.BlockSpec((seq, HEAD_DIM), lambda b, h, i: (b, heads + h)),
            pl.BlockSpec((seq, HEAD_DIM), lambda b, h, i: (b, 2 * heads + h)),
            pl.BlockSpec((1, tq, win), lambda b, h, i: (jnp.maximum(nfar - i, 0), 0, 0)),
        ],
        out_specs=pl.BlockSpec((tq, HEAD_DIM), lambda b, h, i: (b * nq + i, h)),
        out_shape=jax.ShapeDtypeStruct((T, heads * HEAD_DIM), MXU_DTYPE),
        compiler_params=_cparams(("parallel", "parallel", "arbitrary")),
        name="dilated_swa",
    )(proj, proj, proj, _swa_bias(tq, win))


def _dn_prep_kernel(x_ref, halo_ref, w_ref, o_ref, *, n_q_blocks):
    s = pl.program_id(1)
    cb = pl.program_id(2)
    ts, width = x_ref.shape
    x = x_ref[...]
    halo = jnp.where(s == 0, 0.0, halo_ref[...])
    xp = jnp.concatenate([halo, x], axis=0)
    w = w_ref[...]
    kw = w.shape[0]
    acc = x * w[kw - 1:kw, :]
    for i in range(kw - 1):
        off = 8 - (kw - 1) + i
        acc = acc + xp[off:off + ts, :] * w[i:i + 1, :]
    y = _silu(acc)
    is_q = cb < n_q_blocks
    is_qk = cb < 2 * n_q_blocks
    for hh in range(width // HEAD_DIM):
        cols = slice(hh * HEAD_DIM, (hh + 1) * HEAD_DIM)
        yh = y[:, cols]
        inv = lax.rsqrt(jnp.sum(yh * yh, axis=-1, keepdims=True) + NORM_EPS)
        inv = jnp.where(is_q, inv * (HEAD_DIM ** -0.5), inv)
        o_ref[:, cols] = yh * jnp.where(is_qk, inv, 1.0)


def _dn_prep(qkv, conv_w, batch, seq, width):
    T = batch * seq
    ts = _tile(seq, 512, 8)
    cw = _tile(width, 512, HEAD_DIM)
    ns = seq // ts
    kern = functools.partial(_dn_prep_kernel, n_q_blocks=width // cw)
    return pl.pallas_call(
        kern,
        grid=(batch, ns, 3 * width // cw),
        in_specs=[
            pl.BlockSpec((ts, cw), lambda b, s, c: (b * ns + s, c)),
            pl.BlockSpec((8, cw), lambda b, s, c: (jnp.maximum((b * ns + s) * (ts // 8) - 1, 0), c)),
            pl.BlockSpec((conv_w.shape[0], cw), lambda b, s, c: (0, c)),
        ],
        out_specs=pl.BlockSpec((ts, cw), lambda b, s, c: (b * ns + s, c)),
        out_shape=jax.ShapeDtypeStruct((T, 3 * width), F32),
        compiler_params=_cparams(("parallel", "parallel", "parallel")),
        name="dn_conv_norm",
    )(qkv, qkv, conv_w)


def _softplus(x):
    return jnp.maximum(x, 0.0) + jnp.log1p(jnp.exp(-jnp.abs(x)))


def _dn_gates_kernel(x_ref, w_ref, wt_ref, pc_ref, pr_ref, col_ref, row_ref, *, heads):
    c = DN_CHUNK
    xb = x_ref[...].astype(MXU_DTYPE)
    col = jnp.dot(xb, w_ref[...], preferred_element_type=F32)
    row = lax.dot_general(wt_ref[...], xb, (((1,), (1,)), ((), ())), preferred_element_type=F32)
    ii = lax.broadcasted_iota(jnp.int32, (c, c), 0)
    jj = lax.broadcasted_iota(jnp.int32, (c, c), 1)
    incl = (ii >= jj).astype(F32)
    g_col = -jnp.exp(pc_ref[0:1, :]) * _softplus(col + pc_ref[1:2, :])
    gc_col = jnp.dot(incl, g_col, preferred_element_type=F32, precision=lax.Precision.HIGHEST)
    lane = lax.broadcasted_iota(jnp.int32, col.shape, 1)
    col_ref[...] = jnp.where(lane < heads, gc_col, jax.nn.sigmoid(col))
    g_row = -jnp.exp(pr_ref[:, 0:1]) * _softplus(row + pr_ref[:, 1:2])
    gc_row = lax.dot_general(g_row, incl, (((1,), (1,)), ((), ())), preferred_element_type=F32,
                             precision=lax.Precision.HIGHEST)
    sub = lax.broadcasted_iota(jnp.int32, row.shape, 0)
    row_ref[...] = jnp.where(sub < heads, gc_row, jax.nn.sigmoid(row))


def _dn_gates(x, w_small, a_log, dt_bias, heads):
    T, D = x.shape
    assert 2 * heads <= 128
    pad = 128 - 2 * heads
    w = jnp.pad(w_small, ((0, 0), (0, pad))).astype(MXU_DTYPE)
    zeros = jnp.zeros((128 - heads,), F32)
    params = jnp.stack([jnp.concatenate([a_log.astype(F32), zeros]), jnp.concatenate([dt_bias.astype(F32), zeros])])
    c = DN_CHUNK
    kern = functools.partial(_dn_gates_kernel, heads=heads)
    return pl.pallas_call(
        kern,
        grid=(T // c,),
        in_specs=[
            pl.BlockSpec((c, D), lambda i: (i, 0)),
            pl.BlockSpec((D, 128), lambda i: (0, 0)),
            pl.BlockSpec((128, D), lambda i: (0, 0)),
            pl.BlockSpec((2, 128), lambda i: (0, 0)),
            pl.BlockSpec((128, 2), lambda i: (0, 0)),
        ],
        out_specs=[pl.BlockSpec((c, 128), lambda i: (i, 0)), pl.BlockSpec((128, c), lambda i: (0, i))],
        out_shape=[jax.ShapeDtypeStruct((T, 128), F32), jax.ShapeDtypeStruct((128, T), F32)],
        compiler_params=_cparams(("parallel",)),
        name="dn_gates",
    )(x, w, w.T, params, params.T)


def _inv_unit_lower(a, ii, jj):
    c = a.shape[0]
    t = (ii == jj).astype(F32) - jnp.where((ii >> 1) == (jj >> 1), a, 0.0)
    for ls in range(1, int(math.log2(c))):
        lower_left = (((ii >> (ls + 1)) == (jj >> (ls + 1))) & (((ii >> ls) & 1) == 1) & (((jj >> ls) & 1) == 0))
        x = _mm(jnp.where(lower_left, a, 0.0), t)
        t = t - _mm(t, x)
    return t


def _dn_kernel(q_ref, k_ref, v_ref, gate_ref, col_ref, row_ref, nw_ref, o_ref, st_ref, *, group, heads):
    c = DN_CHUNK

    @pl.when(pl.program_id(2) == 0)
    def _():
        st_ref[...] = jnp.zeros_like(st_ref)

    ii = lax.broadcasted_iota(jnp.int32, (c, c), 0)
    jj = lax.broadcasted_iota(jnp.int32, (c, c), 1)
    col = col_ref[...]
    lane = lax.broadcasted_iota(jnp.int32, col.shape, 1)
    for hh in range(group):
        h = pl.program_id(1) * group + hh
        cols = slice(hh * HEAD_DIM, (hh + 1) * HEAD_DIM)
        q, k, v = q_ref[:, cols], k_ref[:, cols], v_ref[:, cols]
        gc_i = jnp.sum(jnp.where(lane == h, col, 0.0), axis=-1, keepdims=True)
        beta = jnp.sum(jnp.where(lane == h + heads, col, 0.0), axis=-1, keepdims=True)
        gc_j = row_ref[pl.ds(h, 1), :]
        gc_last = gc_i[c - 1:c, :]
        decay = jnp.exp(jnp.where(ii >= jj, gc_i - gc_j, MASKED))
        e_g = jnp.exp(gc_i)
        kb = k * beta
        a = jnp.where(ii > jj, _mm_nt(kb, k) * decay, 0.0)
        t = _inv_unit_lower(a, ii, jj)
        u = _mm(t, v * beta)
        w = _mm(t, kb * e_g)
        qk = _mm_nt(q, k) * decay
        st = st_ref[hh]
        v_new = u - _mm(w, st)
        o = _mm(q * e_g, st) + _mm(qk, v_new)
        st_ref[hh] = st * jnp.exp(gc_last) + _mm_tn(k * jnp.exp(gc_last - gc_i), v_new)
        o = o * lax.rsqrt(jnp.mean(o * o, axis=-1, keepdims=True) + NORM_EPS) * nw_ref[...]
        o_ref[:, cols] = (o * _silu(gate_ref[:, cols])).astype(o_ref.dtype)


def _delta_rule(qkv, gate, g_col, g_row, norm_w, batch, seq, heads):
    T = batch * seq
    c = DN_CHUNK
    group = _tile(heads, 4, 1)
    ng = heads // group
    nc = seq // c
    gw = group * HEAD_DIM
    blk = lambda off: pl.BlockSpec((c, gw), lambda b, g, n: (b * nc + n, off * ng + g))
    kern = functools.partial(_dn_kernel, group=group, heads=heads)
    return pl.pallas_call(
        kern,
        grid=(batch, ng, nc),
        in_specs=[blk(0), blk(1), blk(2),
                  pl.BlockSpec((c, gw), lambda b, g, n: (b * nc + n, g)),
                  pl.BlockSpec((c, 128), lambda b, g, n: (b * nc + n, 0)),
                  pl.BlockSpec((128, c), lambda b, g, n: (0, b * nc + n)),
                  pl.BlockSpec((1, HEAD_DIM), lambda b, g, n: (0, 0))],
        out_specs=pl.BlockSpec((c, gw), lambda b, g, n: (b * nc + n, g)),
        out_shape=jax.ShapeDtypeStruct((T, heads * HEAD_DIM), MXU_DTYPE),
        scratch_shapes=[pltpu.VMEM((group, HEAD_DIM, HEAD_DIM), F32)],
        compiler_params=_cparams(("parallel", "parallel", "arbitrary")),
        name="delta_rule",
    )(qkv, qkv, qkv, gate, g_col, g_row, norm_w.reshape(1, HEAD_DIM))


def _rope_tables(seq):
    pos = jnp.arange(seq, dtype=F32)
    inv_freq = 1.0 / (ROPE_BASE ** jnp.linspace(0.0, 1.0, HEAD_DIM // 2, dtype=F32))
    ang = pos[:, None] * inv_freq[None, :]
    cos, sin = jnp.cos(ang), jnp.sin(ang)
    return jnp.concatenate([cos, cos], axis=1), jnp.concatenate([-sin, sin], axis=1)


def _mixer_retention_swa(x, w_in, gn_w, w_out, lnw, lnb, alpha, batch, seq):
    D = x.shape[1]
    width = D // 2
    heads = width // HEAD_DIM
    perm = jnp.concatenate([jnp.arange(0, HEAD_DIM, 2), jnp.arange(1, HEAD_DIM, 2)])
    qk_cols = (jnp.arange(heads)[:, None] * HEAD_DIM + perm[None, :]).reshape(-1)
    cols = jnp.concatenate([qk_cols, width + qk_cols, jnp.arange(2 * width, 4 * width)])
    proj_ret = _proj(x, jnp.take(w_in[:, :4 * width], cols, axis=1), F32)
    proj_att = _proj(x, w_in[:, 4 * width:], MXU_DTYPE)
    cos2, sin2 = _rope_tables(seq)
    log_g = jnp.log1p(-jnp.exp2(-5.0 - jnp.arange(heads, dtype=F32)))
    log_g = jnp.broadcast_to(log_g[:, None, None], (heads, 1, HEAD_DIM))
    y_ret = _retention(proj_ret, cos2, sin2, log_g, gn_w, batch, seq, heads)
    y_att = _swa(proj_att, batch, seq, heads)
    return _proj_ln(jnp.concatenate([y_ret, y_att], axis=1), w_out, x, lnw, lnb, alpha)


def _mixer_gated_deltanet(x, w_in, conv_w, a_log, dt_bias, norm_w, w_out, lnw, lnb, alpha, batch, seq):
    D = x.shape[1]
    heads = D // HEAD_DIM
    qkv = _proj(x, w_in[:, :3 * D], F32)
    gate = _proj(x, w_in[:, 3 * D:4 * D], F32)
    g_col, g_row = _dn_gates(x, w_in[:, 4 * D:], a_log, dt_bias, heads)
    qkv = _dn_prep(qkv, conv_w, batch, seq, D)
    o = _delta_rule(qkv, gate, g_col, g_row, norm_w, batch, seq, heads)
    return _proj_ln(o, w_out, x, lnw, lnb, alpha)


def kernel(x, ffn_w_gate, ffn_w_up, ffn_w_down, ln_w, ln_b, ab_w_in, ab_gn_w, ab_w_out,
           dn_w_in, dn_conv_w, dn_a_log, dn_dt_bias, dn_norm_w, dn_w_out):
    batch, seq, D = x.shape
    depth = ffn_w_gate.shape[0]
    alpha = (2.0 * depth) ** 0.25
    h = x.reshape(batch * seq, D)
    for l in range(depth):
        h = _ffn_ln(h, ffn_w_gate[l, 0], ffn_w_up[l, 0], ffn_w_down[l, 0], ln_w[l, 0], ln_b[l, 0], alpha)
        i = l // 2
        if l % 2 == 0:
            h = _mixer_retention_swa(h, ab_w_in[i], ab_gn_w[i], ab_w_out[i], ln_w[l, 1], ln_b[l, 1], alpha,
                                     batch, seq)
        else:
            h = _mixer_gated_deltanet(h, dn_w_in[i], dn_conv_w[i], dn_a_log[i], dn_dt_bias[i], dn_norm_w[i],
                                      dn_w_out[i], ln_w[l, 1], ln_b[l, 1], alpha, batch, seq)
        h = _ffn_ln(h, ffn_w_gate[l, 1], ffn_w_up[l, 1], ffn_w_down[l, 1], ln_w[l, 2], ln_b[l, 2], alpha)
    return h.reshape(batch, seq, D)
```

```python
import functools
import math

import jax
import jax.numpy as jnp
from jax import lax
from jax.experimental import pallas as pl
from jax.experimental.pallas import tpu as pltpu

F32 = jnp.float32
MXU_DTYPE = jnp.bfloat16
HEAD_DIM = 128
LN_EPS = 1e-5
NORM_EPS = 1e-6
ROPE_BASE = 10000.0
SWA_PATTERNS = ((128, 1), (512, 4), (2048, 16))
SWA_FAR = max(w for w, _ in SWA_PATTERNS)
DN_CHUNK = 128
RET_CHUNK = 128
MASKED = -1e30
V7X_VMEM_LIMIT_BYTES = 56 * 1024 * 1024


def _cparams(sem):
    return pltpu.CompilerParams(dimension_semantics=sem, vmem_limit_bytes=V7X_VMEM_LIMIT_BYTES)


def _mm(a, b):
    return jnp.dot(a.astype(MXU_DTYPE), b.astype(MXU_DTYPE), preferred_element_type=F32)


def _mm_nt(a, b):
    return lax.dot_general(a.astype(MXU_DTYPE), b.astype(MXU_DTYPE), (((1,), (1,)), ((), ())),
                           preferred_element_type=F32)


def _mm_tn(a, b):
    return lax.dot_general(a.astype(MXU_DTYPE), b.astype(MXU_DTYPE), (((0,), (0,)), ((), ())),
                           preferred_element_type=F32)


def _silu(x):
    return x * jax.nn.sigmoid(x)


def _layer_norm(y, w, b):
    mu = jnp.mean(y, axis=-1, keepdims=True)
    yc = y - mu
    var = jnp.mean(yc * yc, axis=-1, keepdims=True)
    return yc * lax.rsqrt(var + LN_EPS) * w + b


def _tile(n, pref, unit):
    if n <= pref:
        return n
    t = (pref // unit) * unit
    while t > unit and n % t:
        t -= unit
    assert n % t == 0, (n, pref, unit)
    return t


def _residual_ln_rows(x_ref, o_ref, lnw_ref, lnb_ref, alpha, branch_scale, rows):
    tm = o_ref.shape[0]

    def body(r, carry):
        sl = pl.ds(pl.multiple_of(r * rows, rows), rows)
        y = alpha * x_ref[sl, :] + branch_scale * o_ref[sl, :]
        o_ref[sl, :] = _layer_norm(y, lnw_ref[...], lnb_ref[...])
        return carry

    lax.fori_loop(0, tm // rows, body, 0)


def _ffn_kernel(x_ref, wg_ref, wu_ref, wd_ref, lnw_ref, lnb_ref, o_ref, xb_ref, *, alpha, n_split, rows):
    j = pl.program_id(1)

    @pl.when(j == 0)
    def _():
        xb_ref[...] = x_ref[...].astype(xb_ref.dtype)
        o_ref[...] = jnp.zeros_like(o_ref)

    xb = xb_ref[...]
    g = jnp.dot(xb, wg_ref[...], preferred_element_type=F32)
    u = jnp.dot(xb, wu_ref[...], preferred_element_type=F32)
    h = (_silu(g) * u).astype(xb_ref.dtype)
    dn = o_ref.shape[1] // n_split
    for n in range(n_split):
        cols = slice(n * dn, (n + 1) * dn)
        o_ref[:, cols] += jnp.dot(h, wd_ref[:, cols], preferred_element_type=F32)

    @pl.when(j == pl.num_programs(1) - 1)
    def _():
        _residual_ln_rows(x_ref, o_ref, lnw_ref, lnb_ref, alpha, 0.5, rows)


def _ffn_ln(x, wg, wu, wd, l, k, lnw, lnb, alpha):
    T, D = x.shape
    F = wg.shape[-1]
    tm = _tile(T, 512, 8)
    tf = _tile(F, 256, 128)
    kern = functools.partial(_ffn_kernel, alpha=alpha, n_split=max(D // 512, 1), rows=min(tm, 32))
    return pl.pallas_call(
        kern,
        grid=(T // tm, F // tf),
        in_specs=[
            pl.BlockSpec((tm, D), lambda i, j: (i, 0)),
            pl.BlockSpec((None, None, D, tf), lambda i, j: (l, k, 0, j)),
            pl.BlockSpec((None, None, D, tf), lambda i, j: (l, k, 0, j)),
            pl.BlockSpec((None, None, tf, D), lambda i, j: (l, k, j, 0)),
            pl.BlockSpec((1, D), lambda i, j: (0, 0)),
            pl.BlockSpec((1, D), lambda i, j: (0, 0)),
        ],
        out_specs=pl.BlockSpec((tm, D), lambda i, j: (i, 0)),
        out_shape=jax.ShapeDtypeStruct((T, D), F32),
        scratch_shapes=[pltpu.VMEM((tm, D), MXU_DTYPE)],
        compiler_params=_cparams(("parallel", "arbitrary")),
        name="ffn_ln",
    )(x, wg, wu, wd, lnw.reshape(1, D), lnb.reshape(1, D))


def _proj_kernel(x_ref, w_ref, o_ref, xb_ref):
    @pl.when(pl.program_id(1) == 0)
    def _():
        xb_ref[...] = x_ref[...].astype(xb_ref.dtype)

    o_ref[...] = jnp.dot(xb_ref[...], w_ref[...], preferred_element_type=F32).astype(o_ref.dtype)


def _proj(x, w, col0, ncols, out_dtype):
    T, K = x.shape
    tm = _tile(T, 512, 8)
    tn = _tile(math.gcd(ncols, col0) if col0 else ncols, 1024, 128)
    j0 = col0 // tn
    return pl.pallas_call(
        _proj_kernel,
        grid=(T // tm, ncols // tn),
        in_specs=[pl.BlockSpec((tm, K), lambda i, j: (i, 0)),
                  pl.BlockSpec((K, tn), lambda i, j: (0, j0 + j))],
        out_specs=pl.BlockSpec((tm, tn), lambda i, j: (i, j)),
        out_shape=jax.ShapeDtypeStruct((T, ncols), out_dtype),
        scratch_shapes=[pltpu.VMEM((tm, K), MXU_DTYPE)],
        compiler_params=_cparams(("parallel", "arbitrary")),
        name="in_proj",
    )(x, w)


def _proj_ln_kernel(a_ref, w_ref, x_ref, lnw_ref, lnb_ref, o_ref, *, alpha, rows):
    k = pl.program_id(1)

    @pl.when(k == 0)
    def _():
        o_ref[...] = jnp.zeros_like(o_ref)

    o_ref[...] += jnp.dot(a_ref[...], w_ref[...], preferred_element_type=F32)

    @pl.when(k == pl.num_programs(1) - 1)
    def _():
        _residual_ln_rows(x_ref, o_ref, lnw_ref, lnb_ref, alpha, 1.0, rows)


def _proj_ln(a, w, x, lnw, lnb, alpha):
    T, K = a.shape
    D = w.shape[1]
    tm = _tile(T, 512, 8)
    tk = _tile(K, 512, 128)
    kern = functools.partial(_proj_ln_kernel, alpha=alpha, rows=min(tm, 32))
    return pl.pallas_call(
        kern,
        grid=(T // tm, K // tk),
        in_specs=[
            pl.BlockSpec((tm, tk), lambda i, k: (i, k)),
            pl.BlockSpec((tk, D), lambda i, k: (k, 0)),
            pl.BlockSpec((tm, D), lambda i, k: (i, 0)),
            pl.BlockSpec((1, D), lambda i, k: (0, 0)),
            pl.BlockSpec((1, D), lambda i, k: (0, 0)),
        ],
        out_specs=pl.BlockSpec((tm, D), lambda i, k: (i, 0)),
        out_shape=jax.ShapeDtypeStruct((T, D), F32),
        compiler_params=_cparams(("parallel", "arbitrary")),
        name="out_proj_ln",
    )(a, w, x, lnw.reshape(1, D), lnb.reshape(1, D))


def _retention_kernel(q_ref, k_ref, v_ref, g_ref, cos_ref, sin_ref, lg_ref, gnw_ref, o_ref, st_ref, *, n_chunks):
    c = RET_CHUNK

    @pl.when(pl.program_id(2) == 0)
    def _():
        st_ref[...] = jnp.zeros_like(st_ref)

    lg = lg_ref[0]
    ii = lax.broadcasted_iota(jnp.int32, (c, c), 0)
    jj = lax.broadcasted_iota(jnp.int32, (c, c), 1)
    rel = (ii - jj).astype(F32)
    dmask = jnp.where(rel >= 0, jnp.exp(lg * jnp.maximum(rel, 0.0)), 0.0)
    row = ii.astype(F32)
    zeta = jnp.exp(lg * (c - 1 - row))
    xi = jnp.exp(lg * (row + 1.0))
    chunk_decay = jnp.exp(lg * c)
    even_lane = (jj & 1) == 0

    def rotate(t, cs, sn):
        partner = jnp.where(even_lane, pltpu.roll(t, HEAD_DIM - 1, 1), pltpu.roll(t, 1, 1))
        return t * cs + partner * sn

    for n in range(n_chunks):
        sl = slice(n * c, (n + 1) * c)
        cs, sn = cos_ref[sl, :], sin_ref[sl, :]
        q, k, v = q_ref[sl, :], k_ref[sl, :], v_ref[sl, :]
        qr = rotate(q, cs, sn)
        kr = rotate(k, cs, sn) * (HEAD_DIM ** -0.5)
        st = st_ref[...]
        y = _mm(_mm_nt(qr, kr) * dmask, v) + _mm(qr, st) * xi
        st_ref[...] = st * chunk_decay + _mm_tn(kr * zeta, v)
        mu = jnp.mean(y, axis=-1, keepdims=True)
        yc = y - mu
        var = jnp.mean(yc * yc, axis=-1, keepdims=True)
        yn = yc * lax.rsqrt(var + LN_EPS) * gnw_ref[...]
        o_ref[sl, :] = (yn * _silu(g_ref[sl, :])).astype(o_ref.dtype)


def _retention(proj, cos2, sin2, log_g, gn_w, batch, seq, heads):
    T = batch * seq
    ts = _tile(seq, 1024, RET_CHUNK)
    ns = seq // ts
    blk = lambda off: pl.BlockSpec((ts, HEAD_DIM), lambda b, h, s: (b * ns + s, off * heads + h))
    tab = pl.BlockSpec((ts, HEAD_DIM), lambda b, h, s: (s, 0))
    kern = functools.partial(_retention_kernel, n_chunks=ts // RET_CHUNK)
    return pl.pallas_call(
        kern,
        grid=(batch, heads, ns),
        in_specs=[blk(0), blk(1), blk(2), blk(3), tab, tab,
                  pl.BlockSpec((1, 1, HEAD_DIM), lambda b, h, s: (h, 0, 0)),
                  pl.BlockSpec((1, HEAD_DIM), lambda b, h, s: (0, h))],
        out_specs=pl.BlockSpec((ts, HEAD_DIM), lambda b, h, s: (b * ns + s, h)),
        out_shape=jax.ShapeDtypeStruct((T, heads * HEAD_DIM), MXU_DTYPE),
        scratch_shapes=[pltpu.VMEM((HEAD_DIM, HEAD_DIM), F32)],
        compiler_params=_cparams(("parallel", "parallel", "arbitrary")),
        name="retention",
    )(proj, proj, proj, proj, cos2, sin2, log_g, gn_w.reshape(1, heads * HEAD_DIM))


def _swa_kernel(q_ref, k_ref, v_ref, bias_ref, o_ref, *, tq, win):
    qi = pl.program_id(2)
    start = pl.multiple_of(jnp.maximum(qi * tq - SWA_FAR, 0), tq)
    kw = k_ref[pl.ds(start, win), :]
    vw = v_ref[pl.ds(start, win), :]
    s = _mm_nt(q_ref[...], kw) * (HEAD_DIM ** -0.5) + bias_ref[0]
    m = jnp.max(s, axis=-1, keepdims=True)
    p = jnp.exp(s - m)
    l = jnp.sum(p, axis=-1, keepdims=True)
    o_ref[...] = (_mm(p, vw) / l).astype(o_ref.dtype)


def _swa_bias(tq, win):
    nvar = SWA_FAR // tq + 1
    v = lax.broadcasted_iota(jnp.int32, (nvar, tq, win), 0)
    i = lax.broadcasted_iota(jnp.int32, (nvar, tq, win), 1)
    c = lax.broadcasted_iota(jnp.int32, (nvar, tq, win), 2)
    dist = SWA_FAR - v * tq + i - c
    mult = jnp.zeros((nvar, tq, win), F32)
    for window, dilation in SWA_PATTERNS:
        mult += ((dist >= 0) & (dist <= window) & (dist % dilation == 0)).astype(F32)
    return jnp.where(mult > 0, jnp.log(jnp.maximum(mult, 1.0)), MASKED)


def _swa(proj, batch, seq, heads):
    T = batch * seq
    tq = _tile(seq, 256, 128)
    assert SWA_FAR % tq == 0 and seq >= SWA_FAR + tq
    win = SWA_FAR + tq
    nq = seq // tq
    nfar = SWA_FAR // tq
    kern = functools.partial(_swa_kernel, tq=tq, win=win)
    return pl.pallas_call(
        kern,
        grid=(batch, heads, nq),
        in_specs=[
            pl.BlockSpec((tq, HEAD_DIM), lambda b, h, i: (b * nq + i, h)),
            pl.BlockSpec((seq, HEAD_DIM), lambda b, h, i: (b, heads + h)),
            pl.BlockSpec((seq, HEAD_DIM), lambda b, h, i: (b, 2 * heads + h)),
            pl.BlockSpec((1, tq, win), lambda b, h, i: (jnp.maximum(nfar - i, 0), 0, 0)),
        ],
        out_specs=pl.BlockSpec((tq, HEAD_DIM), lambda b, h, i: (b * nq + i, h)),
        out_shape=jax.ShapeDtypeStruct((T, heads * HEAD_DIM), MXU_DTYPE),
        compiler_params=_cparams(("parallel", "parallel", "arbitrary")),
        name="dilated_swa",
    )(proj, proj, proj, _swa_bias(tq, win))


def _dn_prep_kernel(x_ref, halo_ref, w_ref, o_ref, *, n_q_blocks):
    s = pl.program_id(1)
    cb = pl.program_id(2)
    ts, width = x_ref.shape
    x = x_ref[...]
    halo = jnp.where(s == 0, 0.0, halo_ref[...])
    xp = jnp.concatenate([halo, x], axis=0)
    w = w_ref[...]
    kw = w.shape[0]
    acc = x * w[kw - 1:kw, :]
    for i in range(kw - 1):
        off = 8 - (kw - 1) + i
        acc = acc + xp[off:off + ts, :] * w[i:i + 1, :]
    y = _silu(acc)
    is_q = cb < n_q_blocks
    is_qk = cb < 2 * n_q_blocks
    for hh in range(width // HEAD_DIM):
        cols = slice(hh * HEAD_DIM, (hh + 1) * HEAD_DIM)
        yh = y[:, cols]
        inv = lax.rsqrt(jnp.sum(yh * yh, axis=-1, keepdims=True) + NORM_EPS)
        inv = jnp.where(is_q, inv * (HEAD_DIM ** -0.5), inv)
        o_ref[:, cols] = yh * jnp.where(is_qk, inv, 1.0)


def _dn_prep(qkv, conv_w, batch, seq, width):
    T = batch * seq
    ts = _tile(seq, 512, 8)
    cw = _tile(width, 512, HEAD_DIM)
    ns = seq // ts
    kern = functools.partial(_dn_prep_kernel, n_q_blocks=width // cw)
    return pl.pallas_call(
        kern,
        grid=(batch, ns, 3 * width // cw),
        in_specs=[
            pl.BlockSpec((ts, cw), lambda b, s, c: (b * ns + s, c)),
            pl.BlockSpec((8, cw), lambda b, s, c: (jnp.maximum((b * ns + s) * (ts // 8) - 1, 0), c)),
            pl.BlockSpec((conv_w.shape[0], cw), lambda b, s, c: (0, c)),
        ],
        out_specs=pl.BlockSpec((ts, cw), lambda b, s, c: (b * ns + s, c)),
        out_shape=jax.ShapeDtypeStruct((T, 3 * width), F32),
        compiler_params=_cparams(("parallel", "parallel", "parallel")),
        name="dn_conv_norm",
    )(qkv, qkv, conv_w)


def _softplus(x):
    return jnp.maximum(x, 0.0) + jnp.log1p(jnp.exp(-jnp.abs(x)))


def _dn_gates_kernel(x_ref, w_ref, wt_ref, pc_ref, pr_ref, col_ref, row_ref, *, heads):
    c = DN_CHUNK
    xb = x_ref[...].astype(MXU_DTYPE)
    col = jnp.dot(xb, w_ref[...], preferred_element_type=F32)
    row = lax.dot_general(wt_ref[...], xb, (((1,), (1,)), ((), ())), preferred_element_type=F32)
    ii = lax.broadcasted_iota(jnp.int32, (c, c), 0)
    jj = lax.broadcasted_iota(jnp.int32, (c, c), 1)
    incl = (ii >= jj).astype(F32)
    g_col = -jnp.exp(pc_ref[0:1, :]) * _softplus(col + pc_ref[1:2, :])
    gc_col = jnp.dot(incl, g_col, preferred_element_type=F32, precision=lax.Precision.HIGHEST)
    lane = lax.broadcasted_iota(jnp.int32, col.shape, 1)
    col_ref[...] = jnp.where(lane < heads, gc_col, jax.nn.sigmoid(col))
    g_row = -jnp.exp(pr_ref[:, 0:1]) * _softplus(row + pr_ref[:, 1:2])
    gc_row = lax.dot_general(g_row, incl, (((1,), (1,)), ((), ())), preferred_element_type=F32,
                             precision=lax.Precision.HIGHEST)
    sub = lax.broadcasted_iota(jnp.int32, row.shape, 0)
    row_ref[...] = jnp.where(sub < heads, gc_row, jax.nn.sigmoid(row))


def _dn_gates(x, w_small, a_log, dt_bias, heads):
    T, D = x.shape
    assert 2 * heads <= 128
    pad = 128 - 2 * heads
    w = jnp.pad(w_small, ((0, 0), (0, pad))).astype(MXU_DTYPE)
    zeros = jnp.zeros((128 - heads,), F32)
    params = jnp.stack([jnp.concatenate([a_log.astype(F32), zeros]), jnp.concatenate([dt_bias.astype(F32), zeros])])
    c = DN_CHUNK
    kern = functools.partial(_dn_gates_kernel, heads=heads)
    return pl.pallas_call(
        kern,
        grid=(T // c,),
        in_specs=[
            pl.BlockSpec((c, D), lambda i: (i, 0)),
            pl.BlockSpec((D, 128), lambda i: (0, 0)),
            pl.BlockSpec((128, D), lambda i: (0, 0)),
            pl.BlockSpec((2, 128), lambda i: (0, 0)),
            pl.BlockSpec((128, 2), lambda i: (0, 0)),
        ],
        out_specs=[pl.BlockSpec((c, 128), lambda i: (i, 0)), pl.BlockSpec((128, c), lambda i: (0, i))],
        out_shape=[jax.ShapeDtypeStruct((T, 128), F32), jax.ShapeDtypeStruct((128, T), F32)],
        compiler_params=_cparams(("parallel",)),
        name="dn_gates",
    )(x, w, w.T, params, params.T)


def _dn_kernel(q_ref, k_ref, v_ref, gate_ref, col_ref, row_ref, nw_ref, o_ref, st_ref, *, group, heads):
    c = DN_CHUNK
    hs = range(group)

    @pl.when(pl.program_id(2) == 0)
    def _():
        st_ref[...] = jnp.zeros_like(st_ref)

    ii = lax.broadcasted_iota(jnp.int32, (c, c), 0)
    jj = lax.broadcasted_iota(jnp.int32, (c, c), 1)
    eye = (ii == jj).astype(F32)
    n_levels = int(math.log2(c))
    level = [(ii >> 1) == (jj >> 1)] + [
        ((ii >> (ls + 1)) == (jj >> (ls + 1))) & (((ii >> ls) & 1) == 1) & (((jj >> ls) & 1) == 0)
        for ls in range(1, n_levels)]
    col = col_ref[...]
    lane = lax.broadcasted_iota(jnp.int32, col.shape, 1)
    cols = [slice(h * HEAD_DIM, (h + 1) * HEAD_DIM) for h in hs]
    head = [pl.program_id(1) * group + h for h in hs]
    q = [q_ref[:, cols[h]] for h in hs]
    k = [k_ref[:, cols[h]] for h in hs]
    v = [v_ref[:, cols[h]] for h in hs]
    gc_i = [jnp.sum(jnp.where(lane == head[h], col, 0.0), axis=-1, keepdims=True) for h in hs]
    beta = [jnp.sum(jnp.where(lane == head[h] + heads, col, 0.0), axis=-1, keepdims=True) for h in hs]
    gc_j = [row_ref[pl.ds(head[h], 1), :] for h in hs]
    gc_last = [gc_i[h][c - 1:c, :] for h in hs]
    decay = [jnp.exp(jnp.where(ii >= jj, gc_i[h] - gc_j[h], MASKED)) for h in hs]
    e_g = [jnp.exp(gc_i[h]) for h in hs]
    kb = [k[h] * beta[h] for h in hs]
    kq = [_mm_nt(jnp.concatenate([kb[h], q[h]], axis=0), k[h]) for h in hs]
    a = [jnp.where(ii > jj, kq[h][:c] * decay[h], 0.0) for h in hs]
    qk = [kq[h][c:] * decay[h] for h in hs]
    w0 = [jnp.where(level[0], a[h], 0.0) for h in hs]
    t = [eye - w0[h] for h in hs]
    z = [a[h] - _mm(a[h], w0[h]) for h in hs]
    for ls in range(1, n_levels):
        wl = [jnp.where(level[ls], z[h], 0.0) for h in hs]
        if ls < n_levels - 1:
            tz = [_mm(jnp.concatenate([t[h], z[h]], axis=0), wl[h]) for h in hs]
            t = [t[h] - tz[h][:c] for h in hs]
            z = [z[h] - tz[h][c:] for h in hs]
        else:
            t = [t[h] - _mm(t[h], wl[h]) for h in hs]
    uw = [_mm(t[h], jnp.concatenate([v[h] * beta[h], kb[h] * e_g[h]], axis=1)) for h in hs]
    st = [st_ref[h] for h in hs]
    ws_qs = [_mm(jnp.concatenate([uw[h][:, HEAD_DIM:], q[h] * e_g[h]], axis=0), st[h]) for h in hs]
    v_new = [uw[h][:, :HEAD_DIM] - ws_qs[h][:c] for h in hs]
    o = [ws_qs[h][c:] + _mm(qk[h], v_new[h]) for h in hs]
    for h in hs:
        st_ref[h] = st[h] * jnp.exp(gc_last[h]) + _mm_tn(k[h] * jnp.exp(gc_last[h] - gc_i[h]), v_new[h])
    for h in hs:
        on = o[h] * lax.rsqrt(jnp.mean(o[h] * o[h], axis=-1, keepdims=True) + NORM_EPS) * nw_ref[...]
        o_ref[:, cols[h]] = (on * _silu(gate_ref[:, cols[h]])).astype(o_ref.dtype)


def _delta_rule(qkv, gate, g_col, g_row, norm_w, batch, seq, heads):
    T = batch * seq
    c = DN_CHUNK
    group = _tile(heads, 4, 1)
    ng = heads // group
    nc = seq // c
    gw = group * HEAD_DIM
    blk = lambda off: pl.BlockSpec((c, gw), lambda b, g, n: (b * nc + n, off * ng + g))
    kern = functools.partial(_dn_kernel, group=group, heads=heads)
    return pl.pallas_call(
        kern,
        grid=(batch, ng, nc),
        in_specs=[blk(0), blk(1), blk(2),
                  pl.BlockSpec((c, gw), lambda b, g, n: (b * nc + n, g)),
                  pl.BlockSpec((c, 128), lambda b, g, n: (b * nc + n, 0)),
                  pl.BlockSpec((128, c), lambda b, g, n: (0, b * nc + n)),
                  pl.BlockSpec((1, HEAD_DIM), lambda b, g, n: (0, 0))],
        out_specs=pl.BlockSpec((c, gw), lambda b, g, n: (b * nc + n, g)),
        out_shape=jax.ShapeDtypeStruct((T, heads * HEAD_DIM), MXU_DTYPE),
        scratch_shapes=[pltpu.VMEM((group, HEAD_DIM, HEAD_DIM), F32)],
        compiler_params=_cparams(("parallel", "parallel", "arbitrary")),
        name="delta_rule",
    )(qkv, qkv, qkv, gate, g_col, g_row, norm_w.reshape(1, HEAD_DIM))


def _rope_tables(seq):
    pos = jnp.arange(seq, dtype=F32)
    inv_freq = 1.0 / (ROPE_BASE ** jnp.linspace(0.0, 1.0, HEAD_DIM // 2, dtype=F32))
    ang = pos[:, None] * inv_freq[None, :]
    cos, sin = jnp.cos(ang), jnp.sin(ang)
    return jnp.repeat(cos, 2, axis=1), jnp.stack([-sin, sin], axis=-1).reshape(seq, HEAD_DIM)


def _mixer_retention_swa(x, w_in, gn_w, w_out, lnw, lnb, alpha, batch, seq):
    D = x.shape[1]
    width = D // 2
    heads = width // HEAD_DIM
    proj_ret = _proj(x, w_in, 0, 4 * width, F32)
    proj_att = _proj(x, w_in, 4 * width, 3 * width, MXU_DTYPE)
    cos2, sin2 = _rope_tables(seq)
    log_g = jnp.log1p(-jnp.exp2(-5.0 - jnp.arange(heads, dtype=F32)))
    log_g = jnp.broadcast_to(log_g[:, None, None], (heads, 1, HEAD_DIM))
    y_ret = _retention(proj_ret, cos2, sin2, log_g, gn_w, batch, seq, heads)
    y_att = _swa(proj_att, batch, seq, heads)
    return _proj_ln(jnp.concatenate([y_ret, y_att], axis=1), w_out, x, lnw, lnb, alpha)


def _mixer_gated_deltanet(x, w_in, conv_w, a_log, dt_bias, norm_w, w_out, lnw, lnb, alpha, batch, seq):
    D = x.shape[1]
    heads = D // HEAD_DIM
    qkv = _proj(x, w_in, 0, 3 * D, F32)
    gate = _proj(x, w_in, 3 * D, D, F32)
    g_col, g_row = _dn_gates(x, w_in[:, 4 * D:], a_log, dt_bias, heads)
    qkv = _dn_prep(qkv, conv_w, batch, seq, D)
    o = _delta_rule(qkv, gate, g_col, g_row, norm_w, batch, seq, heads)
    return _proj_ln(o, w_out, x, lnw, lnb, alpha)


def kernel(x, ffn_w_gate, ffn_w_up, ffn_w_down, ln_w, ln_b, ab_w_in, ab_gn_w, ab_w_out,
           dn_w_in, dn_conv_w, dn_a_log, dn_dt_bias, dn_norm_w, dn_w_out):
    batch, seq, D = x.shape
    depth = ffn_w_gate.shape[0]
    alpha = (2.0 * depth) ** 0.25
    h = x.reshape(batch * seq, D)
    wg, wu, wd = (w.astype(MXU_DTYPE) for w in (ffn_w_gate, ffn_w_up, ffn_w_down))
    for l in range(depth):
        h = _ffn_ln(h, wg, wu, wd, l, 0, ln_w[l, 0], ln_b[l, 0], alpha)
        i = l // 2
        if l % 2 == 0:
            h = _mixer_retention_swa(h, ab_w_in[i].astype(MXU_DTYPE), ab_gn_w[i], ab_w_out[i].astype(MXU_DTYPE),
                                     ln_w[l, 1], ln_b[l, 1], alpha, batch, seq)
        else:
            h = _mixer_gated_deltanet(h, dn_w_in[i].astype(MXU_DTYPE), dn_conv_w[i], dn_a_log[i], dn_dt_bias[i],
                                      dn_norm_w[i], dn_w_out[i].astype(MXU_DTYPE), ln_w[l, 1], ln_b[l, 1], alpha,
                                      batch, seq)
        h = _ffn_ln(h, wg, wu, wd, l, 1, ln_w[l, 2], ln_b[l, 2], alpha)
    return h.reshape(batch, seq, D)
```

```python
import functools
import math

import jax
import jax.numpy as jnp
from jax import lax
from jax.experimental import pallas as pl
from jax.experimental.pallas import tpu as pltpu

F32 = jnp.float32
MXU_DTYPE = jnp.bfloat16
HEAD_DIM = 128
LN_EPS = 1e-5
NORM_EPS = 1e-6
ROPE_BASE = 10000.0
SWA_PATTERNS = ((128, 1), (512, 4), (2048, 16))
SWA_FAR = max(w for w, _ in SWA_PATTERNS)
DN_CHUNK = 128
DN_GROUP = 8
RET_CHUNK = 128
MASKED = -1e30
V7X_VMEM_LIMIT_BYTES = 56 * 1024 * 1024


def _cparams(sem):
    return pltpu.CompilerParams(dimension_semantics=sem, vmem_limit_bytes=V7X_VMEM_LIMIT_BYTES)


def _mm(a, b):
    return jnp.dot(a.astype(MXU_DTYPE), b.astype(MXU_DTYPE), preferred_element_type=F32)


def _mm_nt(a, b):
    return lax.dot_general(a.astype(MXU_DTYPE), b.astype(MXU_DTYPE), (((1,), (1,)), ((), ())),
                           preferred_element_type=F32)


def _mm_tn(a, b):
    return lax.dot_general(a.astype(MXU_DTYPE), b.astype(MXU_DTYPE), (((0,), (0,)), ((), ())),
                           preferred_element_type=F32)


def _silu(x):
    return x * jax.nn.sigmoid(x)


def _layer_norm(y, w, b):
    mu = jnp.mean(y, axis=-1, keepdims=True)
    yc = y - mu
    var = jnp.mean(yc * yc, axis=-1, keepdims=True)
    return yc * lax.rsqrt(var + LN_EPS) * w + b


def _tile(n, pref, unit):
    if n <= pref:
        return n
    t = (pref // unit) * unit
    while t > unit and n % t:
        t -= unit
    assert n % t == 0, (n, pref, unit)
    return t


def _residual_ln_rows(x_ref, o_ref, lnw_ref, lnb_ref, alpha, branch_scale, rows):
    tm = o_ref.shape[0]

    def body(r, carry):
        sl = pl.ds(pl.multiple_of(r * rows, rows), rows)
        y = alpha * x_ref[sl, :] + branch_scale * o_ref[sl, :]
        o_ref[sl, :] = _layer_norm(y, lnw_ref[...], lnb_ref[...])
        return carry

    lax.fori_loop(0, tm // rows, body, 0)


def _ffn_kernel(x_ref, wg_ref, wu_ref, wd_ref, lnw_ref, lnb_ref, o_ref, xb_ref, *, alpha, n_split, rows):
    j = pl.program_id(1)

    @pl.when(j == 0)
    def _():
        xb_ref[...] = x_ref[...].astype(xb_ref.dtype)
        o_ref[...] = jnp.zeros_like(o_ref)

    xb = xb_ref[...]
    g = jnp.dot(xb, wg_ref[...], preferred_element_type=F32)
    u = jnp.dot(xb, wu_ref[...], preferred_element_type=F32)
    h = (_silu(g) * u).astype(xb_ref.dtype)
    dn = o_ref.shape[1] // n_split
    for n in range(n_split):
        cols = slice(n * dn, (n + 1) * dn)
        o_ref[:, cols] += jnp.dot(h, wd_ref[:, cols], preferred_element_type=F32)

    @pl.when(j == pl.num_programs(1) - 1)
    def _():
        _residual_ln_rows(x_ref, o_ref, lnw_ref, lnb_ref, alpha, 0.5, rows)


def _ffn_ln(x, wg, wu, wd, l, k, lnw, lnb, alpha):
    T, D = x.shape
    F = wg.shape[-1]
    tm = _tile(T, 512, 8)
    tf = _tile(F, 256, 128)
    kern = functools.partial(_ffn_kernel, alpha=alpha, n_split=max(D // 512, 1), rows=min(tm, 32))
    return pl.pallas_call(
        kern,
        grid=(T // tm, F // tf),
        in_specs=[
            pl.BlockSpec((tm, D), lambda i, j: (i, 0)),
            pl.BlockSpec((None, None, D, tf), lambda i, j: (l, k, 0, j)),
            pl.BlockSpec((None, None, D, tf), lambda i, j: (l, k, 0, j)),
            pl.BlockSpec((None, None, tf, D), lambda i, j: (l, k, j, 0)),
            pl.BlockSpec((1, D), lambda i, j: (0, 0)),
            pl.BlockSpec((1, D), lambda i, j: (0, 0)),
        ],
        out_specs=pl.BlockSpec((tm, D), lambda i, j: (i, 0)),
        out_shape=jax.ShapeDtypeStruct((T, D), F32),
        scratch_shapes=[pltpu.VMEM((tm, D), MXU_DTYPE)],
        compiler_params=_cparams(("parallel", "arbitrary")),
        name="ffn_ln",
    )(x, wg, wu, wd, lnw.reshape(1, D), lnb.reshape(1, D))


def _proj_kernel(x_ref, w_ref, o_ref, xb_ref):
    @pl.when(pl.program_id(1) == 0)
    def _():
        xb_ref[...] = x_ref[...].astype(xb_ref.dtype)

    o_ref[...] = jnp.dot(xb_ref[...], w_ref[...], preferred_element_type=F32).astype(o_ref.dtype)


def _proj(x, w, col0, ncols, out_dtype):
    T, K = x.shape
    tm = _tile(T, 512, 8)
    tn = _tile(math.gcd(ncols, col0) if col0 else ncols, 1024, 128)
    j0 = col0 // tn
    return pl.pallas_call(
        _proj_kernel,
        grid=(T // tm, ncols // tn),
        in_specs=[pl.BlockSpec((tm, K), lambda i, j: (i, 0)),
                  pl.BlockSpec((K, tn), lambda i, j: (0, j0 + j))],
        out_specs=pl.BlockSpec((tm, tn), lambda i, j: (i, j)),
        out_shape=jax.ShapeDtypeStruct((T, ncols), out_dtype),
        scratch_shapes=[pltpu.VMEM((tm, K), MXU_DTYPE)],
        compiler_params=_cparams(("parallel", "arbitrary")),
        name="in_proj",
    )(x, w)


def _proj_ln_kernel(a_ref, w_ref, x_ref, lnw_ref, lnb_ref, o_ref, *, alpha, n_split, rows):
    a = a_ref[...]
    dn = o_ref.shape[1] // n_split
    for n in range(n_split):
        cols = slice(n * dn, (n + 1) * dn)
        o_ref[:, cols] = jnp.dot(a, w_ref[:, cols], preferred_element_type=F32)
    _residual_ln_rows(x_ref, o_ref, lnw_ref, lnb_ref, alpha, 1.0, rows)


def _proj_ln(a, w, x, lnw, lnb, alpha):
    T, K = a.shape
    D = w.shape[1]
    tm = _tile(T, 256, 8)
    kern = functools.partial(_proj_ln_kernel, alpha=alpha, n_split=max(D // 512, 1), rows=min(tm, 32))
    return pl.pallas_call(
        kern,
        grid=(T // tm,),
        in_specs=[
            pl.BlockSpec((tm, K), lambda i: (i, 0)),
            pl.BlockSpec((K, D), lambda i: (0, 0), pipeline_mode=pl.Buffered(1)),
            pl.BlockSpec((tm, D), lambda i: (i, 0)),
            pl.BlockSpec((1, D), lambda i: (0, 0)),
            pl.BlockSpec((1, D), lambda i: (0, 0)),
        ],
        out_specs=pl.BlockSpec((tm, D), lambda i: (i, 0)),
        out_shape=jax.ShapeDtypeStruct((T, D), F32),
        compiler_params=_cparams(("parallel",)),
        name="out_proj_ln",
    )(a, w, x, lnw.reshape(1, D), lnb.reshape(1, D))


def _retention_kernel(q_ref, k_ref, v_ref, g_ref, cos_ref, sin_ref, lg_ref, gnw_ref, o_ref, st_ref, *, n_chunks):
    c = RET_CHUNK

    @pl.when(pl.program_id(2) == 0)
    def _():
        st_ref[...] = jnp.zeros_like(st_ref)

    lg = lg_ref[0]
    ii = lax.broadcasted_iota(jnp.int32, (c, c), 0)
    jj = lax.broadcasted_iota(jnp.int32, (c, c), 1)
    rel = (ii - jj).astype(F32)
    dmask = jnp.where(rel >= 0, jnp.exp(lg * jnp.maximum(rel, 0.0)), 0.0)
    row = ii.astype(F32)
    zeta = jnp.exp(lg * (c - 1 - row))
    xi = jnp.exp(lg * (row + 1.0))
    chunk_decay = jnp.exp(lg * c)
    even_lane = (jj & 1) == 0

    def rotate(t, cs, sn):
        partner = jnp.where(even_lane, pltpu.roll(t, HEAD_DIM - 1, 1), pltpu.roll(t, 1, 1))
        return t * cs + partner * sn

    for n in range(n_chunks):
        sl = slice(n * c, (n + 1) * c)
        cs, sn = cos_ref[sl, :], sin_ref[sl, :]
        q, k, v = q_ref[sl, :], k_ref[sl, :], v_ref[sl, :]
        qr = rotate(q, cs, sn)
        kr = rotate(k, cs, sn) * (HEAD_DIM ** -0.5)
        st = st_ref[...]
        y = _mm(_mm_nt(qr, kr) * dmask, v) + _mm(qr, st) * xi
        st_ref[...] = st * chunk_decay + _mm_tn(kr * zeta, v)
        mu = jnp.mean(y, axis=-1, keepdims=True)
        yc = y - mu
        var = jnp.mean(yc * yc, axis=-1, keepdims=True)
        yn = yc * lax.rsqrt(var + LN_EPS) * gnw_ref[...]
        o_ref[sl, :] = (yn * _silu(g_ref[sl, :])).astype(o_ref.dtype)


def _retention(proj, cos2, sin2, log_g, gn_w, batch, seq, heads):
    T = batch * seq
    ts = _tile(seq, 1024, RET_CHUNK)
    ns = seq // ts
    blk = lambda off: pl.BlockSpec((ts, HEAD_DIM), lambda b, h, s: (b * ns + s, off * heads + h))
    tab = pl.BlockSpec((ts, HEAD_DIM), lambda b, h, s: (s, 0))
    kern = functools.partial(_retention_kernel, n_chunks=ts // RET_CHUNK)
    return pl.pallas_call(
        kern,
        grid=(batch, heads, ns),
        in_specs=[blk(0), blk(1), blk(2), blk(3), tab, tab,
                  pl.BlockSpec((1, 1, HEAD_DIM), lambda b, h, s: (h, 0, 0)),
                  pl.BlockSpec((1, HEAD_DIM), lambda b, h, s: (0, h))],
        out_specs=pl.BlockSpec((ts, HEAD_DIM), lambda b, h, s: (b * ns + s, h)),
        out_shape=jax.ShapeDtypeStruct((T, heads * HEAD_DIM), MXU_DTYPE),
        scratch_shapes=[pltpu.VMEM((HEAD_DIM, HEAD_DIM), F32)],
        compiler_params=_cparams(("parallel", "parallel", "arbitrary")),
        name="retention",
    )(proj, proj, proj, proj, cos2, sin2, log_g, gn_w.reshape(1, heads * HEAD_DIM))


def _swa_kernel(q_ref, k_ref, v_ref, bias_ref, o_ref, *, tq, win):
    qi = pl.program_id(2)
    start = pl.multiple_of(jnp.maximum(qi * tq - SWA_FAR, 0), tq)
    kw = k_ref[pl.ds(start, win), :]
    vw = v_ref[pl.ds(start, win), :]
    s = _mm_nt(q_ref[...], kw) * (HEAD_DIM ** -0.5) + bias_ref[0]
    m = jnp.max(s, axis=-1, keepdims=True)
    p = jnp.exp(s - m)
    l = jnp.sum(p, axis=-1, keepdims=True)
    o_ref[...] = (_mm(p, vw) / l).astype(o_ref.dtype)


def _swa_bias(tq, win):
    nvar = SWA_FAR // tq + 1
    v = lax.broadcasted_iota(jnp.int32, (nvar, tq, win), 0)
    i = lax.broadcasted_iota(jnp.int32, (nvar, tq, win), 1)
    c = lax.broadcasted_iota(jnp.int32, (nvar, tq, win), 2)
    dist = SWA_FAR - v * tq + i - c
    mult = jnp.zeros((nvar, tq, win), F32)
    for window, dilation in SWA_PATTERNS:
        mult += ((dist >= 0) & (dist <= window) & (dist % dilation == 0)).astype(F32)
    return jnp.where(mult > 0, jnp.log(jnp.maximum(mult, 1.0)), MASKED)


def _swa(proj, batch, seq, heads):
    T = batch * seq
    tq = _tile(seq, 256, 128)
    assert SWA_FAR % tq == 0 and seq >= SWA_FAR + tq
    win = SWA_FAR + tq
    nq = seq // tq
    nfar = SWA_FAR // tq
    kern = functools.partial(_swa_kernel, tq=tq, win=win)
    return pl.pallas_call(
        kern,
        grid=(batch, heads, nq),
        in_specs=[
            pl.BlockSpec((tq, HEAD_DIM), lambda b, h, i: (b * nq + i, h)),
            pl.BlockSpec((seq, HEAD_DIM), lambda b, h, i: (b, heads + h)),
            pl.BlockSpec((seq, HEAD_DIM), lambda b, h, i: (b, 2 * heads + h)),
            pl.BlockSpec((1, tq, win), lambda b, h, i: (jnp.maximum(nfar - i, 0), 0, 0)),
        ],
        out_specs=pl.BlockSpec((tq, HEAD_DIM), lambda b, h, i: (b * nq + i, h)),
        out_shape=jax.ShapeDtypeStruct((T, heads * HEAD_DIM), MXU_DTYPE),
        compiler_params=_cparams(("parallel", "parallel", "arbitrary")),
        name="dilated_swa",
    )(proj, proj, proj, _swa_bias(tq, win))


def _softplus(x):
    return jnp.maximum(x, 0.0) + jnp.log1p(jnp.exp(-jnp.abs(x)))


def _dn_gates_kernel(x_ref, w_ref, wt_ref, pc_ref, pr_ref, col_ref, row_ref, *, heads):
    c = DN_CHUNK
    xb = x_ref[...].astype(MXU_DTYPE)
    col = jnp.dot(xb, w_ref[...], preferred_element_type=F32)
    row = lax.dot_general(wt_ref[...], xb, (((1,), (1,)), ((), ())), preferred_element_type=F32)
    ii = lax.broadcasted_iota(jnp.int32, (c, c), 0)
    jj = lax.broadcasted_iota(jnp.int32, (c, c), 1)
    incl = (ii >= jj).astype(F32)
    g_col = -jnp.exp(pc_ref[0:1, :]) * _softplus(col + pc_ref[1:2, :])
    gc_col = jnp.dot(incl, g_col, preferred_element_type=F32, precision=lax.Precision.HIGHEST)
    lane = lax.broadcasted_iota(jnp.int32, col.shape, 1)
    col_ref[...] = jnp.where(lane < heads, gc_col, jax.nn.sigmoid(col))
    g_row = -jnp.exp(pr_ref[:, 0:1]) * _softplus(row + pr_ref[:, 1:2])
    gc_row = lax.dot_general(g_row, incl, (((1,), (1,)), ((), ())), preferred_element_type=F32,
                             precision=lax.Precision.HIGHEST)
    sub = lax.broadcasted_iota(jnp.int32, row.shape, 0)
    row_ref[...] = jnp.where(sub < heads, gc_row, jax.nn.sigmoid(row))


def _dn_gates(x, w_small, a_log, dt_bias, heads):
    T, D = x.shape
    assert 2 * heads <= 128
    pad = 128 - 2 * heads
    w = jnp.pad(w_small, ((0, 0), (0, pad))).astype(MXU_DTYPE)
    zeros = jnp.zeros((128 - heads,), F32)
    params = jnp.stack([jnp.concatenate([a_log.astype(F32), zeros]), jnp.concatenate([dt_bias.astype(F32), zeros])])
    c = DN_CHUNK
    kern = functools.partial(_dn_gates_kernel, heads=heads)
    return pl.pallas_call(
        kern,
        grid=(T // c,),
        in_specs=[
            pl.BlockSpec((c, D), lambda i: (i, 0)),
            pl.BlockSpec((D, 128), lambda i: (0, 0)),
            pl.BlockSpec((128, D), lambda i: (0, 0)),
            pl.BlockSpec((2, 128), lambda i: (0, 0)),
            pl.BlockSpec((128, 2), lambda i: (0, 0)),
        ],
        out_specs=[pl.BlockSpec((c, 128), lambda i: (i, 0)), pl.BlockSpec((128, c), lambda i: (0, i))],
        out_shape=[jax.ShapeDtypeStruct((T, 128), F32), jax.ShapeDtypeStruct((128, T), F32)],
        compiler_params=_cparams(("parallel",)),
        name="dn_gates",
    )(x, w, w.T, params, params.T)


def _conv_silu(x, prev, w):
    c = x.shape[0]
    taps = w.shape[0]
    xp = jnp.concatenate([prev, x], axis=0)
    acc = x * w[taps - 1:taps, :]
    for i in range(taps - 1):
        off = 8 - (taps - 1) + i
        acc = acc + xp[off:off + c, :] * w[i:i + 1, :]
    return _silu(acc)


def _l2norm(t):
    return t * lax.rsqrt(jnp.sum(t * t, axis=-1, keepdims=True) + NORM_EPS)


def _dn_kernel(q_ref, k_ref, v_ref, cwq_ref, cwk_ref, cwv_ref, gate_ref, col_ref, row_ref, nw_ref, o_ref,
               st_ref, tail_ref, *, group, heads):
    c = DN_CHUNK
    hs = range(group)

    @pl.when(pl.program_id(2) == 0)
    def _():
        st_ref[...] = jnp.zeros_like(st_ref)
        tail_ref[...] = jnp.zeros_like(tail_ref)

    ii = lax.broadcasted_iota(jnp.int32, (c, c), 0)
    jj = lax.broadcasted_iota(jnp.int32, (c, c), 1)
    eye = (ii == jj).astype(F32)
    n_levels = int(math.log2(c))
    level = [(ii >> 1) == (jj >> 1)] + [
        ((ii >> (ls + 1)) == (jj >> (ls + 1))) & (((ii >> ls) & 1) == 1) & (((jj >> ls) & 1) == 0)
        for ls in range(1, n_levels)]
    col = col_ref[...]
    lane = lax.broadcasted_iota(jnp.int32, col.shape, 1)
    cols = [slice(h * HEAD_DIM, (h + 1) * HEAD_DIM) for h in hs]
    head = [pl.program_id(1) * group + h for h in hs]
    q = [_l2norm(_conv_silu(q_ref[:, cols[h]], tail_ref[0, :, cols[h]], cwq_ref[:, cols[h]])) * (HEAD_DIM ** -0.5)
         for h in hs]
    k = [_l2norm(_conv_silu(k_ref[:, cols[h]], tail_ref[1, :, cols[h]], cwk_ref[:, cols[h]])) for h in hs]
    v = [_conv_silu(v_ref[:, cols[h]], tail_ref[2, :, cols[h]], cwv_ref[:, cols[h]]) for h in hs]
    for i, ref in enumerate((q_ref, k_ref, v_ref)):
        tail_ref[i] = ref[c - 8:c, :]
    gc_i =[jnp.sum(jnp.where(lane == head[h], col, 0.0), axis=-1, keepdims=True) for h in hs]
    beta = [jnp.sum(jnp.where(lane == head[h] + heads, col, 0.0), axis=-1, keepdims=True) for h in hs]
    gc_j = [row_ref[pl.ds(head[h], 1), :] for h in hs]
    gc_last = [gc_i[h][c - 1:c, :] for h in hs]
    decay = [jnp.exp(jnp.where(ii >= jj, gc_i[h] - gc_j[h], MASKED)) for h in hs]
    e_g = [jnp.exp(gc_i[h]) for h in hs]
    kb = [k[h] * beta[h] for h in hs]
    kq = [_mm_nt(jnp.concatenate([kb[h], q[h]], axis=0), k[h]) for h in hs]
    a = [jnp.where(ii > jj, kq[h][:c] * decay[h], 0.0) for h in hs]
    qk = [kq[h][c:] * decay[h] for h in hs]
    w0 = [jnp.where(level[0], a[h], 0.0) for h in hs]
    t = [eye - w0[h] for h in hs]
    z = [a[h] - _mm(a[h], w0[h]) for h in hs]
    for ls in range(1, n_levels):
        wl = [jnp.where(level[ls], z[h], 0.0) for h in hs]
        if ls < n_levels - 1:
            tz = [_mm(jnp.concatenate([t[h], z[h]], axis=0), wl[h]) for h in hs]
            t = [t[h] - tz[h][:c] for h in hs]
            z = [z[h] - tz[h][c:] for h in hs]
        else:
            t = [t[h] - _mm(t[h], wl[h]) for h in hs]
    uw = [_mm(t[h], jnp.concatenate([v[h] * beta[h], kb[h] * e_g[h]], axis=1)) for h in hs]
    st = [st_ref[h] for h in hs]
    ws_qs = [_mm(jnp.concatenate([uw[h][:, HEAD_DIM:], q[h] * e_g[h]], axis=0), st[h]) for h in hs]
    v_new = [uw[h][:, :HEAD_DIM] - ws_qs[h][:c] for h in hs]
    o = [ws_qs[h][c:] + _mm(qk[h], v_new[h]) for h in hs]
    for h in hs:
        st_ref[h] = st[h] * jnp.exp(gc_last[h]) + _mm_tn(k[h] * jnp.exp(gc_last[h] - gc_i[h]), v_new[h])
    for h in hs:
        on = o[h] * lax.rsqrt(jnp.mean(o[h] * o[h], axis=-1, keepdims=True) + NORM_EPS) * nw_ref[...]
        o_ref[:, cols[h]] = (on * _silu(gate_ref[:, cols[h]])).astype(o_ref.dtype)


def _delta_rule(qkv, conv_w, gate, g_col, g_row, norm_w, batch, seq, heads):
    T = batch * seq
    c = DN_CHUNK
    group = _tile(heads, DN_GROUP, 1)
    ng = heads // group
    nc = seq // c
    gw = group * HEAD_DIM
    taps = conv_w.shape[0]
    blk = lambda off: pl.BlockSpec((c, gw), lambda b, g, n: (b * nc + n, off * ng + g))
    cw = lambda off: pl.BlockSpec((taps, gw), lambda b, g, n: (0, off * ng + g))
    kern = functools.partial(_dn_kernel, group=group, heads=heads)
    return pl.pallas_call(
        kern,
        grid=(batch, ng, nc),
        in_specs=[blk(0), blk(1), blk(2), cw(0), cw(1), cw(2),
                  pl.BlockSpec((c, gw), lambda b, g, n: (b * nc + n, g)),
                  pl.BlockSpec((c, 128), lambda b, g, n: (b * nc + n, 0)),
                  pl.BlockSpec((128, c), lambda b, g, n: (0, b * nc + n)),
                  pl.BlockSpec((1, HEAD_DIM), lambda b, g, n: (0, 0))],
        out_specs=pl.BlockSpec((c, gw), lambda b, g, n: (b * nc + n, g)),
        out_shape=jax.ShapeDtypeStruct((T, heads * HEAD_DIM), MXU_DTYPE),
        scratch_shapes=[pltpu.VMEM((group, HEAD_DIM, HEAD_DIM), F32),
                        pltpu.VMEM((3, 8, gw), F32)],
        compiler_params=_cparams(("parallel", "parallel", "arbitrary")),
        name="delta_rule",
    )(qkv, qkv, qkv, conv_w, conv_w, conv_w, gate, g_col, g_row, norm_w.reshape(1, HEAD_DIM))


def _rope_tables(seq):
    pos = jnp.arange(seq, dtype=F32)
    inv_freq = 1.0 / (ROPE_BASE ** jnp.linspace(0.0, 1.0, HEAD_DIM // 2, dtype=F32))
    ang = pos[:, None] * inv_freq[None, :]
    cos, sin = jnp.cos(ang), jnp.sin(ang)
    return jnp.repeat(cos, 2, axis=1), jnp.stack([-sin, sin], axis=-1).reshape(seq, HEAD_DIM)


def _mixer_retention_swa(x, w_in, gn_w, w_out, lnw, lnb, alpha, batch, seq):
    D = x.shape[1]
    width = D // 2
    heads = width // HEAD_DIM
    proj_ret = _proj(x, w_in, 0, 4 * width, F32)
    proj_att = _proj(x, w_in, 4 * width, 3 * width, MXU_DTYPE)
    cos2, sin2 = _rope_tables(seq)
    log_g = jnp.log1p(-jnp.exp2(-5.0 - jnp.arange(heads, dtype=F32)))
    log_g = jnp.broadcast_to(log_g[:, None, None], (heads, 1, HEAD_DIM))
    y_ret = _retention(proj_ret, cos2, sin2, log_g, gn_w, batch, seq, heads)
    y_att = _swa(proj_att, batch, seq, heads)
    return _proj_ln(jnp.concatenate([y_ret, y_att], axis=1), w_out, x, lnw, lnb, alpha)


def _mixer_gated_deltanet(x, w_in, conv_w, a_log, dt_bias, norm_w, w_out, lnw, lnb, alpha, batch, seq):
    D = x.shape[1]
    heads = D // HEAD_DIM
    qkv = _proj(x, w_in, 0, 3 * D, F32)
    gate = _proj(x, w_in, 3 * D, D, F32)
    g_col, g_row = _dn_gates(x, w_in[:, 4 * D:], a_log, dt_bias, heads)
    o = _delta_rule(qkv, conv_w, gate, g_col, g_row, norm_w, batch, seq, heads)
    return _proj_ln(o, w_out, x, lnw, lnb, alpha)


def kernel(x, ffn_w_gate, ffn_w_up, ffn_w_down, ln_w, ln_b, ab_w_in, ab_gn_w, ab_w_out,
           dn_w_in, dn_conv_w, dn_a_log, dn_dt_bias, dn_norm_w, dn_w_out):
    batch, seq, D = x.shape
    depth = ffn_w_gate.shape[0]
    alpha = (2.0 * depth) ** 0.25
    h = x.reshape(batch * seq, D)
    wg, wu, wd = (w.astype(MXU_DTYPE) for w in (ffn_w_gate, ffn_w_up, ffn_w_down))
    for l in range(depth):
        h = _ffn_ln(h, wg, wu, wd, l, 0, ln_w[l, 0], ln_b[l, 0], alpha)
        i = l // 2
        if l % 2 == 0:
            h = _mixer_retention_swa(h, ab_w_in[i].astype(MXU_DTYPE), ab_gn_w[i], ab_w_out[i].astype(MXU_DTYPE),
                                     ln_w[l, 1], ln_b[l, 1], alpha, batch, seq)
        else:
            h = _mixer_gated_deltanet(h, dn_w_in[i].astype(MXU_DTYPE), dn_conv_w[i], dn_a_log[i], dn_dt_bias[i],
                                      dn_norm_w[i], dn_w_out[i].astype(MXU_DTYPE), ln_w[l, 1], ln_b[l, 1], alpha,
                                      batch, seq)
        h = _ffn_ln(h, wg, wu, wd, l, 1, ln_w[l, 2], ln_b[l, 2], alpha)
    return h.reshape(batch, seq, D)
```

```python
import functools
import math

import jax
import jax.numpy as jnp
from jax import lax
from jax.experimental import pallas as pl
from jax.experimental.pallas import tpu as pltpu

F32 = jnp.float32
MXU_DTYPE = jnp.bfloat16
HEAD_DIM = 128
LN_EPS = 1e-5
NORM_EPS = 1e-6
ROPE_BASE = 10000.0
SWA_PATTERNS = ((128, 1), (512, 4), (2048, 16))
SWA_FAR = max(w for w, _ in SWA_PATTERNS)
SWA_ROWS = 128
DN_CHUNK = 128
DN_GROUP = 8
RET_CHUNK = 128
MASKED = -1e30
V7X_VMEM_LIMIT_BYTES = 56 * 1024 * 1024


def _cparams(sem):
    return pltpu.CompilerParams(dimension_semantics=sem, vmem_limit_bytes=V7X_VMEM_LIMIT_BYTES)


def _mm(a, b):
    return jnp.dot(a.astype(MXU_DTYPE), b.astype(MXU_DTYPE), preferred_element_type=F32)


def _mm_nt(a, b):
    return lax.dot_general(a.astype(MXU_DTYPE), b.astype(MXU_DTYPE), (((1,), (1,)), ((), ())),
                           preferred_element_type=F32)


def _mm_tn(a, b):
    return lax.dot_general(a.astype(MXU_DTYPE), b.astype(MXU_DTYPE), (((0,), (0,)), ((), ())),
                           preferred_element_type=F32)


def _silu(x):
    return x * jax.nn.sigmoid(x)


def _layer_norm(y, w, b):
    mu = jnp.mean(y, axis=-1, keepdims=True)
    yc = y - mu
    var = jnp.mean(yc * yc, axis=-1, keepdims=True)
    return yc * lax.rsqrt(var + LN_EPS) * w + b


def _tile(n, pref, unit):
    if n <= pref:
        return n
    t = (pref // unit) * unit
    while t > unit and n % t:
        t -= unit
    assert n % t == 0, (n, pref, unit)
    return t


def _residual_ln_rows(x_ref, o_ref, lnw_ref, lnb_ref, alpha, branch_scale, rows):
    tm = o_ref.shape[0]

    def body(r, carry):
        sl = pl.ds(pl.multiple_of(r * rows, rows), rows)
        y = alpha * x_ref[sl, :] + branch_scale * o_ref[sl, :]
        o_ref[sl, :] = _layer_norm(y, lnw_ref[...], lnb_ref[...])
        return carry

    lax.fori_loop(0, tm // rows, body, 0, unroll=2)


def _ffn_kernel(x_ref, wg_ref, wu_ref, wd_ref, lnw_ref, lnb_ref, o_ref, xb_ref, *, alpha, n_split, rows):
    j = pl.program_id(1)

    @pl.when(j == 0)
    def _():
        xb_ref[...] = x_ref[...].astype(xb_ref.dtype)
        o_ref[...] = jnp.zeros_like(o_ref)

    xb = xb_ref[...]
    g = jnp.dot(xb, wg_ref[...], preferred_element_type=F32)
    u = jnp.dot(xb, wu_ref[...], preferred_element_type=F32)
    h = (_silu(g) * u).astype(xb_ref.dtype)
    dn = o_ref.shape[1] // n_split
    for n in range(n_split):
        cols = slice(n * dn, (n + 1) * dn)
        o_ref[:, cols] += jnp.dot(h, wd_ref[:, cols], preferred_element_type=F32)

    @pl.when(j == pl.num_programs(1) - 1)
    def _():
        _residual_ln_rows(x_ref, o_ref, lnw_ref, lnb_ref, alpha, 0.5, rows)


def _ffn_ln(x, wg, wu, wd, l, k, lnw, lnb, alpha):
    T, D = x.shape
    F = wg.shape[-1]
    tm = _tile(T, 512, 8)
    tf = _tile(F, 256, 128)
    kern = functools.partial(_ffn_kernel, alpha=alpha, n_split=max(D // 512, 1), rows=min(tm, 32))
    return pl.pallas_call(
        kern,
        grid=(T // tm, F // tf),
        in_specs=[
            pl.BlockSpec((tm, D), lambda i, j: (i, 0)),
            pl.BlockSpec((None, None, D, tf), lambda i, j: (l, k, 0, j)),
            pl.BlockSpec((None, None, D, tf), lambda i, j: (l, k, 0, j)),
            pl.BlockSpec((None, None, tf, D), lambda i, j: (l, k, j, 0)),
            pl.BlockSpec((1, D), lambda i, j: (0, 0)),
            pl.BlockSpec((1, D), lambda i, j: (0, 0)),
        ],
        out_specs=pl.BlockSpec((tm, D), lambda i, j: (i, 0)),
        out_shape=jax.ShapeDtypeStruct((T, D), F32),
        scratch_shapes=[pltpu.VMEM((tm, D), MXU_DTYPE)],
        compiler_params=_cparams(("parallel", "arbitrary")),
        name="ffn_ln",
    )(x, wg, wu, wd, lnw.reshape(1, D), lnb.reshape(1, D))


def _proj_kernel(x_ref, w_ref, o_ref):
    o_ref[...] = jnp.dot(x_ref[...].astype(w_ref.dtype), w_ref[...], preferred_element_type=F32).astype(o_ref.dtype)


def _proj(x, w, col0, ncols, out_dtype):
    T, K = x.shape
    tm = _tile(T, 512, 8)
    tn = _tile(math.gcd(ncols, col0) if col0 else ncols, 2048, 128)
    j0 = col0 // tn
    return pl.pallas_call(
        _proj_kernel,
        grid=(ncols // tn, T // tm),
        in_specs=[pl.BlockSpec((tm, K), lambda j, i: (i, 0)),
                  pl.BlockSpec((K, tn), lambda j, i: (0, j0 + j), pipeline_mode=pl.Buffered(1))],
        out_specs=pl.BlockSpec((tm, tn), lambda j, i: (i, j)),
        out_shape=jax.ShapeDtypeStruct((T, ncols), out_dtype),
        compiler_params=_cparams(("parallel", "arbitrary")),
        name="in_proj",
    )(x, w)


def _proj_ln_kernel(*refs, alpha, n_parts, n_split, rows):
    a_refs = refs[:n_parts]
    w_ref, x_ref, lnw_ref, lnb_ref, o_ref = refs[n_parts:]
    dn = o_ref.shape[1] // n_split
    for n in range(n_split):
        cols = slice(n * dn, (n + 1) * dn)
        k0 = 0
        acc = None
        for a_ref in a_refs:
            kp = a_ref.shape[1]
            part = jnp.dot(a_ref[...], w_ref[k0:k0 + kp, cols], preferred_element_type=F32)
            acc = part if acc is None else acc + part
            k0 += kp
        o_ref[:, cols] = acc
    _residual_ln_rows(x_ref, o_ref, lnw_ref, lnb_ref, alpha, 1.0, rows)


def _proj_ln(parts, w, x, lnw, lnb, alpha):
    T = x.shape[0]
    K, D = w.shape
    assert sum(p.shape[1] for p in parts) == K
    tm = _tile(T, 256, 8)
    kern = functools.partial(_proj_ln_kernel, alpha=alpha, n_parts=len(parts), n_split=max(D // 512, 1),
                             rows=min(tm, 32))
    return pl.pallas_call(
        kern,
        grid=(T // tm,),
        in_specs=[pl.BlockSpec((tm, p.shape[1]), lambda i: (i, 0)) for p in parts] + [
            pl.BlockSpec((K, D), lambda i: (0, 0), pipeline_mode=pl.Buffered(1)),
            pl.BlockSpec((tm, D), lambda i: (i, 0)),
            pl.BlockSpec((1, D), lambda i: (0, 0)),
            pl.BlockSpec((1, D), lambda i: (0, 0)),
        ],
        out_specs=pl.BlockSpec((tm, D), lambda i: (i, 0)),
        out_shape=jax.ShapeDtypeStruct((T, D), F32),
        compiler_params=_cparams(("parallel",)),
        name="out_proj_ln",
    )(*parts, w, x, lnw.reshape(1, D), lnb.reshape(1, D))


def _retention_kernel(q_ref, k_ref, v_ref, g_ref, cos_ref, sin_ref, lg_ref, gnw_ref, o_ref, st_ref, *, n_chunks):
    c = RET_CHUNK

    @pl.when(pl.program_id(2) == 0)
    def _():
        st_ref[...] = jnp.zeros_like(st_ref)

    lg = lg_ref[0]
    ii = lax.broadcasted_iota(jnp.int32, (c, c), 0)
    jj = lax.broadcasted_iota(jnp.int32, (c, c), 1)
    rel = (ii - jj).astype(F32)
    dmask = jnp.where(rel >= 0, jnp.exp(lg * jnp.maximum(rel, 0.0)), 0.0)
    row = ii.astype(F32)
    zeta = jnp.exp(lg * (c - 1 - row))
    xi = jnp.exp(lg * (row + 1.0))
    chunk_decay = jnp.exp(lg * c)
    even_lane = (jj & 1) == 0

    def rotate(t, cs, sn):
        partner = jnp.where(even_lane, pltpu.roll(t, HEAD_DIM - 1, 1), pltpu.roll(t, 1, 1))
        return t * cs + partner * sn

    for n in range(n_chunks):
        sl = slice(n * c, (n + 1) * c)
        cs, sn = cos_ref[sl, :], sin_ref[sl, :]
        q, k, v = q_ref[sl, :], k_ref[sl, :], v_ref[sl, :]
        qr = rotate(q, cs, sn)
        kr = rotate(k, cs, sn) * (HEAD_DIM ** -0.5)
        st = st_ref[...]
        y = _mm(_mm_nt(qr, kr) * dmask, v) + _mm(qr, st) * xi
        st_ref[...] = st * chunk_decay + _mm_tn(kr * zeta, v)
        mu = jnp.mean(y, axis=-1, keepdims=True)
        yc = y - mu
        var = jnp.mean(yc * yc, axis=-1, keepdims=True)
        yn = yc * lax.rsqrt(var + LN_EPS) * gnw_ref[...]
        o_ref[sl, :] = (yn * _silu(g_ref[sl, :])).astype(o_ref.dtype)


def _retention(proj, cos2, sin2, log_g, gn_w, batch, seq, heads):
    T = batch * seq
    ts = _tile(seq, 1024, RET_CHUNK)
    ns = seq // ts
    blk = lambda off: pl.BlockSpec((ts, HEAD_DIM), lambda b, h, s: (b * ns + s, off * heads + h))
    tab = pl.BlockSpec((ts, HEAD_DIM), lambda b, h, s: (s, 0))
    kern = functools.partial(_retention_kernel, n_chunks=ts // RET_CHUNK)
    return pl.pallas_call(
        kern,
        grid=(batch, heads, ns),
        in_specs=[blk(0), blk(1), blk(2), blk(3), tab, tab,
                  pl.BlockSpec((1, 1, HEAD_DIM), lambda b, h, s: (h, 0, 0)),
                  pl.BlockSpec((1, HEAD_DIM), lambda b, h, s: (0, h))],
        out_specs=pl.BlockSpec((ts, HEAD_DIM), lambda b, h, s: (b * ns + s, h)),
        out_shape=jax.ShapeDtypeStruct((T, heads * HEAD_DIM), MXU_DTYPE),
        scratch_shapes=[pltpu.VMEM((HEAD_DIM, HEAD_DIM), F32)],
        compiler_params=_cparams(("parallel", "parallel", "arbitrary")),
        name="retention",
    )(proj, proj, proj, proj, cos2, sin2, log_g, gn_w.reshape(1, heads * HEAD_DIM))


def _swa_kernel(q_ref, k_ref, v_ref, bias_ref, o_ref, *, tq, win):
    qi = pl.program_id(2)
    start = pl.multiple_of(jnp.maximum(qi * tq - SWA_FAR, 0), tq)
    kw = k_ref[pl.ds(start, win), :]
    vw = v_ref[pl.ds(start, win), :]
    rows = [slice(r * SWA_ROWS, (r + 1) * SWA_ROWS) for r in range(tq // SWA_ROWS)]
    s = [_mm_nt(q_ref[r, :], kw) * (HEAD_DIM ** -0.5) + bias_ref[0, r, :] for r in rows]
    p = [jnp.exp(si - jnp.max(si, axis=-1, keepdims=True)) for si in s]
    l = [jnp.sum(pi, axis=-1, keepdims=True) for pi in p]
    for r, pi, li in zip(rows, p, l):
        o_ref[r, :] = (_mm(pi, vw) / li).astype(o_ref.dtype)


def _swa_bias(tq, win):
    nvar = SWA_FAR // tq + 1
    v = lax.broadcasted_iota(jnp.int32, (nvar, tq, win), 0)
    i = lax.broadcasted_iota(jnp.int32, (nvar, tq, win), 1)
    c = lax.broadcasted_iota(jnp.int32, (nvar, tq, win), 2)
    dist = SWA_FAR - v * tq + i - c
    mult = jnp.zeros((nvar, tq, win), F32)
    for window, dilation in SWA_PATTERNS:
        mult += ((dist >= 0) & (dist <= window) & (dist % dilation == 0)).astype(F32)
    return jnp.where(mult > 0, jnp.log(jnp.maximum(mult, 1.0)), MASKED)


def _swa(proj, batch, seq, heads):
    T = batch * seq
    tq = _tile(seq, 256, 128)
    assert SWA_FAR % tq == 0 and seq >= SWA_FAR + tq
    win = SWA_FAR + tq
    nq = seq // tq
    nfar = SWA_FAR // tq
    kern = functools.partial(_swa_kernel, tq=tq, win=win)
    return pl.pallas_call(
        kern,
        grid=(batch, heads, nq),
        in_specs=[
            pl.BlockSpec((tq, HEAD_DIM), lambda b, h, i: (b * nq + i, h)),
            pl.BlockSpec((seq, HEAD_DIM), lambda b, h, i: (b, heads + h)),
            pl.BlockSpec((seq, HEAD_DIM), lambda b, h, i: (b, 2 * heads + h)),
            pl.BlockSpec((1, tq, win), lambda b, h, i: (jnp.maximum(nfar - i, 0), 0, 0)),
        ],
        out_specs=pl.BlockSpec((tq, HEAD_DIM), lambda b, h, i: (b * nq + i, h)),
        out_shape=jax.ShapeDtypeStruct((T, heads * HEAD_DIM), MXU_DTYPE),
        compiler_params=_cparams(("parallel", "parallel", "arbitrary")),
        name="dilated_swa",
    )(proj, proj, proj, _swa_bias(tq, win))


def _softplus(x):
    return jnp.maximum(x, 0.0) + jnp.log1p(jnp.exp(-jnp.abs(x)))


def _dn_gates_kernel(x_ref, w_ref, wt_ref, pc_ref, pr_ref, col_ref, row_ref, *, heads):
    c = DN_CHUNK
    xb = x_ref[...].astype(MXU_DTYPE)
    col = jnp.dot(xb, w_ref[...], preferred_element_type=F32)
    row = lax.dot_general(wt_ref[...], xb, (((1,), (1,)), ((), ())), preferred_element_type=F32)
    ii = lax.broadcasted_iota(jnp.int32, (c, c), 0)
    jj = lax.broadcasted_iota(jnp.int32, (c, c), 1)
    incl = (ii >= jj).astype(F32)
    g_col = -jnp.exp(pc_ref[0:1, :]) * _softplus(col + pc_ref[1:2, :])
    gc_col = jnp.dot(incl, g_col, preferred_element_type=F32, precision=lax.Precision.HIGHEST)
    lane = lax.broadcasted_iota(jnp.int32, col.shape, 1)
    col_ref[...] = jnp.where(lane < heads, gc_col, jax.nn.sigmoid(col))
    g_row = -jnp.exp(pr_ref[:, 0:1]) * _softplus(row + pr_ref[:, 1:2])
    gc_row = lax.dot_general(g_row, incl, (((1,), (1,)), ((), ())), preferred_element_type=F32,
                             precision=lax.Precision.HIGHEST)
    sub = lax.broadcasted_iota(jnp.int32, row.shape, 0)
    row_ref[...] = jnp.where(sub < heads, gc_row, jax.nn.sigmoid(row))


def _dn_gates(x, w_small, a_log, dt_bias, heads):
    T, D = x.shape
    assert 2 * heads <= 128
    pad = 128 - 2 * heads
    w = jnp.pad(w_small, ((0, 0), (0, pad))).astype(MXU_DTYPE)
    zeros = jnp.zeros((128 - heads,), F32)
    params = jnp.stack([jnp.concatenate([a_log.astype(F32), zeros]), jnp.concatenate([dt_bias.astype(F32), zeros])])
    c = DN_CHUNK
    kern = functools.partial(_dn_gates_kernel, heads=heads)
    return pl.pallas_call(
        kern,
        grid=(T // c,),
        in_specs=[
            pl.BlockSpec((c, D), lambda i: (i, 0)),
            pl.BlockSpec((D, 128), lambda i: (0, 0)),
            pl.BlockSpec((128, D), lambda i: (0, 0)),
            pl.BlockSpec((2, 128), lambda i: (0, 0)),
            pl.BlockSpec((128, 2), lambda i: (0, 0)),
        ],
        out_specs=[pl.BlockSpec((c, 128), lambda i: (i, 0)), pl.BlockSpec((128, c), lambda i: (0, i))],
        out_shape=[jax.ShapeDtypeStruct((T, 128), F32), jax.ShapeDtypeStruct((128, T), F32)],
        compiler_params=_cparams(("parallel",)),
        name="dn_gates",
    )(x, w, w.T, params, params.T)


def _conv_silu(x, prev, w):
    c = x.shape[0]
    taps = w.shape[0]
    xp = jnp.concatenate([prev, x], axis=0)
    acc = x * w[taps - 1:taps, :]
    for i in range(taps - 1):
        off = 8 - (taps - 1) + i
        acc = acc + xp[off:off + c, :] * w[i:i + 1, :]
    return _silu(acc)


def _l2norm(t):
    return t * lax.rsqrt(jnp.sum(t * t, axis=-1, keepdims=True) + NORM_EPS)


def _dn_kernel(q_ref, k_ref, v_ref, cwq_ref, cwk_ref, cwv_ref, gate_ref, col_ref, row_ref, nw_ref, o_ref,
               st_ref, tail_ref, *, group, heads):
    c = DN_CHUNK
    hs = range(group)

    @pl.when(pl.program_id(2) == 0)
    def _():
        st_ref[...] = jnp.zeros_like(st_ref)
        tail_ref[...] = jnp.zeros_like(tail_ref)

    ii = lax.broadcasted_iota(jnp.int32, (c, c), 0)
    jj = lax.broadcasted_iota(jnp.int32, (c, c), 1)
    eye = (ii == jj).astype(F32)
    n_levels = int(math.log2(c))
    level = [(ii >> 1) == (jj >> 1)] + [
        ((ii >> (ls + 1)) == (jj >> (ls + 1))) & (((ii >> ls) & 1) == 1) & (((jj >> ls) & 1) == 0)
        for ls in range(1, n_levels)]
    col = col_ref[...]
    lane = lax.broadcasted_iota(jnp.int32, col.shape, 1)
    cols = [slice(h * HEAD_DIM, (h + 1) * HEAD_DIM) for h in hs]
    head = [pl.program_id(1) * group + h for h in hs]
    q = [_l2norm(_conv_silu(q_ref[:, cols[h]], tail_ref[0, :, cols[h]], cwq_ref[:, cols[h]])) * (HEAD_DIM ** -0.5)
         for h in hs]
    k = [_l2norm(_conv_silu(k_ref[:, cols[h]], tail_ref[1, :, cols[h]], cwk_ref[:, cols[h]])) for h in hs]
    v = [_conv_silu(v_ref[:, cols[h]], tail_ref[2, :, cols[h]], cwv_ref[:, cols[h]]) for h in hs]
    for i, ref in enumerate((q_ref, k_ref, v_ref)):
        tail_ref[i] = ref[c - 8:c, :]
    gc_i =[jnp.sum(jnp.where(lane == head[h], col, 0.0), axis=-1, keepdims=True) for h in hs]
    beta = [jnp.sum(jnp.where(lane == head[h] + heads, col, 0.0), axis=-1, keepdims=True) for h in hs]
    gc_j = [row_ref[pl.ds(head[h], 1), :] for h in hs]
    gc_last = [gc_i[h][c - 1:c, :] for h in hs]
    decay = [jnp.exp(jnp.where(ii >= jj, gc_i[h] - gc_j[h], MASKED)) for h in hs]
    e_g = [jnp.exp(gc_i[h]) for h in hs]
    kb = [k[h] * beta[h] for h in hs]
    kq = [_mm_nt(jnp.concatenate([kb[h], q[h]], axis=0), k[h]) for h in hs]
    a = [jnp.where(ii > jj, kq[h][:c] * decay[h], 0.0) for h in hs]
    qk = [kq[h][c:] * decay[h] for h in hs]
    w0 = [jnp.where(level[0], a[h], 0.0) for h in hs]
    t = [eye - w0[h] for h in hs]
    z = [a[h] - _mm(a[h], w0[h]) for h in hs]
    for ls in range(1, n_levels):
        wl = [jnp.where(level[ls], z[h], 0.0) for h in hs]
        if ls < n_levels - 1:
            tz = [_mm(jnp.concatenate([t[h], z[h]], axis=0), wl[h]) for h in hs]
            t = [t[h] - tz[h][:c] for h in hs]
            z = [z[h] - tz[h][c:] for h in hs]
        else:
            t = [t[h] - _mm(t[h], wl[h]) for h in hs]
    uw = [_mm(t[h], jnp.concatenate([v[h] * beta[h], kb[h] * e_g[h]], axis=1)) for h in hs]
    st = [st_ref[h] for h in hs]
    ws_qs = [_mm(jnp.concatenate([uw[h][:, HEAD_DIM:], q[h] * e_g[h]], axis=0), st[h]) for h in hs]
    v_new = [uw[h][:, :HEAD_DIM] - ws_qs[h][:c] for h in hs]
    o = [ws_qs[h][c:] + _mm(qk[h], v_new[h]) for h in hs]
    for h in hs:
        st_ref[h] = st[h] * jnp.exp(gc_last[h]) + _mm_tn(k[h] * jnp.exp(gc_last[h] - gc_i[h]), v_new[h])
    for h in hs:
        on = o[h] * lax.rsqrt(jnp.mean(o[h] * o[h], axis=-1, keepdims=True) + NORM_EPS) * nw_ref[...]
        o_ref[:, cols[h]] = (on * _silu(gate_ref[:, cols[h]])).astype(o_ref.dtype)


def _delta_rule(qkv, conv_w, gate, g_col, g_row, norm_w, batch, seq, heads):
    T = batch * seq
    c = DN_CHUNK
    group = _tile(heads, DN_GROUP, 1)
    ng = heads // group
    nc = seq // c
    gw = group * HEAD_DIM
    taps = conv_w.shape[0]
    blk = lambda off: pl.BlockSpec((c, gw), lambda b, g, n: (b * nc + n, off * ng + g))
    cw = lambda off: pl.BlockSpec((taps, gw), lambda b, g, n: (0, off * ng + g))
    kern = functools.partial(_dn_kernel, group=group, heads=heads)
    return pl.pallas_call(
        kern,
        grid=(batch, ng, nc),
        in_specs=[blk(0), blk(1), blk(2), cw(0), cw(1), cw(2),
                  pl.BlockSpec((c, gw), lambda b, g, n: (b * nc + n, g)),
                  pl.BlockSpec((c, 128), lambda b, g, n: (b * nc + n, 0)),
                  pl.BlockSpec((128, c), lambda b, g, n: (0, b * nc + n)),
                  pl.BlockSpec((1, HEAD_DIM), lambda b, g, n: (0, 0))],
        out_specs=pl.BlockSpec((c, gw), lambda b, g, n: (b * nc + n, g)),
        out_shape=jax.ShapeDtypeStruct((T, heads * HEAD_DIM), MXU_DTYPE),
        scratch_shapes=[pltpu.VMEM((group, HEAD_DIM, HEAD_DIM), F32),
                        pltpu.VMEM((3, 8, gw), F32)],
        compiler_params=_cparams(("parallel", "parallel", "arbitrary")),
        name="delta_rule",
    )(qkv, qkv, qkv, conv_w, conv_w, conv_w, gate, g_col, g_row, norm_w.reshape(1, HEAD_DIM))


def _rope_tables(seq):
    pos = jnp.arange(seq, dtype=F32)
    inv_freq = 1.0 / (ROPE_BASE ** jnp.linspace(0.0, 1.0, HEAD_DIM // 2, dtype=F32))
    ang = pos[:, None] * inv_freq[None, :]
    cos, sin = jnp.cos(ang), jnp.sin(ang)
    return jnp.repeat(cos, 2, axis=1), jnp.stack([-sin, sin], axis=-1).reshape(seq, HEAD_DIM)


def _mixer_retention_swa(x, w_in, gn_w, w_out, lnw, lnb, alpha, batch, seq):
    D = x.shape[1]
    width = D // 2
    heads = width // HEAD_DIM
    proj_ret = _proj(x, w_in, 0, 4 * width, F32)
    proj_att = _proj(x, w_in, 4 * width, 3 * width, MXU_DTYPE)
    cos2, sin2 = _rope_tables(seq)
    log_g = jnp.log1p(-jnp.exp2(-5.0 - jnp.arange(heads, dtype=F32)))
    log_g = jnp.broadcast_to(log_g[:, None, None], (heads, 1, HEAD_DIM))
    y_ret = _retention(proj_ret, cos2, sin2, log_g, gn_w, batch, seq, heads)
    y_att = _swa(proj_att, batch, seq, heads)
    return _proj_ln([y_ret, y_att], w_out, x, lnw, lnb, alpha)


def _mixer_gated_deltanet(x, w_in, conv_w, a_log, dt_bias, norm_w, w_out, lnw, lnb, alpha, batch, seq):
    D = x.shape[1]
    heads = D // HEAD_DIM
    w_main = w_in[:, :4 * D].astype(MXU_DTYPE)
    qkv = _proj(x, w_main, 0, 3 * D, F32)
    gate = _proj(x, w_main, 3 * D, D, F32)
    g_col, g_row = _dn_gates(x, w_in[:, 4 * D:], a_log, dt_bias, heads)
    o = _delta_rule(qkv, conv_w, gate, g_col, g_row, norm_w, batch, seq, heads)
    return _proj_ln([o], w_out, x, lnw, lnb, alpha)


def kernel(x, ffn_w_gate, ffn_w_up, ffn_w_down, ln_w, ln_b, ab_w_in, ab_gn_w, ab_w_out,
           dn_w_in, dn_conv_w, dn_a_log, dn_dt_bias, dn_norm_w, dn_w_out):
    batch, seq, D = x.shape
    depth = ffn_w_gate.shape[0]
    alpha = (2.0 * depth) ** 0.25
    h = x.reshape(batch * seq, D)
    wg, wu, wd = (w.astype(MXU_DTYPE) for w in (ffn_w_gate, ffn_w_up, ffn_w_down))
    for l in range(depth):
        h = _ffn_ln(h, wg, wu, wd, l, 0, ln_w[l, 0], ln_b[l, 0], alpha)
        i = l // 2
        if l % 2 == 0:
            h = _mixer_retention_swa(h, ab_w_in[i].astype(MXU_DTYPE), ab_gn_w[i], ab_w_out[i].astype(MXU_DTYPE),
                                     ln_w[l, 1], ln_b[l, 1], alpha, batch, seq)
        else:
            h = _mixer_gated_deltanet(h, dn_w_in[i], dn_conv_w[i], dn_a_log[i], dn_dt_bias[i],
                                      dn_norm_w[i], dn_w_out[i].astype(MXU_DTYPE), ln_w[l, 1], ln_b[l, 1], alpha,
                                      batch, seq)
        h = _ffn_ln(h, wg, wu, wd, l, 1, ln_w[l, 2], ln_b[l, 2], alpha)
    return h.reshape(batch, seq, D)
```

```python
import functools
import math

import jax
import jax.numpy as jnp
from jax import lax
from jax.experimental import pallas as pl
from jax.experimental.pallas import tpu as pltpu

F32 = jnp.float32
MXU_DTYPE = jnp.bfloat16
HEAD_DIM = 128
LN_EPS = 1e-5
NORM_EPS = 1e-6
ROPE_BASE = 10000.0
SWA_PATTERNS = ((128, 1), (512, 4), (2048, 16))
SWA_FAR = max(w for w, _ in SWA_PATTERNS)
SWA_ROWS = 128
DN_CHUNK = 128
LN_ROWS = 32
DN_GROUP = 8
RET_CHUNK = 128
MASKED = -1e30
V7X_VMEM_LIMIT_BYTES = 56 * 1024 * 1024


def _cparams(sem):
    return pltpu.CompilerParams(dimension_semantics=sem, vmem_limit_bytes=V7X_VMEM_LIMIT_BYTES)


def _mm(a, b):
    return jnp.dot(a.astype(MXU_DTYPE), b.astype(MXU_DTYPE), preferred_element_type=F32)


def _mm_nt(a, b):
    return lax.dot_general(a.astype(MXU_DTYPE), b.astype(MXU_DTYPE), (((1,), (1,)), ((), ())),
                           preferred_element_type=F32)


def _mm_tn(a, b):
    return lax.dot_general(a.astype(MXU_DTYPE), b.astype(MXU_DTYPE), (((0,), (0,)), ((), ())),
                           preferred_element_type=F32)


def _silu(x):
    return x * jax.nn.sigmoid(x)


def _layer_norm(y, w, b):
    mu = jnp.mean(y, axis=-1, keepdims=True)
    yc = y - mu
    var = jnp.mean(yc * yc, axis=-1, keepdims=True)
    return yc * lax.rsqrt(var + LN_EPS) * w + b


def _tile(n, pref, unit):
    if n <= pref:
        return n
    t = (pref // unit) * unit
    while t > unit and n % t:
        t -= unit
    assert n % t == 0, (n, pref, unit)
    return t


def _residual_ln_rows(x_ref, o_ref, lnw_ref, lnb_ref, alpha, branch_scale, rows):
    tm = o_ref.shape[0]

    def body(r, carry):
        sl = pl.ds(pl.multiple_of(r * rows, rows), rows)
        y = alpha * x_ref[sl, :] + branch_scale * o_ref[sl, :]
        o_ref[sl, :] = _layer_norm(y, lnw_ref[...], lnb_ref[...])
        return carry

    lax.fori_loop(0, tm // rows, body, 0, unroll=2)


def _ffn_kernel(x_ref, wg_ref, wu_ref, wd_ref, lnw_ref, lnb_ref, o_ref, xb_ref, *, alpha, n_split, rows):
    j = pl.program_id(1)

    @pl.when(j == 0)
    def _():
        xb_ref[...] = x_ref[...].astype(xb_ref.dtype)
        o_ref[...] = jnp.zeros_like(o_ref)

    xb = xb_ref[...]
    g = jnp.dot(xb, wg_ref[...], preferred_element_type=F32)
    u = jnp.dot(xb, wu_ref[...], preferred_element_type=F32)
    h = (_silu(g) * u).astype(xb_ref.dtype)
    dn = o_ref.shape[1] // n_split
    for n in range(n_split):
        cols = slice(n * dn, (n + 1) * dn)
        o_ref[:, cols] += jnp.dot(h, wd_ref[:, cols], preferred_element_type=F32)

    @pl.when(j == pl.num_programs(1) - 1)
    def _():
        _residual_ln_rows(x_ref, o_ref, lnw_ref, lnb_ref, alpha, 0.5, rows)


def _ffn_ln(x, wg, wu, wd, l, k, lnw, lnb, alpha):
    T, D = x.shape
    F = wg.shape[-1]
    tm = _tile(T, 512, 8)
    tf = _tile(F, 256, 128)
    kern = functools.partial(_ffn_kernel, alpha=alpha, n_split=max(D // 512, 1), rows=LN_ROWS)
    return pl.pallas_call(
        kern,
        grid=(T // tm, F // tf),
        in_specs=[
            pl.BlockSpec((tm, D), lambda i, j: (i, 0)),
            pl.BlockSpec((None, None, D, tf), lambda i, j: (l, k, 0, j)),
            pl.BlockSpec((None, None, D, tf), lambda i, j: (l, k, 0, j)),
            pl.BlockSpec((None, None, tf, D), lambda i, j: (l, k, j, 0)),
            pl.BlockSpec((1, D), lambda i, j: (0, 0)),
            pl.BlockSpec((1, D), lambda i, j: (0, 0)),
        ],
        out_specs=pl.BlockSpec((tm, D), lambda i, j: (i, 0)),
        out_shape=jax.ShapeDtypeStruct((T, D), F32),
        scratch_shapes=[pltpu.VMEM((tm, D), MXU_DTYPE)],
        compiler_params=_cparams(("parallel", "arbitrary")),
        name="ffn_ln",
    )(x, wg, wu, wd, lnw.reshape(1, D), lnb.reshape(1, D))


def _proj_kernel(x_ref, w_ref, o_ref):
    o_ref[...] = jnp.dot(x_ref[...].astype(w_ref.dtype), w_ref[...], preferred_element_type=F32).astype(o_ref.dtype)


def _proj(x, w, col0, ncols, out_dtype):
    T, K = x.shape
    tm = _tile(T, 512, 8)
    tn = _tile(math.gcd(ncols, col0) if col0 else ncols, 2048, 128)
    j0 = col0 // tn
    return pl.pallas_call(
        _proj_kernel,
        grid=(ncols // tn, T // tm),
        in_specs=[pl.BlockSpec((tm, K), lambda j, i: (i, 0)),
                  pl.BlockSpec((K, tn), lambda j, i: (0, j0 + j), pipeline_mode=pl.Buffered(1))],
        out_specs=pl.BlockSpec((tm, tn), lambda j, i: (i, j)),
        out_shape=jax.ShapeDtypeStruct((T, ncols), out_dtype),
        compiler_params=_cparams(("parallel", "arbitrary")),
        name="in_proj",
    )(x, w)


def _conv_silu(x, prev, w):
    c = x.shape[0]
    taps = w.shape[0]
    xp = jnp.concatenate([prev, x], axis=0)
    acc = x * w[taps - 1:taps, :]
    for i in range(taps - 1):
        off = 8 - (taps - 1) + i
        acc = acc + xp[off:off + c, :] * w[i:i + 1, :]
    return _silu(acc)


def _proj_conv_kernel(x_ref, w_ref, cw_ref, o_ref, tail_ref, *, n_split, tiles_per_seq, n_q_blocks):
    j = pl.program_id(0)
    i = pl.program_id(1)
    tm, tn = o_ref.shape
    dn = tn // n_split
    xb = x_ref[...].astype(w_ref.dtype)
    seq_start = (i % tiles_per_seq) == 0
    q_scale = jnp.where(j < n_q_blocks, HEAD_DIM ** -0.5, 1.0)
    normed = j < 2 * n_q_blocks
    def product(n):
        return jnp.dot(xb, w_ref[:, n * dn:(n + 1) * dn], preferred_element_type=F32)

    raw = product(0)
    for n in range(n_split):
        cols = slice(n * dn, (n + 1) * dn)
        nxt = product(n + 1) if n + 1 < n_split else None
        prev = jnp.where(seq_start, 0.0, tail_ref[:, cols])
        tail_ref[:, cols] = raw[tm - 8:tm, :]
        y = _conv_silu(raw, prev, cw_ref[:, cols])
        for hh in range(dn // HEAD_DIM):
            hc = slice(hh * HEAD_DIM, (hh + 1) * HEAD_DIM)
            yh = y[:, hc]
            inv = lax.rsqrt(jnp.sum(yh * yh, axis=-1, keepdims=True) + NORM_EPS) * q_scale
            o_ref[:, n * dn + hh * HEAD_DIM:n * dn + (hh + 1) * HEAD_DIM] = yh * jnp.where(normed, inv, 1.0)
        raw = nxt


def _proj_conv(x, w, conv_w, width, seq):
    T, K = x.shape
    tm = _tile(seq, 512, 8)
    tn = _tile(width, 2048, HEAD_DIM)
    kern = functools.partial(_proj_conv_kernel, n_split=max(tn // 256, 1), tiles_per_seq=seq // tm,
                             n_q_blocks=width // tn)
    return pl.pallas_call(
        kern,
        grid=(3 * width // tn, T // tm),
        in_specs=[pl.BlockSpec((tm, K), lambda j, i: (i, 0)),
                  pl.BlockSpec((K, tn), lambda j, i: (0, j), pipeline_mode=pl.Buffered(1)),
                  pl.BlockSpec((conv_w.shape[0], tn), lambda j, i: (0, j))],
        out_specs=pl.BlockSpec((tm, tn), lambda j, i: (i, j)),
        out_shape=jax.ShapeDtypeStruct((T, 3 * width), F32),
        scratch_shapes=[pltpu.VMEM((8, tn), F32)],
        compiler_params=_cparams(("parallel", "arbitrary")),
        name="in_proj_conv",
    )(x, w, conv_w)


def _proj_ln_kernel(*refs, alpha, n_parts, n_split, rows):
    a_refs = refs[:n_parts]
    w_ref, x_ref, lnw_ref, lnb_ref, o_ref = refs[n_parts:]
    dn = o_ref.shape[1] // n_split
    for n in range(n_split):
        cols = slice(n * dn, (n + 1) * dn)
        k0 = 0
        acc = None
        for a_ref in a_refs:
            kp = a_ref.shape[1]
            part = jnp.dot(a_ref[...], w_ref[k0:k0 + kp, cols], preferred_element_type=F32)
            acc = part if acc is None else acc + part
            k0 += kp
        o_ref[:, cols] = acc
    _residual_ln_rows(x_ref, o_ref, lnw_ref, lnb_ref, alpha, 1.0, rows)


def _proj_ln(parts, w, x, lnw, lnb, alpha):
    T = x.shape[0]
    K, D = w.shape
    assert sum(p.shape[1] for p in parts) == K
    tm = _tile(T, 256, 8)
    kern = functools.partial(_proj_ln_kernel, alpha=alpha, n_parts=len(parts), n_split=max(D // 512, 1),
                             rows=LN_ROWS)
    return pl.pallas_call(
        kern,
        grid=(T // tm,),
        in_specs=[pl.BlockSpec((tm, p.shape[1]), lambda i: (i, 0)) for p in parts] + [
            pl.BlockSpec((K, D), lambda i: (0, 0), pipeline_mode=pl.Buffered(1)),
            pl.BlockSpec((tm, D), lambda i: (i, 0)),
            pl.BlockSpec((1, D), lambda i: (0, 0)),
            pl.BlockSpec((1, D), lambda i: (0, 0)),
        ],
        out_specs=pl.BlockSpec((tm, D), lambda i: (i, 0)),
        out_shape=jax.ShapeDtypeStruct((T, D), F32),
        compiler_params=_cparams(("parallel",)),
        name="out_proj_ln",
    )(*parts, w, x, lnw.reshape(1, D), lnb.reshape(1, D))


def _retention_kernel(q_ref, k_ref, v_ref, g_ref, cos_ref, sin_ref, lg_ref, gnw_ref, o_ref, st_ref, *, n_chunks):
    c = RET_CHUNK

    @pl.when(pl.program_id(2) == 0)
    def _():
        st_ref[...] = jnp.zeros_like(st_ref)

    lg = lg_ref[0]
    ii = lax.broadcasted_iota(jnp.int32, (c, c), 0)
    jj = lax.broadcasted_iota(jnp.int32, (c, c), 1)
    rel = (ii - jj).astype(F32)
    dmask = jnp.where(rel >= 0, jnp.exp(lg * jnp.maximum(rel, 0.0)), 0.0)
    row = ii.astype(F32)
    zeta = jnp.exp(lg * (c - 1 - row))
    xi = jnp.exp(lg * (row + 1.0))
    chunk_decay = jnp.exp(lg * c)
    even_lane = (jj & 1) == 0

    def rotate(t, cs, sn):
        partner = jnp.where(even_lane, pltpu.roll(t, HEAD_DIM - 1, 1), pltpu.roll(t, 1, 1))
        return t * cs + partner * sn

    for n in range(n_chunks):
        sl = slice(n * c, (n + 1) * c)
        cs, sn = cos_ref[sl, :], sin_ref[sl, :]
        q, k, v = q_ref[sl, :], k_ref[sl, :], v_ref[sl, :]
        qr = rotate(q, cs, sn)
        kr = rotate(k, cs, sn) * (HEAD_DIM ** -0.5)
        st = st_ref[...]
        y = _mm(_mm_nt(qr, kr) * dmask, v) + _mm(qr, st) * xi
        st_ref[...] = st * chunk_decay + _mm_tn(kr * zeta, v)
        mu = jnp.mean(y, axis=-1, keepdims=True)
        yc = y - mu
        var = jnp.mean(yc * yc, axis=-1, keepdims=True)
        yn = yc * lax.rsqrt(var + LN_EPS) * gnw_ref[...]
        o_ref[sl, :] = (yn * _silu(g_ref[sl, :])).astype(o_ref.dtype)


def _retention(proj, cos2, sin2, log_g, gn_w, batch, seq, heads):
    T = batch * seq
    ts = _tile(seq, 1024, RET_CHUNK)
    ns = seq // ts
    blk = lambda off: pl.BlockSpec((ts, HEAD_DIM), lambda b, h, s: (b * ns + s, off * heads + h))
    tab = pl.BlockSpec((ts, HEAD_DIM), lambda b, h, s: (s, 0))
    kern = functools.partial(_retention_kernel, n_chunks=ts // RET_CHUNK)
    return pl.pallas_call(
        kern,
        grid=(batch, heads, ns),
        in_specs=[blk(0), blk(1), blk(2), blk(3), tab, tab,
                  pl.BlockSpec((1, 1, HEAD_DIM), lambda b, h, s: (h, 0, 0)),
                  pl.BlockSpec((1, HEAD_DIM), lambda b, h, s: (0, h))],
        out_specs=pl.BlockSpec((ts, HEAD_DIM), lambda b, h, s: (b * ns + s, h)),
        out_shape=jax.ShapeDtypeStruct((T, heads * HEAD_DIM), MXU_DTYPE),
        scratch_shapes=[pltpu.VMEM((HEAD_DIM, HEAD_DIM), F32)],
        compiler_params=_cparams(("parallel", "parallel", "arbitrary")),
        name="retention",
    )(proj, proj, proj, proj, cos2, sin2, log_g, gn_w.reshape(1, heads * HEAD_DIM))


def _swa_kernel(q_ref, k_ref, v_ref, bias_ref, o_ref, *, tq, win):
    qi = pl.program_id(2)
    start = pl.multiple_of(jnp.maximum(qi * tq - SWA_FAR, 0), tq)
    kw = k_ref[pl.ds(start, win), :]
    vw = v_ref[pl.ds(start, win), :]
    rows = [slice(r * SWA_ROWS, (r + 1) * SWA_ROWS) for r in range(tq // SWA_ROWS)]
    s = [_mm_nt(q_ref[r, :], kw) * (HEAD_DIM ** -0.5) + bias_ref[0, r, :] for r in rows]
    p = [jnp.exp(si - jnp.max(si, axis=-1, keepdims=True)) for si in s]
    l = [jnp.sum(pi, axis=-1, keepdims=True) for pi in p]
    for r, pi, li in zip(rows, p, l):
        o_ref[r, :] = (_mm(pi, vw) / li).astype(o_ref.dtype)


def _swa_bias(tq, win):
    nvar = SWA_FAR // tq + 1
    v = lax.broadcasted_iota(jnp.int32, (nvar, tq, win), 0)
    i = lax.broadcasted_iota(jnp.int32, (nvar, tq, win), 1)
    c = lax.broadcasted_iota(jnp.int32, (nvar, tq, win), 2)
    dist = SWA_FAR - v * tq + i - c
    mult = jnp.zeros((nvar, tq, win), F32)
    for window, dilation in SWA_PATTERNS:
        mult += ((dist >= 0) & (dist <= window) & (dist % dilation == 0)).astype(F32)
    return jnp.where(mult > 0, jnp.log(jnp.maximum(mult, 1.0)), MASKED)


def _swa(proj, batch, seq, heads):
    T = batch * seq
    tq = _tile(seq, 256, 128)
    assert SWA_FAR % tq == 0 and seq >= SWA_FAR + tq
    win = SWA_FAR + tq
    nq = seq // tq
    nfar = SWA_FAR // tq
    kern = functools.partial(_swa_kernel, tq=tq, win=win)
    return pl.pallas_call(
        kern,
        grid=(batch, heads, nq),
        in_specs=[
            pl.BlockSpec((tq, HEAD_DIM), lambda b, h, i: (b * nq + i, h)),
            pl.BlockSpec((seq, HEAD_DIM), lambda b, h, i: (b, heads + h)),
            pl.BlockSpec((seq, HEAD_DIM), lambda b, h, i: (b, 2 * heads + h)),
            pl.BlockSpec((1, tq, win), lambda b, h, i: (jnp.maximum(nfar - i, 0), 0, 0)),
        ],
        out_specs=pl.BlockSpec((tq, HEAD_DIM), lambda b, h, i: (b * nq + i, h)),
        out_shape=jax.ShapeDtypeStruct((T, heads * HEAD_DIM), MXU_DTYPE),
        compiler_params=_cparams(("parallel", "parallel", "arbitrary")),
        name="dilated_swa",
    )(proj, proj, proj, _swa_bias(tq, win))


def _softplus(x):
    return jnp.maximum(x, 0.0) + jnp.log1p(jnp.exp(-jnp.abs(x)))


def _dn_gates_kernel(x_ref, w_ref, pc_ref, col_ref, row_ref, *, heads):
    c = DN_CHUNK
    xb = x_ref[...].astype(MXU_DTYPE)
    col = jnp.dot(xb, w_ref[...], preferred_element_type=F32)
    ii = lax.broadcasted_iota(jnp.int32, (c, c), 0)
    jj = lax.broadcasted_iota(jnp.int32, (c, c), 1)
    incl = (ii >= jj).astype(F32)
    g_col = -jnp.exp(pc_ref[0:1, :]) * _softplus(col + pc_ref[1:2, :])
    gc_col = jnp.dot(incl, g_col, preferred_element_type=F32, precision=lax.Precision.HIGHEST)
    lane = lax.broadcasted_iota(jnp.int32, col.shape, 1)
    out = jnp.where(lane < heads, gc_col, jax.nn.sigmoid(col))
    col_ref[...] = out
    row_ref[...] = out.T


def _dn_gates(x, w_small, a_log, dt_bias, heads):
    T, D = x.shape
    assert 2 * heads <= 128
    pad = 128 - 2 * heads
    w = jnp.pad(w_small, ((0, 0), (0, pad))).astype(MXU_DTYPE)
    zeros = jnp.zeros((128 - heads,), F32)
    params = jnp.stack([jnp.concatenate([a_log.astype(F32), zeros]), jnp.concatenate([dt_bias.astype(F32), zeros])])
    c = DN_CHUNK
    kern = functools.partial(_dn_gates_kernel, heads=heads)
    return pl.pallas_call(
        kern,
        grid=(T // c,),
        in_specs=[
            pl.BlockSpec((c, D), lambda i: (i, 0)),
            pl.BlockSpec((D, 128), lambda i: (0, 0)),
            pl.BlockSpec((2, 128), lambda i: (0, 0)),
        ],
        out_specs=[pl.BlockSpec((c, 128), lambda i: (i, 0)), pl.BlockSpec((128, c), lambda i: (0, i))],
        out_shape=[jax.ShapeDtypeStruct((T, 128), F32), jax.ShapeDtypeStruct((128, T), F32)],
        compiler_params=_cparams(("parallel",)),
        name="dn_gates",
    )(x, w, params)


def _dn_kernel(q_ref, k_ref, v_ref, gate_ref, col_ref, row_ref, nw_ref, o_ref, st_ref, *, group, heads):
    c = DN_CHUNK
    hs = range(group)

    @pl.when(pl.program_id(2) == 0)
    def _():
        st_ref[...] = jnp.zeros_like(st_ref)

    ii = lax.broadcasted_iota(jnp.int32, (c, c), 0)
    jj = lax.broadcasted_iota(jnp.int32, (c, c), 1)
    eye = (ii == jj).astype(F32)
    n_levels = int(math.log2(c))
    level = [(ii >> 1) == (jj >> 1)] + [
        ((ii >> (ls + 1)) == (jj >> (ls + 1))) & (((ii >> ls) & 1) == 1) & (((jj >> ls) & 1) == 0)
        for ls in range(1, n_levels)]
    col = col_ref[...]
    lane = lax.broadcasted_iota(jnp.int32, col.shape, 1)
    cols = [slice(h * HEAD_DIM, (h + 1) * HEAD_DIM) for h in hs]
    head = [pl.program_id(1) * group + h for h in hs]
    q = [q_ref[:, cols[h]] for h in hs]
    k = [k_ref[:, cols[h]] for h in hs]
    v = [v_ref[:, cols[h]] for h in hs]
    gc_i =[jnp.sum(jnp.where(lane == head[h], col, 0.0), axis=-1, keepdims=True) for h in hs]
    beta = [jnp.sum(jnp.where(lane == head[h] + heads, col, 0.0), axis=-1, keepdims=True) for h in hs]
    gc_j = [row_ref[pl.ds(head[h], 1), :] for h in hs]
    gc_last = [gc_i[h][c - 1:c, :] for h in hs]
    decay = [jnp.exp(jnp.where(ii >= jj, gc_i[h] - gc_j[h], MASKED)) for h in hs]
    e_g = [jnp.exp(gc_i[h]) for h in hs]
    kb = [k[h] * beta[h] for h in hs]
    kq = [_mm_nt(jnp.concatenate([kb[h], q[h]], axis=0), k[h]) for h in hs]
    a = [jnp.where(ii > jj, kq[h][:c] * decay[h], 0.0) for h in hs]
    qk = [kq[h][c:] * decay[h] for h in hs]
    w0 = [jnp.where(level[0], a[h], 0.0) for h in hs]
    t = [eye - w0[h] for h in hs]
    z = [a[h] - _mm(a[h], w0[h]) for h in hs]
    for ls in range(1, n_levels):
        wl = [jnp.where(level[ls], z[h], 0.0) for h in hs]
        if ls < n_levels - 1:
            tz = [_mm(jnp.concatenate([t[h], z[h]], axis=0), wl[h]) for h in hs]
            t = [t[h] - tz[h][:c] for h in hs]
            z = [z[h] - tz[h][c:] for h in hs]
        else:
            t = [t[h] - _mm(t[h], wl[h]) for h in hs]
    uw = [_mm(t[h], jnp.concatenate([v[h] * beta[h], kb[h] * e_g[h]], axis=1)) for h in hs]
    st = [st_ref[h] for h in hs]
    ws_qs = [_mm(jnp.concatenate([uw[h][:, HEAD_DIM:], q[h] * e_g[h]], axis=0), st[h]) for h in hs]
    v_new = [uw[h][:, :HEAD_DIM] - ws_qs[h][:c] for h in hs]
    o = [ws_qs[h][c:] + _mm(qk[h], v_new[h]) for h in hs]
    for h in hs:
        st_ref[h] = st[h] * jnp.exp(gc_last[h]) + _mm_tn(k[h] * jnp.exp(gc_last[h] - gc_i[h]), v_new[h])
    for h in hs:
        on = o[h] * lax.rsqrt(jnp.mean(o[h] * o[h], axis=-1, keepdims=True) + NORM_EPS) * nw_ref[...]
        o_ref[:, cols[h]] = (on * _silu(gate_ref[:, cols[h]])).astype(o_ref.dtype)


def _delta_rule(qkv, gate, g_col, g_row, norm_w, batch, seq, heads):
    T = batch * seq
    c = DN_CHUNK
    group = _tile(heads, DN_GROUP, 1)
    ng = heads // group
    nc = seq // c
    gw = group * HEAD_DIM
    blk = lambda off: pl.BlockSpec((c, gw), lambda b, g, n: (b * nc + n, off * ng + g))
    kern = functools.partial(_dn_kernel, group=group, heads=heads)
    return pl.pallas_call(
        kern,
        grid=(batch, ng, nc),
        in_specs=[blk(0), blk(1), blk(2),
                  pl.BlockSpec((c, gw), lambda b, g, n: (b * nc + n, g)),
                  pl.BlockSpec((c, 128), lambda b, g, n: (b * nc + n, 0)),
                  pl.BlockSpec((128, c), lambda b, g, n: (0, b * nc + n)),
                  pl.BlockSpec((1, HEAD_DIM), lambda b, g, n: (0, 0))],
        out_specs=pl.BlockSpec((c, gw), lambda b, g, n: (b * nc + n, g)),
        out_shape=jax.ShapeDtypeStruct((T, heads * HEAD_DIM), MXU_DTYPE),
        scratch_shapes=[pltpu.VMEM((group, HEAD_DIM, HEAD_DIM), F32)],
        compiler_params=_cparams(("parallel", "parallel", "arbitrary")),
        name="delta_rule",
    )(qkv, qkv, qkv, gate, g_col, g_row, norm_w.reshape(1, HEAD_DIM))


def _rope_tables(seq):
    pos = jnp.arange(seq, dtype=F32)
    inv_freq = 1.0 / (ROPE_BASE ** jnp.linspace(0.0, 1.0, HEAD_DIM // 2, dtype=F32))
    ang = pos[:, None] * inv_freq[None, :]
    cos, sin = jnp.cos(ang), jnp.sin(ang)
    return jnp.repeat(cos, 2, axis=1), jnp.stack([-sin, sin], axis=-1).reshape(seq, HEAD_DIM)


def _mixer_retention_swa(x, w_in, gn_w, w_out, lnw, lnb, alpha, batch, seq):
    D = x.shape[1]
    width = D // 2
    heads = width // HEAD_DIM
    proj_ret = _proj(x, w_in, 0, 4 * width, F32)
    proj_att = _proj(x, w_in, 4 * width, 3 * width, MXU_DTYPE)
    cos2, sin2 = _rope_tables(seq)
    log_g = jnp.log1p(-jnp.exp2(-5.0 - jnp.arange(heads, dtype=F32)))
    log_g = jnp.broadcast_to(log_g[:, None, None], (heads, 1, HEAD_DIM))
    y_ret = _retention(proj_ret, cos2, sin2, log_g, gn_w, batch, seq, heads)
    y_att = _swa(proj_att, batch, seq, heads)
    return _proj_ln([y_ret, y_att], w_out, x, lnw, lnb, alpha)


def _mixer_gated_deltanet(x, w_in, conv_w, a_log, dt_bias, norm_w, w_out, lnw, lnb, alpha, batch, seq):
    D = x.shape[1]
    heads = D // HEAD_DIM
    w_main = w_in[:, :4 * D].astype(MXU_DTYPE)
    qkv = _proj_conv(x, w_main, conv_w, D, seq)
    gate = _proj(x, w_main, 3 * D, D, F32)
    g_col, g_row = _dn_gates(x, w_in[:, 4 * D:], a_log, dt_bias, heads)
    o = _delta_rule(qkv, gate, g_col, g_row, norm_w, batch, seq, heads)
    return _proj_ln([o], w_out, x, lnw, lnb, alpha)


def kernel(x, ffn_w_gate, ffn_w_up, ffn_w_down, ln_w, ln_b, ab_w_in, ab_gn_w, ab_w_out,
           dn_w_in, dn_conv_w, dn_a_log, dn_dt_bias, dn_norm_w, dn_w_out):
    batch, seq, D = x.shape
    depth = ffn_w_gate.shape[0]
    alpha = (2.0 * depth) ** 0.25
    h = x.reshape(batch * seq, D)
    wg, wu, wd = (w.astype(MXU_DTYPE) for w in (ffn_w_gate, ffn_w_up, ffn_w_down))
    for l in range(depth):
        h = _ffn_ln(h, wg, wu, wd, l, 0, ln_w[l, 0], ln_b[l, 0], alpha)
        i = l // 2
        if l % 2 == 0:
            h = _mixer_retention_swa(h, ab_w_in[i].astype(MXU_DTYPE), ab_gn_w[i], ab_w_out[i].astype(MXU_DTYPE),
                                     ln_w[l, 1], ln_b[l, 1], alpha, batch, seq)
        else:
            h = _mixer_gated_deltanet(h, dn_w_in[i], dn_conv_w[i], dn_a_log[i], dn_dt_bias[i],
                                      dn_norm_w[i], dn_w_out[i].astype(MXU_DTYPE), ln_w[l, 1], ln_b[l, 1], alpha,
                                      batch, seq)
        h = _ffn_ln(h, wg, wu, wd, l, 1, ln_w[l, 2], ln_b[l, 2], alpha)
    return h.reshape(batch, seq, D)
```

```python
import functools
import math

import jax
import jax.numpy as jnp
from jax import lax
from jax.experimental import pallas as pl
from jax.experimental.pallas import tpu as pltpu

F32 = jnp.float32
MXU_DTYPE = jnp.bfloat16
HEAD_DIM = 128
LN_EPS = 1e-5
NORM_EPS = 1e-6
ROPE_BASE = 10000.0
SWA_PATTERNS = ((128, 1), (512, 4), (2048, 16))
SWA_FAR = max(w for w, _ in SWA_PATTERNS)
SWA_ROWS = 128
DN_CHUNK = 128
LN_ROWS = 128
DN_GROUP = 8
RET_CHUNK = 128
MASKED = -1e30
V7X_VMEM_LIMIT_BYTES = 56 * 1024 * 1024


def _cparams(sem):
    return pltpu.CompilerParams(dimension_semantics=sem, vmem_limit_bytes=V7X_VMEM_LIMIT_BYTES)


def _mm(a, b):
    return jnp.dot(a.astype(MXU_DTYPE), b.astype(MXU_DTYPE), preferred_element_type=F32)


def _mm_nt(a, b):
    return lax.dot_general(a.astype(MXU_DTYPE), b.astype(MXU_DTYPE), (((1,), (1,)), ((), ())),
                           preferred_element_type=F32)


def _mm_tn(a, b):
    return lax.dot_general(a.astype(MXU_DTYPE), b.astype(MXU_DTYPE), (((0,), (0,)), ((), ())),
                           preferred_element_type=F32)


def _silu(x):
    return x * jax.nn.sigmoid(x)


def _layer_norm(y, w, b):
    mu = jnp.mean(y, axis=-1, keepdims=True)
    yc = y - mu
    var = jnp.mean(yc * yc, axis=-1, keepdims=True)
    return yc * lax.rsqrt(var + LN_EPS) * w + b


def _tile(n, pref, unit):
    if n <= pref:
        return n
    t = (pref // unit) * unit
    while t > unit and n % t:
        t -= unit
    assert n % t == 0, (n, pref, unit)
    return t


def _residual_ln_rows(x_ref, o_ref, lnw_ref, lnb_ref, alpha, rows):
    tm, d = o_ref.shape
    slabs = [slice(c * 128, (c + 1) * 128) for c in range(d // 128)]

    def body(r, carry):
        sl = pl.ds(pl.multiple_of(r * rows, rows), rows)
        s1 = jnp.zeros((rows, 128), F32)
        for cs in slabs:
            y = alpha * x_ref[sl, cs] + o_ref[sl, cs]
            o_ref[sl, cs] = y
            s1 = s1 + y
        mu = jnp.sum(s1, axis=-1, keepdims=True) * (1.0 / d)
        s2 = jnp.zeros((rows, 128), F32)
        for cs in slabs:
            yc = o_ref[sl, cs] - mu
            s2 = s2 + yc * yc
        rstd = lax.rsqrt(jnp.sum(s2, axis=-1, keepdims=True) * (1.0 / d) + LN_EPS)
        for cs in slabs:
            o_ref[sl, cs] = (o_ref[sl, cs] - mu) * rstd * lnw_ref[:, cs] + lnb_ref[:, cs]
        return carry

    lax.fori_loop(0, tm // rows, body, 0)


def _ffn_kernel(x_ref, wg_ref, wu_ref, wd_ref, lnw_ref, lnb_ref, o_ref, xb_ref, *, alpha, n_split, rows):
    j = pl.program_id(1)

    @pl.when(j == 0)
    def _():
        xb_ref[...] = x_ref[...].astype(xb_ref.dtype)
        o_ref[...] = jnp.zeros_like(o_ref)

    xb = xb_ref[...]
    g = jnp.dot(xb, wg_ref[...], preferred_element_type=F32)
    u = jnp.dot(xb, wu_ref[...], preferred_element_type=F32)
    h = (_silu(g) * u * 0.5).astype(xb_ref.dtype)
    dn = o_ref.shape[1] // n_split
    for n in range(n_split):
        cols = slice(n * dn, (n + 1) * dn)
        o_ref[:, cols] += jnp.dot(h, wd_ref[:, cols], preferred_element_type=F32)

    @pl.when(j == pl.num_programs(1) - 1)
    def _():
        _residual_ln_rows(x_ref, o_ref, lnw_ref, lnb_ref, alpha, rows)


def _ffn_ln(x, wg, wu, wd, l, k, lnw, lnb, alpha):
    T, D = x.shape
    F = wg.shape[-1]
    tm = _tile(T, 1024, 8)
    tf = _tile(F, 256, 128)
    kern = functools.partial(_ffn_kernel, alpha=alpha, n_split=max(D // 512, 1), rows=LN_ROWS)
    once = pl.Buffered(1)
    return pl.pallas_call(
        kern,
        grid=(T // tm, F // tf),
        in_specs=[
            pl.BlockSpec((tm, D), lambda i, j: (i, 0), pipeline_mode=once),
            pl.BlockSpec((None, None, D, tf), lambda i, j: (l, k, 0, j)),
            pl.BlockSpec((None, None, D, tf), lambda i, j: (l, k, 0, j)),
            pl.BlockSpec((None, None, tf, D), lambda i, j: (l, k, j, 0)),
            pl.BlockSpec((1, D), lambda i, j: (0, 0)),
            pl.BlockSpec((1, D), lambda i, j: (0, 0)),
        ],
        out_specs=pl.BlockSpec((tm, D), lambda i, j: (i, 0), pipeline_mode=once),
        out_shape=jax.ShapeDtypeStruct((T, D), F32),
        scratch_shapes=[pltpu.VMEM((tm, D), MXU_DTYPE)],
        compiler_params=_cparams(("parallel", "arbitrary")),
        name="ffn_ln",
    )(x, wg, wu, wd, lnw.reshape(1, D), lnb.reshape(1, D))


def _proj_kernel(x_ref, w_ref, o_ref):
    o_ref[...] = jnp.dot(x_ref[...].astype(w_ref.dtype), w_ref[...], preferred_element_type=F32).astype(o_ref.dtype)


def _proj(x, w, col0, ncols, out_dtype):
    T, K = x.shape
    tm = _tile(T, 512, 8)
    tn = _tile(math.gcd(ncols, col0) if col0 else ncols, 2048, 128)
    j0 = col0 // tn
    return pl.pallas_call(
        _proj_kernel,
        grid=(ncols // tn, T // tm),
        in_specs=[pl.BlockSpec((tm, K), lambda j, i: (i, 0)),
                  pl.BlockSpec((K, tn), lambda j, i: (0, j0 + j), pipeline_mode=pl.Buffered(1))],
        out_specs=pl.BlockSpec((tm, tn), lambda j, i: (i, j)),
        out_shape=jax.ShapeDtypeStruct((T, ncols), out_dtype),
        compiler_params=_cparams(("parallel", "arbitrary")),
        name="in_proj",
    )(x, w)


def _proj_ln_kernel(*refs, alpha, n_parts, n_split, rows):
    a_refs = refs[:n_parts]
    w_ref, x_ref, lnw_ref, lnb_ref, o_ref = refs[n_parts:]
    dn = o_ref.shape[1] // n_split
    for n in range(n_split):
        cols = slice(n * dn, (n + 1) * dn)
        k0 = 0
        acc = None
        for a_ref in a_refs:
            kp = a_ref.shape[1]
            part = jnp.dot(a_ref[...], w_ref[k0:k0 + kp, cols], preferred_element_type=F32)
            acc = part if acc is None else acc + part
            k0 += kp
        o_ref[:, cols] = acc
    _residual_ln_rows(x_ref, o_ref, lnw_ref, lnb_ref, alpha, rows)


def _proj_ln(parts, w, x, lnw, lnb, alpha):
    T = x.shape[0]
    K, D = w.shape
    assert sum(p.shape[1] for p in parts) == K
    tm = _tile(T, 256, 8)
    kern = functools.partial(_proj_ln_kernel, alpha=alpha, n_parts=len(parts), n_split=max(D // 512, 1),
                             rows=LN_ROWS)
    return pl.pallas_call(
        kern,
        grid=(T // tm,),
        in_specs=[pl.BlockSpec((tm, p.shape[1]), lambda i: (i, 0)) for p in parts] + [
            pl.BlockSpec((K, D), lambda i: (0, 0), pipeline_mode=pl.Buffered(1)),
            pl.BlockSpec((tm, D), lambda i: (i, 0)),
            pl.BlockSpec((1, D), lambda i: (0, 0)),
            pl.BlockSpec((1, D), lambda i: (0, 0)),
        ],
        out_specs=pl.BlockSpec((tm, D), lambda i: (i, 0)),
        out_shape=jax.ShapeDtypeStruct((T, D), F32),
        compiler_params=_cparams(("parallel",)),
        name="out_proj_ln",
    )(*parts, w, x, lnw.reshape(1, D), lnb.reshape(1, D))


def _retention_kernel(q_ref, k_ref, v_ref, g_ref, cos_ref, sin_ref, lg_ref, gnw_ref, o_ref, st_ref, *, n_chunks):
    c = RET_CHUNK

    @pl.when(pl.program_id(2) == 0)
    def _():
        st_ref[...] = jnp.zeros_like(st_ref)

    lg = lg_ref[0]
    ii = lax.broadcasted_iota(jnp.int32, (c, c), 0)
    jj = lax.broadcasted_iota(jnp.int32, (c, c), 1)
    rel = (ii - jj).astype(F32)
    dmask = jnp.where(rel >= 0, jnp.exp(lg * jnp.maximum(rel, 0.0)), 0.0)
    row = ii.astype(F32)
    zeta = jnp.exp(lg * (c - 1 - row))
    xi = jnp.exp(lg * (row + 1.0))
    chunk_decay = jnp.exp(lg * c)
    even_lane = (jj & 1) == 0

    def rotate(t, cs, sn):
        partner = jnp.where(even_lane, pltpu.roll(t, HEAD_DIM - 1, 1), pltpu.roll(t, 1, 1))
        return t * cs + partner * sn

    chunks = [slice(n * c, (n + 1) * c) for n in range(n_chunks)]
    qr = [rotate(q_ref[sl, :], cos_ref[sl, :], sin_ref[sl, :]) for sl in chunks]
    kr = [rotate(k_ref[sl, :], cos_ref[sl, :], sin_ref[sl, :]) * (HEAD_DIM ** -0.5) for sl in chunks]
    intra = [_mm(_mm_nt(qr[n], kr[n]) * dmask, v_ref[sl, :]) for n, sl in enumerate(chunks)]
    kv = [_mm_tn(kr[n] * zeta, v_ref[sl, :]) for n, sl in enumerate(chunks)]
    st = st_ref[...]
    for n, sl in enumerate(chunks):
        y = intra[n] + _mm(qr[n], st) * xi
        st = st * chunk_decay + kv[n]
        mu = jnp.mean(y, axis=-1, keepdims=True)
        yc = y - mu
        var = jnp.mean(yc * yc, axis=-1, keepdims=True)
        yn = yc * lax.rsqrt(var + LN_EPS) * gnw_ref[...]
        o_ref[sl, :] = (yn * _silu(g_ref[sl, :])).astype(o_ref.dtype)
    st_ref[...] = st


def _retention(proj, cos2, sin2, log_g, gn_w, batch, seq, heads):
    T = batch * seq
    ts = _tile(seq, 1024, RET_CHUNK)
    ns = seq // ts
    blk = lambda off: pl.BlockSpec((ts, HEAD_DIM), lambda b, h, s: (b * ns + s, off * heads + h))
    tab = pl.BlockSpec((ts, HEAD_DIM), lambda b, h, s: (s, 0))
    kern = functools.partial(_retention_kernel, n_chunks=ts // RET_CHUNK)
    return pl.pallas_call(
        kern,
        grid=(batch, heads, ns),
        in_specs=[blk(0), blk(1), blk(2), blk(3), tab, tab,
                  pl.BlockSpec((1, 1, HEAD_DIM), lambda b, h, s: (h, 0, 0)),
                  pl.BlockSpec((1, HEAD_DIM), lambda b, h, s: (0, h))],
        out_specs=pl.BlockSpec((ts, HEAD_DIM), lambda b, h, s: (b * ns + s, h)),
        out_shape=jax.ShapeDtypeStruct((T, heads * HEAD_DIM), MXU_DTYPE),
        scratch_shapes=[pltpu.VMEM((HEAD_DIM, HEAD_DIM), F32)],
        compiler_params=_cparams(("parallel", "parallel", "arbitrary")),
        name="retention",
    )(proj, proj, proj, proj, cos2, sin2, log_g, gn_w.reshape(1, heads * HEAD_DIM))


def _swa_kernel(q_ref, k_ref, v_ref, bias_ref, o_ref, *, tq, win):
    qi = pl.program_id(2)
    start = pl.multiple_of(jnp.maximum(qi * tq - SWA_FAR, 0), tq)
    kw = k_ref[pl.ds(start, win), :]
    vw = v_ref[pl.ds(start, win), :]
    rows = [slice(r * SWA_ROWS, (r + 1) * SWA_ROWS) for r in range(tq // SWA_ROWS)]
    s = [_mm_nt(q_ref[r, :], kw) * (HEAD_DIM ** -0.5) + bias_ref[0, r, :] for r in rows]
    p = [jnp.exp(si - jnp.max(si, axis=-1, keepdims=True)) for si in s]
    l = [jnp.sum(pi, axis=-1, keepdims=True) for pi in p]
    for r, pi, li in zip(rows, p, l):
        o_ref[r, :] = (_mm(pi, vw) / li).astype(o_ref.dtype)


def _swa_bias(tq, win):
    nvar = SWA_FAR // tq + 1
    v = lax.broadcasted_iota(jnp.int32, (nvar, tq, win), 0)
    i = lax.broadcasted_iota(jnp.int32, (nvar, tq, win), 1)
    c = lax.broadcasted_iota(jnp.int32, (nvar, tq, win), 2)
    dist = SWA_FAR - v * tq + i - c
    mult = jnp.zeros((nvar, tq, win), F32)
    for window, dilation in SWA_PATTERNS:
        mult += ((dist >= 0) & (dist <= window) & (dist % dilation == 0)).astype(F32)
    return jnp.where(mult > 0, jnp.log(jnp.maximum(mult, 1.0)), MASKED)


def _swa(proj, batch, seq, heads):
    T = batch * seq
    tq = _tile(seq, 512, 128)
    assert SWA_FAR % tq == 0 and seq >= SWA_FAR + tq
    win = SWA_FAR + tq
    nq = seq // tq
    nfar = SWA_FAR // tq
    kern = functools.partial(_swa_kernel, tq=tq, win=win)
    return pl.pallas_call(
        kern,
        grid=(batch, heads, nq),
        in_specs=[
            pl.BlockSpec((tq, HEAD_DIM), lambda b, h, i: (b * nq + i, h)),
            pl.BlockSpec((seq, HEAD_DIM), lambda b, h, i: (b, heads + h)),
            pl.BlockSpec((seq, HEAD_DIM), lambda b, h, i: (b, 2 * heads + h)),
            pl.BlockSpec((1, tq, win), lambda b, h, i: (jnp.maximum(nfar - i, 0), 0, 0)),
        ],
        out_specs=pl.BlockSpec((tq, HEAD_DIM), lambda b, h, i: (b * nq + i, h)),
        out_shape=jax.ShapeDtypeStruct((T, heads * HEAD_DIM), MXU_DTYPE),
        compiler_params=_cparams(("parallel", "parallel", "arbitrary")),
        name="dilated_swa",
    )(proj, proj, proj, _swa_bias(tq, win))


def _softplus(x):
    return jnp.maximum(x, 0.0) + jnp.log1p(jnp.exp(-jnp.abs(x)))


def _dn_gates_kernel(x_ref, w_ref, pc_ref, col_ref, row_ref, *, heads):
    c = DN_CHUNK
    xb = x_ref[...].astype(MXU_DTYPE)
    col = jnp.dot(xb, w_ref[...], preferred_element_type=F32)
    ii = lax.broadcasted_iota(jnp.int32, (c, c), 0)
    jj = lax.broadcasted_iota(jnp.int32, (c, c), 1)
    incl = (ii >= jj).astype(F32)
    g_col = -jnp.exp(pc_ref[0:1, :]) * _softplus(col + pc_ref[1:2, :])
    gc_col = jnp.dot(incl, g_col, preferred_element_type=F32, precision=lax.Precision.HIGHEST)
    lane = lax.broadcasted_iota(jnp.int32, col.shape, 1)
    out = jnp.where(lane < heads, gc_col, jax.nn.sigmoid(col))
    col_ref[...] = out
    row_ref[...] = out.T


def _dn_gates(x, w_small, a_log, dt_bias, heads):
    T, D = x.shape
    assert 2 * heads <= 128
    pad = 128 - 2 * heads
    w = jnp.pad(w_small, ((0, 0), (0, pad))).astype(MXU_DTYPE)
    zeros = jnp.zeros((128 - heads,), F32)
    params = jnp.stack([jnp.concatenate([a_log.astype(F32), zeros]), jnp.concatenate([dt_bias.astype(F32), zeros])])
    c = DN_CHUNK
    kern = functools.partial(_dn_gates_kernel, heads=heads)
    return pl.pallas_call(
        kern,
        grid=(T // c,),
        in_specs=[
            pl.BlockSpec((c, D), lambda i: (i, 0)),
            pl.BlockSpec((D, 128), lambda i: (0, 0)),
            pl.BlockSpec((2, 128), lambda i: (0, 0)),
        ],
        out_specs=[pl.BlockSpec((c, 128), lambda i: (i, 0)), pl.BlockSpec((128, c), lambda i: (0, i))],
        out_shape=[jax.ShapeDtypeStruct((T, 128), F32), jax.ShapeDtypeStruct((128, T), F32)],
        compiler_params=_cparams(("parallel",)),
        name="dn_gates",
    )(x, w, params)


def _conv_silu(x, prev, w):
    c = x.shape[0]
    taps = w.shape[0]
    xp = jnp.concatenate([prev, x], axis=0)
    acc = x * w[taps - 1:taps, :]
    for i in range(taps - 1):
        off = 8 - (taps - 1) + i
        acc = acc + xp[off:off + c, :] * w[i:i + 1, :]
    return _silu(acc)


def _l2norm(t):
    return t * lax.rsqrt(jnp.sum(t * t, axis=-1, keepdims=True) + NORM_EPS)


def _dn_kernel(q_ref, k_ref, v_ref, cwq_ref, cwk_ref, cwv_ref, gate_ref, col_ref, row_ref, nw_ref, o_ref,
               st_ref, tail_ref, *, group, heads):
    c = DN_CHUNK
    hs = range(group)

    @pl.when(pl.program_id(2) == 0)
    def _():
        st_ref[...] = jnp.zeros_like(st_ref)
        tail_ref[...] = jnp.zeros_like(tail_ref)

    ii = lax.broadcasted_iota(jnp.int32, (c, c), 0)
    jj = lax.broadcasted_iota(jnp.int32, (c, c), 1)
    eye = (ii == jj).astype(F32)
    n_levels = int(math.log2(c))
    level = [(ii >> 1) == (jj >> 1)] + [
        ((ii >> (ls + 1)) == (jj >> (ls + 1))) & (((ii >> ls) & 1) == 1) & (((jj >> ls) & 1) == 0)
        for ls in range(1, n_levels)]
    col = col_ref[...]
    lane = lax.broadcasted_iota(jnp.int32, col.shape, 1)
    cols = [slice(h * HEAD_DIM, (h + 1) * HEAD_DIM) for h in hs]
    head = [pl.program_id(1) * group + h for h in hs]
    q = [_l2norm(_conv_silu(q_ref[:, cols[h]], tail_ref[0, :, cols[h]], cwq_ref[:, cols[h]])) * (HEAD_DIM ** -0.5)
         for h in hs]
    k = [_l2norm(_conv_silu(k_ref[:, cols[h]], tail_ref[1, :, cols[h]], cwk_ref[:, cols[h]])) for h in hs]
    v = [_conv_silu(v_ref[:, cols[h]], tail_ref[2, :, cols[h]], cwv_ref[:, cols[h]]) for h in hs]
    for i, ref in enumerate((q_ref, k_ref, v_ref)):
        tail_ref[i] = ref[c - 8:c, :]
    gc_i =[jnp.sum(jnp.where(lane == head[h], col, 0.0), axis=-1, keepdims=True) for h in hs]
    beta = [jnp.sum(jnp.where(lane == head[h] + heads, col, 0.0), axis=-1, keepdims=True) for h in hs]
    gc_j = [row_ref[pl.ds(head[h], 1), :] for h in hs]
    gc_last = [gc_i[h][c - 1:c, :] for h in hs]
    decay = [jnp.exp(jnp.where(ii >= jj, gc_i[h] - gc_j[h], MASKED)) for h in hs]
    e_g = [jnp.exp(gc_i[h]) for h in hs]
    kb = [k[h] * beta[h] for h in hs]
    kq = [_mm_nt(jnp.concatenate([kb[h], q[h]], axis=0), k[h]) for h in hs]
    a = [jnp.where(ii > jj, kq[h][:c] * decay[h], 0.0) for h in hs]
    qk = [kq[h][c:] * decay[h] for h in hs]
    w0 = [jnp.where(level[0], a[h], 0.0) for h in hs]
    t = [eye - w0[h] for h in hs]
    z = [a[h] - _mm(a[h], w0[h]) for h in hs]
    for ls in range(1, n_levels):
        wl = [jnp.where(level[ls], z[h], 0.0) for h in hs]
        if ls < n_levels - 1:
            tz = [_mm(jnp.concatenate([t[h], z[h]], axis=0), wl[h]) for h in hs]
            t = [t[h] - tz[h][:c] for h in hs]
            z = [z[h] - tz[h][c:] for h in hs]
        else:
            t = [t[h] - _mm(t[h], wl[h]) for h in hs]
    uw = [_mm(t[h], jnp.concatenate([v[h] * beta[h], kb[h] * e_g[h]], axis=1)) for h in hs]
    st = [st_ref[h] for h in hs]
    ws_qs = [_mm(jnp.concatenate([uw[h][:, HEAD_DIM:], q[h] * e_g[h]], axis=0), st[h]) for h in hs]
    v_new = [uw[h][:, :HEAD_DIM] - ws_qs[h][:c] for h in hs]
    o = [ws_qs[h][c:] + _mm(qk[h], v_new[h]) for h in hs]
    for h in hs:
        st_ref[h] = st[h] * jnp.exp(gc_last[h]) + _mm_tn(k[h] * jnp.exp(gc_last[h] - gc_i[h]), v_new[h])
    for h in hs:
        on = o[h] * lax.rsqrt(jnp.mean(o[h] * o[h], axis=-1, keepdims=True) + NORM_EPS) * nw_ref[...]
        o_ref[:, cols[h]] = (on * _silu(gate_ref[:, cols[h]])).astype(o_ref.dtype)


def _delta_rule(qkv, conv_w, gate, g_col, g_row, norm_w, batch, seq, heads):
    T = batch * seq
    c = DN_CHUNK
    group = _tile(heads, DN_GROUP, 1)
    ng = heads // group
    nc = seq // c
    gw = group * HEAD_DIM
    taps = conv_w.shape[0]
    blk = lambda off: pl.BlockSpec((c, gw), lambda b, g, n: (b * nc + n, off * ng + g))
    cw = lambda off: pl.BlockSpec((taps, gw), lambda b, g, n: (0, off * ng + g))
    kern = functools.partial(_dn_kernel, group=group, heads=heads)
    return pl.pallas_call(
        kern,
        grid=(batch, ng, nc),
        in_specs=[blk(0), blk(1), blk(2), cw(0), cw(1), cw(2),
                  pl.BlockSpec((c, gw), lambda b, g, n: (b * nc + n, g)),
                  pl.BlockSpec((c, 128), lambda b, g, n: (b * nc + n, 0)),
                  pl.BlockSpec((128, c), lambda b, g, n: (0, b * nc + n)),
                  pl.BlockSpec((1, HEAD_DIM), lambda b, g, n: (0, 0))],
        out_specs=pl.BlockSpec((c, gw), lambda b, g, n: (b * nc + n, g)),
        out_shape=jax.ShapeDtypeStruct((T, heads * HEAD_DIM), MXU_DTYPE),
        scratch_shapes=[pltpu.VMEM((group, HEAD_DIM, HEAD_DIM), F32),
                        pltpu.VMEM((3, 8, gw), F32)],
        compiler_params=_cparams(("parallel", "parallel", "arbitrary")),
        name="delta_rule",
    )(qkv, qkv, qkv, conv_w, conv_w, conv_w, gate, g_col, g_row, norm_w.reshape(1, HEAD_DIM))


def _rope_tables(seq):
    pos = jnp.arange(seq, dtype=F32)
    inv_freq = 1.0 / (ROPE_BASE ** jnp.linspace(0.0, 1.0, HEAD_DIM // 2, dtype=F32))
    ang = pos[:, None] * inv_freq[None, :]
    cos, sin = jnp.cos(ang), jnp.sin(ang)
    return jnp.repeat(cos, 2, axis=1), jnp.stack([-sin, sin], axis=-1).reshape(seq, HEAD_DIM)


def _mixer_retention_swa(x, w_in, gn_w, w_out, lnw, lnb, alpha, batch, seq):
    D = x.shape[1]
    width = D // 2
    heads = width // HEAD_DIM
    proj_ret = _proj(x, w_in, 0, 4 * width, F32)
    proj_att = _proj(x, w_in, 4 * width, 3 * width, MXU_DTYPE)
    cos2, sin2 = _rope_tables(seq)
    log_g = jnp.log1p(-jnp.exp2(-5.0 - jnp.arange(heads, dtype=F32)))
    log_g = jnp.broadcast_to(log_g[:, None, None], (heads, 1, HEAD_DIM))
    y_ret = _retention(proj_ret, cos2, sin2, log_g, gn_w, batch, seq, heads)
    y_att = _swa(proj_att, batch, seq, heads)
    return _proj_ln([y_ret, y_att], w_out, x, lnw, lnb, alpha)


def _mixer_gated_deltanet(x, w_in, conv_w, a_log, dt_bias, norm_w, w_out, lnw, lnb, alpha, batch, seq):
    D = x.shape[1]
    heads = D // HEAD_DIM
    w_main = w_in.astype(MXU_DTYPE)
    qkv = _proj(x, w_main, 0, 3 * D, F32)
    gate = _proj(x, w_main, 3 * D, D, F32)
    g_col, g_row = _dn_gates(x, w_in[:, 4 * D:], a_log, dt_bias, heads)
    o = _delta_rule(qkv, conv_w, gate, g_col, g_row, norm_w, batch, seq, heads)
    return _proj_ln([o], w_out, x, lnw, lnb, alpha)


def kernel(x, ffn_w_gate, ffn_w_up, ffn_w_down, ln_w, ln_b, ab_w_in, ab_gn_w, ab_w_out,
           dn_w_in, dn_conv_w, dn_a_log, dn_dt_bias, dn_norm_w, dn_w_out):
    batch, seq, D = x.shape
    depth = ffn_w_gate.shape[0]
    alpha = (2.0 * depth) ** 0.25
    h = x.reshape(batch * seq, D)
    wg, wu, wd = (w.astype(MXU_DTYPE) for w in (ffn_w_gate, ffn_w_up, ffn_w_down))
    for l in range(depth):
        h = _ffn_ln(h, wg, wu, wd, l, 0, ln_w[l, 0], ln_b[l, 0], alpha)
        i = l // 2
        if l % 2 == 0:
            h = _mixer_retention_swa(h, ab_w_in[i].astype(MXU_DTYPE), ab_gn_w[i], ab_w_out[i].astype(MXU_DTYPE),
                                     ln_w[l, 1], ln_b[l, 1], alpha, batch, seq)
        else:
            h = _mixer_gated_deltanet(h, dn_w_in[i], dn_conv_w[i], dn_a_log[i], dn_dt_bias[i],
                                      dn_norm_w[i], dn_w_out[i].astype(MXU_DTYPE), ln_w[l, 1], ln_b[l, 1], alpha,
                                      batch, seq)
        h = _ffn_ln(h, wg, wu, wd, l, 1, ln_w[l, 2], ln_b[l, 2], alpha)
    return h.reshape(batch, seq, D)
```

```python
import functools
import math

import jax
import jax.numpy as jnp
from jax import lax
from jax.experimental import pallas as pl
from jax.experimental.pallas import tpu as pltpu

F32 = jnp.float32
MXU_DTYPE = jnp.bfloat16
HEAD_DIM = 128
LN_EPS = 1e-5
NORM_EPS = 1e-6
ROPE_BASE = 10000.0
SWA_PATTERNS = ((128, 1), (512, 4), (2048, 16))
SWA_FAR = max(w for w, _ in SWA_PATTERNS)
SWA_ROWS = 128
DN_CHUNK = 128
LN_ROWS = 128
DN_GROUP = 8
RET_CHUNK = 128
MASKED = -1e30
V7X_VMEM_LIMIT_BYTES = 56 * 1024 * 1024


def _cparams(sem):
    return pltpu.CompilerParams(dimension_semantics=sem, vmem_limit_bytes=V7X_VMEM_LIMIT_BYTES)


def _mm(a, b):
    return jnp.dot(a.astype(MXU_DTYPE), b.astype(MXU_DTYPE), preferred_element_type=F32)


def _mm_nt(a, b):
    return lax.dot_general(a.astype(MXU_DTYPE), b.astype(MXU_DTYPE), (((1,), (1,)), ((), ())),
                           preferred_element_type=F32)


def _mm_tn(a, b):
    return lax.dot_general(a.astype(MXU_DTYPE), b.astype(MXU_DTYPE), (((0,), (0,)), ((), ())),
                           preferred_element_type=F32)


def _silu(x):
    return x * jax.nn.sigmoid(x)


def _layer_norm(y, w, b):
    mu = jnp.mean(y, axis=-1, keepdims=True)
    yc = y - mu
    var = jnp.mean(yc * yc, axis=-1, keepdims=True)
    return yc * lax.rsqrt(var + LN_EPS) * w + b


def _tile(n, pref, unit):
    if n <= pref:
        return n
    t = (pref // unit) * unit
    while t > unit and n % t:
        t -= unit
    assert n % t == 0, (n, pref, unit)
    return t


def _residual_ln_rows(x_ref, o_ref, lnw_ref, lnb_ref, alpha, rows):
    tm, d = o_ref.shape
    slabs = [slice(c * 128, (c + 1) * 128) for c in range(d // 128)]

    def body(r, carry):
        sl = pl.ds(pl.multiple_of(r * rows, rows), rows)
        s1 = jnp.zeros((rows, 128), F32)
        for cs in slabs:
            y = alpha * x_ref[sl, cs] + o_ref[sl, cs]
            o_ref[sl, cs] = y
            s1 = s1 + y
        mu = jnp.sum(s1, axis=-1, keepdims=True) * (1.0 / d)
        s2 = jnp.zeros((rows, 128), F32)
        for cs in slabs:
            yc = o_ref[sl, cs] - mu
            s2 = s2 + yc * yc
        rstd = lax.rsqrt(jnp.sum(s2, axis=-1, keepdims=True) * (1.0 / d) + LN_EPS)
        for cs in slabs:
            o_ref[sl, cs] = (o_ref[sl, cs] - mu) * rstd * lnw_ref[:, cs] + lnb_ref[:, cs]
        return carry

    lax.fori_loop(0, tm // rows, body, 0)


def _ffn_kernel(x_ref, wg_ref, wu_ref, wd_ref, lnw_ref, lnb_ref, o_ref, xb_ref, *, alpha, n_split, rows):
    j = pl.program_id(1)

    @pl.when(j == 0)
    def _():
        xb_ref[...] = x_ref[...].astype(xb_ref.dtype)
        o_ref[...] = jnp.zeros_like(o_ref)

    xb = xb_ref[...]
    g = jnp.dot(xb, wg_ref[...], preferred_element_type=F32)
    u = jnp.dot(xb, wu_ref[...], preferred_element_type=F32)
    h = (_silu(g) * u * 0.5).astype(xb_ref.dtype)
    dn = o_ref.shape[1] // n_split
    for n in range(n_split):
        cols = slice(n * dn, (n + 1) * dn)
        o_ref[:, cols] += jnp.dot(h, wd_ref[:, cols], preferred_element_type=F32)

    @pl.when(j == pl.num_programs(1) - 1)
    def _():
        _residual_ln_rows(x_ref, o_ref, lnw_ref, lnb_ref, alpha, rows)


def _ffn_ln(x, wg, wu, wd, lnw, lnb, alpha):
    T, D = x.shape
    F = wg.shape[-1]
    tm = _tile(T, 1024, 8)
    tf = _tile(F, 256, 128)
    kern = functools.partial(_ffn_kernel, alpha=alpha, n_split=max(D // 512, 1), rows=LN_ROWS)
    once = pl.Buffered(1)
    return pl.pallas_call(
        kern,
        grid=(T // tm, F // tf),
        in_specs=[
            pl.BlockSpec((tm, D), lambda i, j: (i, 0), pipeline_mode=once),
            pl.BlockSpec((D, tf), lambda i, j: (0, j)),
            pl.BlockSpec((D, tf), lambda i, j: (0, j)),
            pl.BlockSpec((tf, D), lambda i, j: (j, 0)),
            pl.BlockSpec((1, D), lambda i, j: (0, 0)),
            pl.BlockSpec((1, D), lambda i, j: (0, 0)),
        ],
        out_specs=pl.BlockSpec((tm, D), lambda i, j: (i, 0), pipeline_mode=once),
        out_shape=jax.ShapeDtypeStruct((T, D), F32),
        scratch_shapes=[pltpu.VMEM((tm, D), MXU_DTYPE)],
        compiler_params=_cparams(("parallel", "arbitrary")),
        name="ffn_ln",
    )(x, wg, wu, wd, lnw.reshape(1, D), lnb.reshape(1, D))


def _proj_kernel(*refs, n_cast):
    x_ref, w_ref = refs[:2]
    src_refs = refs[2:2 + n_cast]
    o_ref = refs[2 + n_cast]
    dst_refs = refs[3 + n_cast:]
    o_ref[...] = jnp.dot(x_ref[...].astype(w_ref.dtype), w_ref[...], preferred_element_type=F32).astype(o_ref.dtype)
    for src_ref, dst_ref in zip(src_refs, dst_refs):
        dst_ref[...] = src_ref[...].astype(dst_ref.dtype)


def _proj(x, w, col0, ncols, out_dtype, cast=()):
    T, K = x.shape
    tm = _tile(T, 512, 8)
    tn = _tile(math.gcd(ncols, col0) if col0 else ncols, 2048, 128)
    j0 = col0 // tn
    n_i = T // tm
    n_steps = (ncols // tn) * n_i
    in_specs = [pl.BlockSpec((tm, K), lambda j, i: (i, 0)),
                pl.BlockSpec((K, tn), lambda j, i: (0, j0 + j), pipeline_mode=pl.Buffered(1))]
    out_specs = [pl.BlockSpec((tm, tn), lambda j, i: (i, j))]
    out_shape = [jax.ShapeDtypeStruct((T, ncols), out_dtype)]
    for stack, l, k in cast:
        rows, cols = stack.shape[2:]
        units = rows // 16
        nrb = max(d for d in range(1, min(units, n_steps) + 1) if units % d == 0)
        br = rows // nrb
        blk = lambda j, i, nrb=nrb: jnp.minimum(j * n_i + i, nrb - 1)
        in_specs.append(pl.BlockSpec((None, None, br, cols), lambda j, i, l=l, k=k, blk=blk: (l, k, blk(j, i), 0)))
        out_specs.append(pl.BlockSpec((br, cols), lambda j, i, blk=blk: (blk(j, i), 0)))
        out_shape.append(jax.ShapeDtypeStruct((rows, cols), MXU_DTYPE))
    outs = pl.pallas_call(
        functools.partial(_proj_kernel, n_cast=len(cast)),
        grid=(ncols // tn, n_i),
        in_specs=in_specs,
        out_specs=out_specs,
        out_shape=out_shape,
        compiler_params=_cparams(("arbitrary", "arbitrary")),
        name="in_proj",
    )(x, w, *(stack for stack, _, _ in cast))
    return outs[0] if not cast else outs


def _proj_ln_kernel(*refs, alpha, n_parts, n_split, rows):
    a_refs = refs[:n_parts]
    w_ref, x_ref, lnw_ref, lnb_ref, o_ref = refs[n_parts:]
    dn = o_ref.shape[1] // n_split
    for n in range(n_split):
        cols = slice(n * dn, (n + 1) * dn)
        k0 = 0
        acc = None
        for a_ref in a_refs:
            kp = a_ref.shape[1]
            part = jnp.dot(a_ref[...], w_ref[k0:k0 + kp, cols], preferred_element_type=F32)
            acc = part if acc is None else acc + part
            k0 += kp
        o_ref[:, cols] = acc
    _residual_ln_rows(x_ref, o_ref, lnw_ref, lnb_ref, alpha, rows)


def _proj_ln(parts, w, x, lnw, lnb, alpha):
    T = x.shape[0]
    K, D = w.shape
    assert sum(p.shape[1] for p in parts) == K
    tm = _tile(T, 256, 8)
    kern = functools.partial(_proj_ln_kernel, alpha=alpha, n_parts=len(parts), n_split=max(D // 512, 1),
                             rows=LN_ROWS)
    return pl.pallas_call(
        kern,
        grid=(T // tm,),
        in_specs=[pl.BlockSpec((tm, p.shape[1]), lambda i: (i, 0)) for p in parts] + [
            pl.BlockSpec((K, D), lambda i: (0, 0), pipeline_mode=pl.Buffered(1)),
            pl.BlockSpec((tm, D), lambda i: (i, 0)),
            pl.BlockSpec((1, D), lambda i: (0, 0)),
            pl.BlockSpec((1, D), lambda i: (0, 0)),
        ],
        out_specs=pl.BlockSpec((tm, D), lambda i: (i, 0)),
        out_shape=jax.ShapeDtypeStruct((T, D), F32),
        compiler_params=_cparams(("parallel",)),
        name="out_proj_ln",
    )(*parts, w, x, lnw.reshape(1, D), lnb.reshape(1, D))


def _retention_kernel(q_ref, k_ref, v_ref, g_ref, cos_ref, sin_ref, lg_ref, gnw_ref, o_ref, st_ref, *, n_chunks):
    c = RET_CHUNK

    @pl.when(pl.program_id(2) == 0)
    def _():
        st_ref[...] = jnp.zeros_like(st_ref)

    lg = lg_ref[0]
    ii = lax.broadcasted_iota(jnp.int32, (c, c), 0)
    jj = lax.broadcasted_iota(jnp.int32, (c, c), 1)
    rel = (ii - jj).astype(F32)
    dmask = jnp.where(rel >= 0, jnp.exp(lg * jnp.maximum(rel, 0.0)), 0.0)
    row = ii.astype(F32)
    zeta = jnp.exp(lg * (c - 1 - row))
    xi = jnp.exp(lg * (row + 1.0))
    chunk_decay = jnp.exp(lg * c)
    even_lane = (jj & 1) == 0

    def rotate(t, cs, sn):
        partner = jnp.where(even_lane, pltpu.roll(t, HEAD_DIM - 1, 1), pltpu.roll(t, 1, 1))
        return t * cs + partner * sn

    chunks = [slice(n * c, (n + 1) * c) for n in range(n_chunks)]
    qr = [rotate(q_ref[sl, :], cos_ref[sl, :], sin_ref[sl, :]) for sl in chunks]
    kr = [rotate(k_ref[sl, :], cos_ref[sl, :], sin_ref[sl, :]) * (HEAD_DIM ** -0.5) for sl in chunks]
    intra = [_mm(_mm_nt(qr[n], kr[n]) * dmask, v_ref[sl, :]) for n, sl in enumerate(chunks)]
    kv = [_mm_tn(kr[n] * zeta, v_ref[sl, :]) for n, sl in enumerate(chunks)]
    st = st_ref[...]
    for n, sl in enumerate(chunks):
        y = intra[n] + _mm(qr[n], st) * xi
        st = st * chunk_decay + kv[n]
        mu = jnp.mean(y, axis=-1, keepdims=True)
        yc = y - mu
        var = jnp.mean(yc * yc, axis=-1, keepdims=True)
        yn = yc * lax.rsqrt(var + LN_EPS) * gnw_ref[...]
        o_ref[sl, :] = (yn * _silu(g_ref[sl, :])).astype(o_ref.dtype)
    st_ref[...] = st


def _retention(proj, cos2, sin2, log_g, gn_w, batch, seq, heads):
    T = batch * seq
    ts = _tile(seq, 1024, RET_CHUNK)
    ns = seq // ts
    blk = lambda off: pl.BlockSpec((ts, HEAD_DIM), lambda b, h, s: (b * ns + s, off * heads + h))
    tab = pl.BlockSpec((ts, HEAD_DIM), lambda b, h, s: (s, 0))
    kern = functools.partial(_retention_kernel, n_chunks=ts // RET_CHUNK)
    return pl.pallas_call(
        kern,
        grid=(batch, heads, ns),
        in_specs=[blk(0), blk(1), blk(2), blk(3), tab, tab,
                  pl.BlockSpec((1, 1, HEAD_DIM), lambda b, h, s: (h, 0, 0)),
                  pl.BlockSpec((1, HEAD_DIM), lambda b, h, s: (0, h))],
        out_specs=pl.BlockSpec((ts, HEAD_DIM), lambda b, h, s: (b * ns + s, h)),
        out_shape=jax.ShapeDtypeStruct((T, heads * HEAD_DIM), MXU_DTYPE),
        scratch_shapes=[pltpu.VMEM((HEAD_DIM, HEAD_DIM), F32)],
        compiler_params=_cparams(("parallel", "parallel", "arbitrary")),
        name="retention",
    )(proj, proj, proj, proj, cos2, sin2, log_g, gn_w.reshape(1, heads * HEAD_DIM))


def _swa_kernel(q_ref, k_ref, v_ref, bias_ref, o_ref, *, tq, win):
    qi = pl.program_id(2)
    start = pl.multiple_of(jnp.maximum(qi * tq - SWA_FAR, 0), tq)
    kw = k_ref[pl.ds(start, win), :]
    vw = v_ref[pl.ds(start, win), :]
    rows = [slice(r * SWA_ROWS, (r + 1) * SWA_ROWS) for r in range(tq // SWA_ROWS)]
    s = [_mm_nt(q_ref[r, :], kw) * (HEAD_DIM ** -0.5) + bias_ref[0, r, :] for r in rows]
    p = [jnp.exp(si - jnp.max(si, axis=-1, keepdims=True)) for si in s]
    l = [jnp.sum(pi, axis=-1, keepdims=True) for pi in p]
    for r, pi, li in zip(rows, p, l):
        o_ref[r, :] = (_mm(pi, vw) / li).astype(o_ref.dtype)


def _swa_bias(tq, win):
    nvar = SWA_FAR // tq + 1
    v = lax.broadcasted_iota(jnp.int32, (nvar, tq, win), 0)
    i = lax.broadcasted_iota(jnp.int32, (nvar, tq, win), 1)
    c = lax.broadcasted_iota(jnp.int32, (nvar, tq, win), 2)
    dist = SWA_FAR - v * tq + i - c
    mult = jnp.zeros((nvar, tq, win), F32)
    for window, dilation in SWA_PATTERNS:
        mult += ((dist >= 0) & (dist <= window) & (dist % dilation == 0)).astype(F32)
    return jnp.where(mult > 0, jnp.log(jnp.maximum(mult, 1.0)), MASKED)


def _swa(proj, batch, seq, heads):
    T = batch * seq
    tq = _tile(seq, 512, 128)
    assert SWA_FAR % tq == 0 and seq >= SWA_FAR + tq
    win = SWA_FAR + tq
    nq = seq // tq
    nfar = SWA_FAR // tq
    kern = functools.partial(_swa_kernel, tq=tq, win=win)
    return pl.pallas_call(
        kern,
        grid=(batch, heads, nq),
        in_specs=[
            pl.BlockSpec((tq, HEAD_DIM), lambda b, h, i: (b * nq + i, h)),
            pl.BlockSpec((seq, HEAD_DIM), lambda b, h, i: (b, heads + h)),
            pl.BlockSpec((seq, HEAD_DIM), lambda b, h, i: (b, 2 * heads + h)),
            pl.BlockSpec((1, tq, win), lambda b, h, i: (jnp.maximum(nfar - i, 0), 0, 0)),
        ],
        out_specs=pl.BlockSpec((tq, HEAD_DIM), lambda b, h, i: (b * nq + i, h)),
        out_shape=jax.ShapeDtypeStruct((T, heads * HEAD_DIM), MXU_DTYPE),
        compiler_params=_cparams(("parallel", "parallel", "arbitrary")),
        name="dilated_swa",
    )(proj, proj, proj, _swa_bias(tq, win))


def _softplus(x):
    return jnp.maximum(x, 0.0) + jnp.log1p(jnp.exp(-jnp.abs(x)))


def _dn_gates_kernel(x_ref, w_ref, pc_ref, col_ref, row_ref, *, heads):
    c = DN_CHUNK
    xb = x_ref[...].astype(MXU_DTYPE)
    col = jnp.dot(xb, w_ref[...], preferred_element_type=F32)
    ii = lax.broadcasted_iota(jnp.int32, (c, c), 0)
    jj = lax.broadcasted_iota(jnp.int32, (c, c), 1)
    incl = (ii >= jj).astype(F32)
    g_col = -jnp.exp(pc_ref[0:1, :]) * _softplus(col + pc_ref[1:2, :])
    gc_col = jnp.dot(incl, g_col, preferred_element_type=F32, precision=lax.Precision.HIGHEST)
    lane = lax.broadcasted_iota(jnp.int32, col.shape, 1)
    out = jnp.where(lane < heads, gc_col, jax.nn.sigmoid(col))
    col_ref[...] = out
    row_ref[...] = out.T


def _dn_gates(x, w_small, a_log, dt_bias, heads):
    T, D = x.shape
    assert 2 * heads <= 128
    pad = 128 - 2 * heads
    w = jnp.pad(w_small, ((0, 0), (0, pad))).astype(MXU_DTYPE)
    zeros = jnp.zeros((128 - heads,), F32)
    params = jnp.stack([jnp.concatenate([a_log.astype(F32), zeros]), jnp.concatenate([dt_bias.astype(F32), zeros])])
    c = DN_CHUNK
    kern = functools.partial(_dn_gates_kernel, heads=heads)
    return pl.pallas_call(
        kern,
        grid=(T // c,),
        in_specs=[
            pl.BlockSpec((c, D), lambda i: (i, 0)),
            pl.BlockSpec((D, 128), lambda i: (0, 0)),
            pl.BlockSpec((2, 128), lambda i: (0, 0)),
        ],
        out_specs=[pl.BlockSpec((c, 128), lambda i: (i, 0)), pl.BlockSpec((128, c), lambda i: (0, i))],
        out_shape=[jax.ShapeDtypeStruct((T, 128), F32), jax.ShapeDtypeStruct((128, T), F32)],
        compiler_params=_cparams(("parallel",)),
        name="dn_gates",
    )(x, w, params)


def _conv_silu(x, prev, w):
    c = x.shape[0]
    taps = w.shape[0]
    xp = jnp.concatenate([prev, x], axis=0)
    acc = x * w[taps - 1:taps, :]
    for i in range(taps - 1):
        off = 8 - (taps - 1) + i
        acc = acc + xp[off:off + c, :] * w[i:i + 1, :]
    return _silu(acc)


def _l2norm(t):
    return t * lax.rsqrt(jnp.sum(t * t, axis=-1, keepdims=True) + NORM_EPS)


def _dn_kernel(q_ref, k_ref, v_ref, cwq_ref, cwk_ref, cwv_ref, gate_ref, col_ref, row_ref, nw_ref, o_ref,
               st_ref, tail_ref, *, group, heads):
    c = DN_CHUNK
    hs = range(group)

    @pl.when(pl.program_id(2) == 0)
    def _():
        st_ref[...] = jnp.zeros_like(st_ref)
        tail_ref[...] = jnp.zeros_like(tail_ref)

    ii = lax.broadcasted_iota(jnp.int32, (c, c), 0)
    jj = lax.broadcasted_iota(jnp.int32, (c, c), 1)
    eye = (ii == jj).astype(F32)
    n_levels = int(math.log2(c))
    level = [(ii >> 1) == (jj >> 1)] + [
        ((ii >> (ls + 1)) == (jj >> (ls + 1))) & (((ii >> ls) & 1) == 1) & (((jj >> ls) & 1) == 0)
        for ls in range(1, n_levels)]
    col = col_ref[...]
    lane = lax.broadcasted_iota(jnp.int32, col.shape, 1)
    cols = [slice(h * HEAD_DIM, (h + 1) * HEAD_DIM) for h in hs]
    head = [pl.program_id(1) * group + h for h in hs]
    q = [_l2norm(_conv_silu(q_ref[:, cols[h]], tail_ref[0, :, cols[h]], cwq_ref[:, cols[h]])) * (HEAD_DIM ** -0.5)
         for h in hs]
    k = [_l2norm(_conv_silu(k_ref[:, cols[h]], tail_ref[1, :, cols[h]], cwk_ref[:, cols[h]])) for h in hs]
    v = [_conv_silu(v_ref[:, cols[h]], tail_ref[2, :, cols[h]], cwv_ref[:, cols[h]]) for h in hs]
    for i, ref in enumerate((q_ref, k_ref, v_ref)):
        tail_ref[i] = ref[c - 8:c, :]
    gc_i =[jnp.sum(jnp.where(lane == head[h], col, 0.0), axis=-1, keepdims=True) for h in hs]
    beta = [jnp.sum(jnp.where(lane == head[h] + heads, col, 0.0), axis=-1, keepdims=True) for h in hs]
    gc_j = [row_ref[pl.ds(head[h], 1), :] for h in hs]
    gc_last = [gc_i[h][c - 1:c, :] for h in hs]
    decay = [jnp.exp(jnp.where(ii >= jj, gc_i[h] - gc_j[h], MASKED)) for h in hs]
    e_g = [jnp.exp(gc_i[h]) for h in hs]
    kb = [k[h] * beta[h] for h in hs]
    kq = [_mm_nt(jnp.concatenate([kb[h], q[h]], axis=0), k[h]) for h in hs]
    a = [jnp.where(ii > jj, kq[h][:c] * decay[h], 0.0) for h in hs]
    qk = [kq[h][c:] * decay[h] for h in hs]
    w0 = [jnp.where(level[0], a[h], 0.0) for h in hs]
    t = [eye - w0[h] for h in hs]
    z = [a[h] - _mm(a[h], w0[h]) for h in hs]
    for ls in range(1, n_levels):
        wl = [jnp.where(level[ls], z[h], 0.0) for h in hs]
        if ls < n_levels - 1:
            tz = [_mm(jnp.concatenate([t[h], z[h]], axis=0), wl[h]) for h in hs]
            t = [t[h] - tz[h][:c] for h in hs]
            z = [z[h] - tz[h][c:] for h in hs]
        else:
            t = [t[h] - _mm(t[h], wl[h]) for h in hs]
    uw = [_mm(t[h], jnp.concatenate([v[h] * beta[h], kb[h] * e_g[h]], axis=1)) for h in hs]
    st = [st_ref[h] for h in hs]
    ws_qs = [_mm(jnp.concatenate([uw[h][:, HEAD_DIM:], q[h] * e_g[h]], axis=0), st[h]) for h in hs]
    v_new = [uw[h][:, :HEAD_DIM] - ws_qs[h][:c] for h in hs]
    o = [ws_qs[h][c:] + _mm(qk[h], v_new[h]) for h in hs]
    for h in hs:
        st_ref[h] = st[h] * jnp.exp(gc_last[h]) + _mm_tn(k[h] * jnp.exp(gc_last[h] - gc_i[h]), v_new[h])
    for h in hs:
        on = o[h] * lax.rsqrt(jnp.mean(o[h] * o[h], axis=-1, keepdims=True) + NORM_EPS) * nw_ref[...]
        o_ref[:, cols[h]] = (on * _silu(gate_ref[:, cols[h]])).astype(o_ref.dtype)


def _delta_rule(qkv, conv_w, gate, g_col, g_row, norm_w, batch, seq, heads):
    T = batch * seq
    c = DN_CHUNK
    group = _tile(heads, DN_GROUP, 1)
    ng = heads // group
    nc = seq // c
    gw = group * HEAD_DIM
    taps = conv_w.shape[0]
    blk = lambda off: pl.BlockSpec((c, gw), lambda b, g, n: (b * nc + n, off * ng + g))
    cw = lambda off: pl.BlockSpec((taps, gw), lambda b, g, n: (0, off * ng + g))
    kern = functools.partial(_dn_kernel, group=group, heads=heads)
    return pl.pallas_call(
        kern,
        grid=(batch, ng, nc),
        in_specs=[blk(0), blk(1), blk(2), cw(0), cw(1), cw(2),
                  pl.BlockSpec((c, gw), lambda b, g, n: (b * nc + n, g)),
                  pl.BlockSpec((c, 128), lambda b, g, n: (b * nc + n, 0)),
                  pl.BlockSpec((128, c), lambda b, g, n: (0, b * nc + n)),
                  pl.BlockSpec((1, HEAD_DIM), lambda b, g, n: (0, 0))],
        out_specs=pl.BlockSpec((c, gw), lambda b, g, n: (b * nc + n, g)),
        out_shape=jax.ShapeDtypeStruct((T, heads * HEAD_DIM), MXU_DTYPE),
        scratch_shapes=[pltpu.VMEM((group, HEAD_DIM, HEAD_DIM), F32),
                        pltpu.VMEM((3, 8, gw), F32)],
        compiler_params=_cparams(("parallel", "parallel", "arbitrary")),
        name="delta_rule",
    )(qkv, qkv, qkv, conv_w, conv_w, conv_w, gate, g_col, g_row, norm_w.reshape(1, HEAD_DIM))


def _rope_tables(seq):
    pos = jnp.arange(seq, dtype=F32)
    inv_freq = 1.0 / (ROPE_BASE ** jnp.linspace(0.0, 1.0, HEAD_DIM // 2, dtype=F32))
    ang = pos[:, None] * inv_freq[None, :]
    cos, sin = jnp.cos(ang), jnp.sin(ang)
    return jnp.repeat(cos, 2, axis=1), jnp.stack([-sin, sin], axis=-1).reshape(seq, HEAD_DIM)


def _mixer_retention_swa(x, w_in, gn_w, w_out, lnw, lnb, alpha, batch, seq, ffn_stacks, ffn_at):
    D = x.shape[1]
    width = D // 2
    heads = width // HEAD_DIM
    gate_stack, up_stack, down_stack = ffn_stacks
    proj_ret, wg, wu = _proj(x, w_in, 0, 4 * width, F32, cast=[(gate_stack, *ffn_at), (up_stack, *ffn_at)])
    proj_att, wd = _proj(x, w_in, 4 * width, 3 * width, MXU_DTYPE, cast=[(down_stack, *ffn_at)])
    cos2, sin2 = _rope_tables(seq)
    log_g = jnp.log1p(-jnp.exp2(-5.0 - jnp.arange(heads, dtype=F32)))
    log_g = jnp.broadcast_to(log_g[:, None, None], (heads, 1, HEAD_DIM))
    y_ret = _retention(proj_ret, cos2, sin2, log_g, gn_w, batch, seq, heads)
    y_att = _swa(proj_att, batch, seq, heads)
    return _proj_ln([y_ret, y_att], w_out, x, lnw, lnb, alpha), (wg, wu, wd)


def _mixer_gated_deltanet(x, w_in, conv_w, a_log, dt_bias, norm_w, w_out, lnw, lnb, alpha, batch, seq,
                          ffn_stacks, ffn_at):
    D = x.shape[1]
    heads = D // HEAD_DIM
    gate_stack, up_stack, down_stack = ffn_stacks
    w_main = w_in.astype(MXU_DTYPE)
    qkv, wg, wu = _proj(x, w_main, 0, 3 * D, F32, cast=[(gate_stack, *ffn_at), (up_stack, *ffn_at)])
    gate, wd = _proj(x, w_main, 3 * D, D, F32, cast=[(down_stack, *ffn_at)])
    g_col, g_row = _dn_gates(x, w_in[:, 4 * D:], a_log, dt_bias, heads)
    o = _delta_rule(qkv, conv_w, gate, g_col, g_row, norm_w, batch, seq, heads)
    return _proj_ln([o], w_out, x, lnw, lnb, alpha), (wg, wu, wd)


def kernel(x, ffn_w_gate, ffn_w_up, ffn_w_down, ln_w, ln_b, ab_w_in, ab_gn_w, ab_w_out,
           dn_w_in, dn_conv_w, dn_a_log, dn_dt_bias, dn_norm_w, dn_w_out):
    batch, seq, D = x.shape
    depth = ffn_w_gate.shape[0]
    alpha = (2.0 * depth) ** 0.25
    h = x.reshape(batch * seq, D)
    stacks = (ffn_w_gate, ffn_w_up, ffn_w_down)
    for l in range(depth):
        h = _ffn_ln(h, *(w[l, 0].astype(MXU_DTYPE) for w in stacks), ln_w[l, 0], ln_b[l, 0], alpha)
        i = l // 2
        if l % 2 == 0:
            h, post = _mixer_retention_swa(h, ab_w_in[i].astype(MXU_DTYPE), ab_gn_w[i],
                                           ab_w_out[i].astype(MXU_DTYPE), ln_w[l, 1], ln_b[l, 1], alpha,
                                           batch, seq, stacks, (l, 1))
        else:
            h, post = _mixer_gated_deltanet(h, dn_w_in[i], dn_conv_w[i], dn_a_log[i], dn_dt_bias[i], dn_norm_w[i],
                                            dn_w_out[i].astype(MXU_DTYPE), ln_w[l, 1], ln_b[l, 1], alpha,
                                            batch, seq, stacks, (l, 1))
        h = _ffn_ln(h, *post, ln_w[l, 2], ln_b[l, 2], alpha)
    return h.reshape(batch, seq, D)
```

```python
import functools
import math

import jax
import jax.numpy as jnp
from jax import lax
from jax.experimental import pallas as pl
from jax.experimental.pallas import tpu as pltpu

F32 = jnp.float32
MXU_DTYPE = jnp.bfloat16
HEAD_DIM = 128
LN_EPS = 1e-5
NORM_EPS = 1e-6
ROPE_BASE = 10000.0
SWA_PATTERNS = ((128, 1), (512, 4), (2048, 16))
SWA_FAR = max(w for w, _ in SWA_PATTERNS)
SWA_ROWS = 128
DN_CHUNK = 128
LN_ROWS = 128
DN_GROUP = 8
RET_CHUNK = 128
MASKED = -1e30
V7X_VMEM_LIMIT_BYTES = 56 * 1024 * 1024


def _cparams(sem):
    return pltpu.CompilerParams(dimension_semantics=sem, vmem_limit_bytes=V7X_VMEM_LIMIT_BYTES)


def _mm(a, b):
    return jnp.dot(a.astype(MXU_DTYPE), b.astype(MXU_DTYPE), preferred_element_type=F32)


def _mm_nt(a, b):
    return lax.dot_general(a.astype(MXU_DTYPE), b.astype(MXU_DTYPE), (((1,), (1,)), ((), ())),
                           preferred_element_type=F32)


def _mm_tn(a, b):
    return lax.dot_general(a.astype(MXU_DTYPE), b.astype(MXU_DTYPE), (((0,), (0,)), ((), ())),
                           preferred_element_type=F32)


def _silu(x):
    return x * jax.nn.sigmoid(x)


def _layer_norm(y, w, b):
    mu = jnp.mean(y, axis=-1, keepdims=True)
    yc = y - mu
    var = jnp.mean(yc * yc, axis=-1, keepdims=True)
    return yc * lax.rsqrt(var + LN_EPS) * w + b


def _tile(n, pref, unit):
    if n <= pref:
        return n
    t = (pref // unit) * unit
    while t > unit and n % t:
        t -= unit
    assert n % t == 0, (n, pref, unit)
    return t


def _residual_ln_rows(x_ref, o_ref, lnw_ref, lnb_ref, alpha, rows):
    tm, d = o_ref.shape
    slabs = [slice(c * 128, (c + 1) * 128) for c in range(d // 128)]

    def body(r, carry):
        sl = pl.ds(pl.multiple_of(r * rows, rows), rows)
        s1 = jnp.zeros((rows, 128), F32)
        for cs in slabs:
            y = alpha * x_ref[sl, cs] + o_ref[sl, cs]
            o_ref[sl, cs] = y
            s1 = s1 + y
        mu = jnp.sum(s1, axis=-1, keepdims=True) * (1.0 / d)
        s2 = jnp.zeros((rows, 128), F32)
        for cs in slabs:
            yc = o_ref[sl, cs] - mu
            s2 = s2 + yc * yc
        rstd = lax.rsqrt(jnp.sum(s2, axis=-1, keepdims=True) * (1.0 / d) + LN_EPS)
        for cs in slabs:
            o_ref[sl, cs] = (o_ref[sl, cs] - mu) * rstd * lnw_ref[:, cs] + lnb_ref[:, cs]
        return carry

    lax.fori_loop(0, tm // rows, body, 0)


def _ffn_kernel(x_ref, wg_ref, wu_ref, wd_ref, lnw_ref, lnb_ref, o_ref, xb_ref, *, alpha, n_split, rows):
    j = pl.program_id(1)

    @pl.when(j == 0)
    def _():
        xb_ref[...] = x_ref[...].astype(xb_ref.dtype)
        o_ref[...] = jnp.zeros_like(o_ref)

    xb = xb_ref[...]
    g = jnp.dot(xb, wg_ref[...], preferred_element_type=F32)
    u = jnp.dot(xb, wu_ref[...], preferred_element_type=F32)
    h = (_silu(g) * u * 0.5).astype(xb_ref.dtype)
    dn = o_ref.shape[1] // n_split
    for n in range(n_split):
        cols = slice(n * dn, (n + 1) * dn)
        o_ref[:, cols] += jnp.dot(h, wd_ref[:, cols], preferred_element_type=F32)

    @pl.when(j == pl.num_programs(1) - 1)
    def _():
        _residual_ln_rows(x_ref, o_ref, lnw_ref, lnb_ref, alpha, rows)


def _ffn_ln(x, wg, wu, wd, lnw, lnb, alpha):
    T, D = x.shape
    F = wg.shape[-1]
    tm = _tile(T, 1024, 8)
    tf = _tile(F, 256, 128)
    kern = functools.partial(_ffn_kernel, alpha=alpha, n_split=max(D // 512, 1), rows=LN_ROWS)
    once = pl.Buffered(1)
    return pl.pallas_call(
        kern,
        grid=(T // tm, F // tf),
        in_specs=[
            pl.BlockSpec((tm, D), lambda i, j: (i, 0), pipeline_mode=once),
            pl.BlockSpec((D, tf), lambda i, j: (0, j)),
            pl.BlockSpec((D, tf), lambda i, j: (0, j)),
            pl.BlockSpec((tf, D), lambda i, j: (j, 0)),
            pl.BlockSpec((1, D), lambda i, j: (0, 0)),
            pl.BlockSpec((1, D), lambda i, j: (0, 0)),
        ],
        out_specs=pl.BlockSpec((tm, D), lambda i, j: (i, 0), pipeline_mode=once),
        out_shape=jax.ShapeDtypeStruct((T, D), F32),
        scratch_shapes=[pltpu.VMEM((tm, D), MXU_DTYPE)],
        compiler_params=_cparams(("parallel", "arbitrary")),
        name="ffn_ln",
    )(x, wg, wu, wd, lnw.reshape(1, D), lnb.reshape(1, D))


def _cast_kernel(src_ref, dst_ref):
    dst_ref[...] = src_ref[...].astype(dst_ref.dtype)


def _cast_weight(stack, l, k):
    rows, cols = stack.shape[2:]
    units = rows // 16
    nrb = max(d for d in range(1, units + 1) if units % d == 0 and (rows // d) * cols * 4 >= (2 << 20) or d == 1)
    br = rows // nrb
    return pl.pallas_call(
        _cast_kernel,
        grid=(nrb,),
        in_specs=[pl.BlockSpec((None, None, br, cols), lambda r: (l, k, r, 0))],
        out_specs=pl.BlockSpec((br, cols), lambda r: (r, 0)),
        out_shape=jax.ShapeDtypeStruct((rows, cols), MXU_DTYPE),
        compiler_params=_cparams(("parallel",)),
        name="cast_weight",
    )(stack)


def _proj_kernel(*refs, n_cast):
    x_ref, w_ref = refs[:2]
    src_refs = refs[2:2 + n_cast]
    o_ref = refs[2 + n_cast]
    dst_refs = refs[3 + n_cast:]
    o_ref[...] = jnp.dot(x_ref[...].astype(w_ref.dtype), w_ref[...], preferred_element_type=F32).astype(o_ref.dtype)
    for src_ref, dst_ref in zip(src_refs, dst_refs):
        dst_ref[...] = src_ref[...].astype(dst_ref.dtype)


def _proj(x, w, col0, ncols, out_dtype, cast=()):
    T, K = x.shape
    tm = _tile(T, 512, 8)
    tn = _tile(math.gcd(ncols, col0) if col0 else ncols, 2048, 128)
    j0 = col0 // tn
    n_i = T // tm
    n_steps = (ncols // tn) * n_i
    in_specs = [pl.BlockSpec((tm, K), lambda j, i: (i, 0)),
                pl.BlockSpec((K, tn), lambda j, i: (0, j0 + j), pipeline_mode=pl.Buffered(1))]
    out_specs = [pl.BlockSpec((tm, tn), lambda j, i: (i, j))]
    out_shape = [jax.ShapeDtypeStruct((T, ncols), out_dtype)]
    for stack, l, k in cast:
        rows, cols = stack.shape[2:]
        units = rows // 16
        nrb = max(d for d in range(1, min(units, n_steps) + 1) if units % d == 0)
        br = rows // nrb
        blk = lambda j, i, nrb=nrb: jnp.minimum(j * n_i + i, nrb - 1)
        in_specs.append(pl.BlockSpec((None, None, br, cols), lambda j, i, l=l, k=k, blk=blk: (l, k, blk(j, i), 0)))
        out_specs.append(pl.BlockSpec((br, cols), lambda j, i, blk=blk: (blk(j, i), 0)))
        out_shape.append(jax.ShapeDtypeStruct((rows, cols), MXU_DTYPE))
    outs = pl.pallas_call(
        functools.partial(_proj_kernel, n_cast=len(cast)),
        grid=(ncols // tn, n_i),
        in_specs=in_specs,
        out_specs=out_specs,
        out_shape=out_shape,
        compiler_params=_cparams(("arbitrary", "arbitrary")),
        name="in_proj",
    )(x, w, *(stack for stack, _, _ in cast))
    return outs[0] if not cast else outs


def _proj_ln_kernel(*refs, alpha, n_parts, n_split, rows):
    a_refs = refs[:n_parts]
    w_ref, x_ref, lnw_ref, lnb_ref, o_ref = refs[n_parts:]
    dn = o_ref.shape[1] // n_split
    for n in range(n_split):
        cols = slice(n * dn, (n + 1) * dn)
        k0 = 0
        acc = None
        for a_ref in a_refs:
            kp = a_ref.shape[1]
            part = jnp.dot(a_ref[...], w_ref[k0:k0 + kp, cols], preferred_element_type=F32)
            acc = part if acc is None else acc + part
            k0 += kp
        o_ref[:, cols] = acc
    _residual_ln_rows(x_ref, o_ref, lnw_ref, lnb_ref, alpha, rows)


def _proj_ln(parts, w, x, lnw, lnb, alpha):
    T = x.shape[0]
    K, D = w.shape
    assert sum(p.shape[1] for p in parts) == K
    tm = _tile(T, 256, 8)
    kern = functools.partial(_proj_ln_kernel, alpha=alpha, n_parts=len(parts), n_split=max(D // 512, 1),
                             rows=LN_ROWS)
    return pl.pallas_call(
        kern,
        grid=(T // tm,),
        in_specs=[pl.BlockSpec((tm, p.shape[1]), lambda i: (i, 0)) for p in parts] + [
            pl.BlockSpec((K, D), lambda i: (0, 0), pipeline_mode=pl.Buffered(1)),
            pl.BlockSpec((tm, D), lambda i: (i, 0)),
            pl.BlockSpec((1, D), lambda i: (0, 0)),
            pl.BlockSpec((1, D), lambda i: (0, 0)),
        ],
        out_specs=pl.BlockSpec((tm, D), lambda i: (i, 0)),
        out_shape=jax.ShapeDtypeStruct((T, D), F32),
        compiler_params=_cparams(("parallel",)),
        name="out_proj_ln",
    )(*parts, w, x, lnw.reshape(1, D), lnb.reshape(1, D))


def _retention_kernel(q_ref, k_ref, v_ref, g_ref, cos_ref, sin_ref, lg_ref, gnw_ref, o_ref, st_ref, *, n_chunks):
    c = RET_CHUNK

    @pl.when(pl.program_id(2) == 0)
    def _():
        st_ref[...] = jnp.zeros_like(st_ref)

    lg = lg_ref[0]
    ii = lax.broadcasted_iota(jnp.int32, (c, c), 0)
    jj = lax.broadcasted_iota(jnp.int32, (c, c), 1)
    rel = (ii - jj).astype(F32)
    dmask = jnp.where(rel >= 0, jnp.exp(lg * jnp.maximum(rel, 0.0)), 0.0)
    row = ii.astype(F32)
    zeta = jnp.exp(lg * (c - 1 - row))
    xi = jnp.exp(lg * (row + 1.0))
    chunk_decay = jnp.exp(lg * c)
    even_lane = (jj & 1) == 0

    def rotate(t, cs, sn):
        partner = jnp.where(even_lane, pltpu.roll(t, HEAD_DIM - 1, 1), pltpu.roll(t, 1, 1))
        return t * cs + partner * sn

    chunks = [slice(n * c, (n + 1) * c) for n in range(n_chunks)]
    qr = [rotate(q_ref[sl, :], cos_ref[sl, :], sin_ref[sl, :]) for sl in chunks]
    kr = [rotate(k_ref[sl, :], cos_ref[sl, :], sin_ref[sl, :]) * (HEAD_DIM ** -0.5) for sl in chunks]
    intra = [_mm(_mm_nt(qr[n], kr[n]) * dmask, v_ref[sl, :]) for n, sl in enumerate(chunks)]
    kv = [_mm_tn(kr[n] * zeta, v_ref[sl, :]) for n, sl in enumerate(chunks)]
    st = st_ref[...]
    for n, sl in enumerate(chunks):
        y = intra[n] + _mm(qr[n], st) * xi
        st = st * chunk_decay + kv[n]
        mu = jnp.mean(y, axis=-1, keepdims=True)
        yc = y - mu
        var = jnp.mean(yc * yc, axis=-1, keepdims=True)
        yn = yc * lax.rsqrt(var + LN_EPS) * gnw_ref[...]
        o_ref[sl, :] = (yn * _silu(g_ref[sl, :])).astype(o_ref.dtype)
    st_ref[...] = st


def _retention(proj, cos2, sin2, log_g, gn_w, batch, seq, heads):
    T = batch * seq
    ts = _tile(seq, 1024, RET_CHUNK)
    ns = seq // ts
    blk = lambda off: pl.BlockSpec((ts, HEAD_DIM), lambda b, h, s: (b * ns + s, off * heads + h))
    tab = pl.BlockSpec((ts, HEAD_DIM), lambda b, h, s: (s, 0))
    kern = functools.partial(_retention_kernel, n_chunks=ts // RET_CHUNK)
    return pl.pallas_call(
        kern,
        grid=(batch, heads, ns),
        in_specs=[blk(0), blk(1), blk(2), blk(3), tab, tab,
                  pl.BlockSpec((1, 1, HEAD_DIM), lambda b, h, s: (h, 0, 0)),
                  pl.BlockSpec((1, HEAD_DIM), lambda b, h, s: (0, h))],
        out_specs=pl.BlockSpec((ts, HEAD_DIM), lambda b, h, s: (b * ns + s, h)),
        out_shape=jax.ShapeDtypeStruct((T, heads * HEAD_DIM), MXU_DTYPE),
        scratch_shapes=[pltpu.VMEM((HEAD_DIM, HEAD_DIM), F32)],
        compiler_params=_cparams(("parallel", "parallel", "arbitrary")),
        name="retention",
    )(proj, proj, proj, proj, cos2, sin2, log_g, gn_w.reshape(1, heads * HEAD_DIM))


def _swa_kernel(q_ref, k_ref, v_ref, bias_ref, o_ref, *, tq, win):
    qi = pl.program_id(2)
    start = pl.multiple_of(jnp.maximum(qi * tq - SWA_FAR, 0), tq)
    kw = k_ref[pl.ds(start, win), :]
    vw = v_ref[pl.ds(start, win), :]
    rows = [slice(r * SWA_ROWS, (r + 1) * SWA_ROWS) for r in range(tq // SWA_ROWS)]
    s = [_mm_nt(q_ref[r, :], kw) * (HEAD_DIM ** -0.5) + bias_ref[0, r, :] for r in rows]
    p = [jnp.exp(si - jnp.max(si, axis=-1, keepdims=True)) for si in s]
    l = [jnp.sum(pi, axis=-1, keepdims=True) for pi in p]
    for r, pi, li in zip(rows, p, l):
        o_ref[r, :] = (_mm(pi, vw) / li).astype(o_ref.dtype)


def _swa_bias(tq, win):
    nvar = SWA_FAR // tq + 1
    v = lax.broadcasted_iota(jnp.int32, (nvar, tq, win), 0)
    i = lax.broadcasted_iota(jnp.int32, (nvar, tq, win), 1)
    c = lax.broadcasted_iota(jnp.int32, (nvar, tq, win), 2)
    dist = SWA_FAR - v * tq + i - c
    mult = jnp.zeros((nvar, tq, win), F32)
    for window, dilation in SWA_PATTERNS:
        mult += ((dist >= 0) & (dist <= window) & (dist % dilation == 0)).astype(F32)
    return jnp.where(mult > 0, jnp.log(jnp.maximum(mult, 1.0)), MASKED)


def _swa(proj, batch, seq, heads):
    T = batch * seq
    tq = _tile(seq, 512, 128)
    assert SWA_FAR % tq == 0 and seq >= SWA_FAR + tq
    win = SWA_FAR + tq
    nq = seq // tq
    nfar = SWA_FAR // tq
    kern = functools.partial(_swa_kernel, tq=tq, win=win)
    return pl.pallas_call(
        kern,
        grid=(batch, heads, nq),
        in_specs=[
            pl.BlockSpec((tq, HEAD_DIM), lambda b, h, i: (b * nq + i, h)),
            pl.BlockSpec((seq, HEAD_DIM), lambda b, h, i: (b, heads + h)),
            pl.BlockSpec((seq, HEAD_DIM), lambda b, h, i: (b, 2 * heads + h)),
            pl.BlockSpec((1, tq, win), lambda b, h, i: (jnp.maximum(nfar - i, 0), 0, 0)),
        ],
        out_specs=pl.BlockSpec((tq, HEAD_DIM), lambda b, h, i: (b * nq + i, h)),
        out_shape=jax.ShapeDtypeStruct((T, heads * HEAD_DIM), MXU_DTYPE),
        compiler_params=_cparams(("parallel", "parallel", "arbitrary")),
        name="dilated_swa",
    )(proj, proj, proj, _swa_bias(tq, win))


def _softplus(x):
    return jnp.maximum(x, 0.0) + jnp.log1p(jnp.exp(-jnp.abs(x)))


def _dn_gates_kernel(x_ref, w_ref, pc_ref, col_ref, row_ref, *, heads):
    c = DN_CHUNK
    xb = x_ref[...].astype(MXU_DTYPE)
    col = jnp.dot(xb, w_ref[...], preferred_element_type=F32)
    ii = lax.broadcasted_iota(jnp.int32, (c, c), 0)
    jj = lax.broadcasted_iota(jnp.int32, (c, c), 1)
    incl = (ii >= jj).astype(F32)
    g_col = -jnp.exp(pc_ref[0:1, :]) * _softplus(col + pc_ref[1:2, :])
    gc_col = jnp.dot(incl, g_col, preferred_element_type=F32, precision=lax.Precision.HIGHEST)
    lane = lax.broadcasted_iota(jnp.int32, col.shape, 1)
    out = jnp.where(lane < heads, gc_col, jax.nn.sigmoid(col))
    col_ref[...] = out
    row_ref[...] = out.T


def _dn_gates(x, w_small, a_log, dt_bias, heads):
    T, D = x.shape
    assert 2 * heads <= 128
    pad = 128 - 2 * heads
    w = jnp.pad(w_small, ((0, 0), (0, pad))).astype(MXU_DTYPE)
    zeros = jnp.zeros((128 - heads,), F32)
    params = jnp.stack([jnp.concatenate([a_log.astype(F32), zeros]), jnp.concatenate([dt_bias.astype(F32), zeros])])
    c = DN_CHUNK
    kern = functools.partial(_dn_gates_kernel, heads=heads)
    return pl.pallas_call(
        kern,
        grid=(T // c,),
        in_specs=[
            pl.BlockSpec((c, D), lambda i: (i, 0)),
            pl.BlockSpec((D, 128), lambda i: (0, 0)),
            pl.BlockSpec((2, 128), lambda i: (0, 0)),
        ],
        out_specs=[pl.BlockSpec((c, 128), lambda i: (i, 0)), pl.BlockSpec((128, c), lambda i: (0, i))],
        out_shape=[jax.ShapeDtypeStruct((T, 128), F32), jax.ShapeDtypeStruct((128, T), F32)],
        compiler_params=_cparams(("parallel",)),
        name="dn_gates",
    )(x, w, params)


def _conv_silu(x, prev, w):
    c = x.shape[0]
    taps = w.shape[0]
    xp = jnp.concatenate([prev, x], axis=0)
    acc = x * w[taps - 1:taps, :]
    for i in range(taps - 1):
        off = 8 - (taps - 1) + i
        acc = acc + xp[off:off + c, :] * w[i:i + 1, :]
    return _silu(acc)


def _l2norm(t):
    return t * lax.rsqrt(jnp.sum(t * t, axis=-1, keepdims=True) + NORM_EPS)


def _dn_kernel(q_ref, k_ref, v_ref, cwq_ref, cwk_ref, cwv_ref, gate_ref, col_ref, row_ref, nw_ref, o_ref,
               st_ref, tail_ref, *, group, heads):
    c = DN_CHUNK
    hs = range(group)

    @pl.when(pl.program_id(2) == 0)
    def _():
        st_ref[...] = jnp.zeros_like(st_ref)
        tail_ref[...] = jnp.zeros_like(tail_ref)

    ii = lax.broadcasted_iota(jnp.int32, (c, c), 0)
    jj = lax.broadcasted_iota(jnp.int32, (c, c), 1)
    eye = (ii == jj).astype(F32)
    n_levels = int(math.log2(c))
    level = [(ii >> 1) == (jj >> 1)] + [
        ((ii >> (ls + 1)) == (jj >> (ls + 1))) & (((ii >> ls) & 1) == 1) & (((jj >> ls) & 1) == 0)
        for ls in range(1, n_levels)]
    col = col_ref[...]
    lane = lax.broadcasted_iota(jnp.int32, col.shape, 1)
    cols = [slice(h * HEAD_DIM, (h + 1) * HEAD_DIM) for h in hs]
    head = [pl.program_id(1) * group + h for h in hs]
    q = [_l2norm(_conv_silu(q_ref[:, cols[h]], tail_ref[0, :, cols[h]], cwq_ref[:, cols[h]])) * (HEAD_DIM ** -0.5)
         for h in hs]
    k = [_l2norm(_conv_silu(k_ref[:, cols[h]], tail_ref[1, :, cols[h]], cwk_ref[:, cols[h]])) for h in hs]
    v = [_conv_silu(v_ref[:, cols[h]], tail_ref[2, :, cols[h]], cwv_ref[:, cols[h]]) for h in hs]
    for i, ref in enumerate((q_ref, k_ref, v_ref)):
        tail_ref[i] = ref[c - 8:c, :]
    gc_i =[jnp.sum(jnp.where(lane == head[h], col, 0.0), axis=-1, keepdims=True) for h in hs]
    beta = [jnp.sum(jnp.where(lane == head[h] + heads, col, 0.0), axis=-1, keepdims=True) for h in hs]
    gc_j = [row_ref[pl.ds(head[h], 1), :] for h in hs]
    gc_last = [gc_i[h][c - 1:c, :] for h in hs]
    decay = [jnp.exp(jnp.where(ii >= jj, gc_i[h] - gc_j[h], MASKED)) for h in hs]
    e_g = [jnp.exp(gc_i[h]) for h in hs]
    kb = [k[h] * beta[h] for h in hs]
    kq = [_mm_nt(jnp.concatenate([kb[h], q[h]], axis=0), k[h]) for h in hs]
    a = [jnp.where(ii > jj, kq[h][:c] * decay[h], 0.0) for h in hs]
    qk = [kq[h][c:] * decay[h] for h in hs]
    w0 = [jnp.where(level[0], a[h], 0.0) for h in hs]
    t = [eye - w0[h] for h in hs]
    z = [a[h] - _mm(a[h], w0[h]) for h in hs]
    for ls in range(1, n_levels):
        wl = [jnp.where(level[ls], z[h], 0.0) for h in hs]
        if ls < n_levels - 1:
            tz = [_mm(jnp.concatenate([t[h], z[h]], axis=0), wl[h]) for h in hs]
            t = [t[h] - tz[h][:c] for h in hs]
            z = [z[h] - tz[h][c:] for h in hs]
        else:
            t = [t[h] - _mm(t[h], wl[h]) for h in hs]
    uw = [_mm(t[h], jnp.concatenate([v[h] * beta[h], kb[h] * e_g[h]], axis=1)) for h in hs]
    st = [st_ref[h] for h in hs]
    ws_qs = [_mm(jnp.concatenate([uw[h][:, HEAD_DIM:], q[h] * e_g[h]], axis=0), st[h]) for h in hs]
    v_new = [uw[h][:, :HEAD_DIM] - ws_qs[h][:c] for h in hs]
    o = [ws_qs[h][c:] + _mm(qk[h], v_new[h]) for h in hs]
    for h in hs:
        st_ref[h] = st[h] * jnp.exp(gc_last[h]) + _mm_tn(k[h] * jnp.exp(gc_last[h] - gc_i[h]), v_new[h])
    for h in hs:
        on = o[h] * lax.rsqrt(jnp.mean(o[h] * o[h], axis=-1, keepdims=True) + NORM_EPS) * nw_ref[...]
        o_ref[:, cols[h]] = (on * _silu(gate_ref[:, cols[h]])).astype(o_ref.dtype)


def _delta_rule(qkv, conv_w, gate, g_col, g_row, norm_w, batch, seq, heads):
    T = batch * seq
    c = DN_CHUNK
    group = _tile(heads, DN_GROUP, 1)
    ng = heads // group
    nc = seq // c
    gw = group * HEAD_DIM
    taps = conv_w.shape[0]
    blk = lambda off: pl.BlockSpec((c, gw), lambda b, g, n: (b * nc + n, off * ng + g))
    cw = lambda off: pl.BlockSpec((taps, gw), lambda b, g, n: (0, off * ng + g))
    kern = functools.partial(_dn_kernel, group=group, heads=heads)
    return pl.pallas_call(
        kern,
        grid=(batch, ng, nc),
        in_specs=[blk(0), blk(1), blk(2), cw(0), cw(1), cw(2),
                  pl.BlockSpec((c, gw), lambda b, g, n: (b * nc + n, g)),
                  pl.BlockSpec((c, 128), lambda b, g, n: (b * nc + n, 0)),
                  pl.BlockSpec((128, c), lambda b, g, n: (0, b * nc + n)),
                  pl.BlockSpec((1, HEAD_DIM), lambda b, g, n: (0, 0))],
        out_specs=pl.BlockSpec((c, gw), lambda b, g, n: (b * nc + n, g)),
        out_shape=jax.ShapeDtypeStruct((T, heads * HEAD_DIM), MXU_DTYPE),
        scratch_shapes=[pltpu.VMEM((group, HEAD_DIM, HEAD_DIM), F32),
                        pltpu.VMEM((3, 8, gw), F32)],
        compiler_params=_cparams(("parallel", "parallel", "arbitrary")),
        name="delta_rule",
    )(qkv, qkv, qkv, conv_w, conv_w, conv_w, gate, g_col, g_row, norm_w.reshape(1, HEAD_DIM))


def _rope_tables(seq):
    pos = jnp.arange(seq, dtype=F32)
    inv_freq = 1.0 / (ROPE_BASE ** jnp.linspace(0.0, 1.0, HEAD_DIM // 2, dtype=F32))
    ang = pos[:, None] * inv_freq[None, :]
    cos, sin = jnp.cos(ang), jnp.sin(ang)
    return jnp.repeat(cos, 2, axis=1), jnp.stack([-sin, sin], axis=-1).reshape(seq, HEAD_DIM)


def _mixer_retention_swa(x, w_in, gn_w, w_out, lnw, lnb, alpha, batch, seq, ffn_stacks, ffn_at):
    D = x.shape[1]
    width = D // 2
    heads = width // HEAD_DIM
    gate_stack, up_stack, down_stack = ffn_stacks
    proj_ret, wg, wu = _proj(x, w_in, 0, 4 * width, F32, cast=[(gate_stack, *ffn_at), (up_stack, *ffn_at)])
    proj_att, wd = _proj(x, w_in, 4 * width, 3 * width, MXU_DTYPE, cast=[(down_stack, *ffn_at)])
    cos2, sin2 = _rope_tables(seq)
    log_g = jnp.log1p(-jnp.exp2(-5.0 - jnp.arange(heads, dtype=F32)))
    log_g = jnp.broadcast_to(log_g[:, None, None], (heads, 1, HEAD_DIM))
    y_ret = _retention(proj_ret, cos2, sin2, log_g, gn_w, batch, seq, heads)
    y_att = _swa(proj_att, batch, seq, heads)
    return _proj_ln([y_ret, y_att], w_out, x, lnw, lnb, alpha), (wg, wu, wd)


def _mixer_gated_deltanet(x, w_in, conv_w, a_log, dt_bias, norm_w, w_out, lnw, lnb, alpha, batch, seq,
                          ffn_stacks, ffn_at):
    D = x.shape[1]
    heads = D // HEAD_DIM
    gate_stack, up_stack, down_stack = ffn_stacks
    w_main = w_in.astype(MXU_DTYPE)
    qkv, wg, wu = _proj(x, w_main, 0, 3 * D, F32, cast=[(gate_stack, *ffn_at), (up_stack, *ffn_at)])
    gate, wd = _proj(x, w_main, 3 * D, D, F32, cast=[(down_stack, *ffn_at)])
    g_col, g_row = _dn_gates(x, w_in[:, 4 * D:], a_log, dt_bias, heads)
    o = _delta_rule(qkv, conv_w, gate, g_col, g_row, norm_w, batch, seq, heads)
    return _proj_ln([o], w_out, x, lnw, lnb, alpha), (wg, wu, wd)


def kernel(x, ffn_w_gate, ffn_w_up, ffn_w_down, ln_w, ln_b, ab_w_in, ab_gn_w, ab_w_out,
           dn_w_in, dn_conv_w, dn_a_log, dn_dt_bias, dn_norm_w, dn_w_out):
    batch, seq, D = x.shape
    depth = ffn_w_gate.shape[0]
    alpha = (2.0 * depth) ** 0.25
    h = x.reshape(batch * seq, D)
    stacks = (ffn_w_gate, ffn_w_up, ffn_w_down)
    for l in range(depth):
        h = _ffn_ln(h, *(_cast_weight(w, l, 0) for w in stacks), ln_w[l, 0], ln_b[l, 0], alpha)
        i = l // 2
        if l % 2 == 0:
            h, post = _mixer_retention_swa(h, ab_w_in[i].astype(MXU_DTYPE), ab_gn_w[i],
                                           ab_w_out[i].astype(MXU_DTYPE), ln_w[l, 1], ln_b[l, 1], alpha,
                                           batch, seq, stacks, (l, 1))
        else:
            h, post = _mixer_gated_deltanet(h, dn_w_in[i], dn_conv_w[i], dn_a_log[i], dn_dt_bias[i], dn_norm_w[i],
                                            dn_w_out[i].astype(MXU_DTYPE), ln_w[l, 1], ln_b[l, 1], alpha,
                                            batch, seq, stacks, (l, 1))
        h = _ffn_ln(h, *post, ln_w[l, 2], ln_b[l, 2], alpha)
    return h.reshape(batch, seq, D)
```

```python
import functools
import math

import jax
import jax.numpy as jnp
from jax import lax
from jax.experimental import pallas as pl
from jax.experimental.pallas import tpu as pltpu

F32 = jnp.float32
MXU_DTYPE = jnp.bfloat16
HEAD_DIM = 128
LN_EPS = 1e-5
NORM_EPS = 1e-6
ROPE_BASE = 10000.0
SWA_PATTERNS = ((128, 1), (512, 4), (2048, 16))
SWA_FAR = max(w for w, _ in SWA_PATTERNS)
SWA_ROWS = 128
DN_CHUNK = 128
LN_ROWS = 128
DN_GROUP = 8
RET_CHUNK = 128
MASKED = -1e30
V7X_VMEM_LIMIT_BYTES = 56 * 1024 * 1024


def _cparams(sem):
    return pltpu.CompilerParams(dimension_semantics=sem, vmem_limit_bytes=V7X_VMEM_LIMIT_BYTES)


def _mm(a, b):
    return jnp.dot(a.astype(MXU_DTYPE), b.astype(MXU_DTYPE), preferred_element_type=F32)


def _mm_nt(a, b):
    return lax.dot_general(a.astype(MXU_DTYPE), b.astype(MXU_DTYPE), (((1,), (1,)), ((), ())),
                           preferred_element_type=F32)


def _mm_tn(a, b):
    return lax.dot_general(a.astype(MXU_DTYPE), b.astype(MXU_DTYPE), (((0,), (0,)), ((), ())),
                           preferred_element_type=F32)


def _silu(x):
    return x * jax.nn.sigmoid(x)


def _layer_norm(y, w, b):
    mu = jnp.mean(y, axis=-1, keepdims=True)
    yc = y - mu
    var = jnp.mean(yc * yc, axis=-1, keepdims=True)
    return yc * lax.rsqrt(var + LN_EPS) * w + b


def _tile(n, pref, unit):
    if n <= pref:
        return n
    t = (pref // unit) * unit
    while t > unit and n % t:
        t -= unit
    assert n % t == 0, (n, pref, unit)
    return t


def _residual_ln_rows(x_ref, o_ref, lnw_ref, lnb_ref, alpha, rows):
    tm, d = o_ref.shape
    slabs = [slice(c * 128, (c + 1) * 128) for c in range(d // 128)]

    def body(r, carry):
        sl = pl.ds(pl.multiple_of(r * rows, rows), rows)
        s1 = jnp.zeros((rows, 128), F32)
        for cs in slabs:
            y = alpha * x_ref[sl, cs] + o_ref[sl, cs]
            o_ref[sl, cs] = y
            s1 = s1 + y
        mu = jnp.sum(s1, axis=-1, keepdims=True) * (1.0 / d)
        s2 = jnp.zeros((rows, 128), F32)
        for cs in slabs:
            yc = o_ref[sl, cs] - mu
            s2 = s2 + yc * yc
        rstd = lax.rsqrt(jnp.sum(s2, axis=-1, keepdims=True) * (1.0 / d) + LN_EPS)
        for cs in slabs:
            o_ref[sl, cs] = (o_ref[sl, cs] - mu) * rstd * lnw_ref[:, cs] + lnb_ref[:, cs]
        return carry

    lax.fori_loop(0, tm // rows, body, 0)


def _cast_jobs(cast, n_steps, step):
    in_specs, out_specs, out_shapes = [], [], []
    for array, prefix in cast:
        rows, cols = array.shape[-2:]
        units = rows // 16
        nrb = max(d for d in range(1, min(units, n_steps) + 1) if units % d == 0)
        br = rows // nrb
        blk = lambda *ids, nrb=nrb: jnp.minimum(step(*ids), nrb - 1)
        in_specs.append(pl.BlockSpec((None,) * len(prefix) + (br, cols),
                                     lambda *ids, prefix=prefix, blk=blk: (*prefix, blk(*ids), 0)))
        out_specs.append(pl.BlockSpec((br, cols), lambda *ids, blk=blk: (blk(*ids), 0)))
        out_shapes.append(jax.ShapeDtypeStruct((rows, cols), MXU_DTYPE))
    return in_specs, out_specs, out_shapes


def _ffn_kernel(*refs, alpha, n_cast, n_split, rows):
    x_ref, wg_ref, wu_ref, wd_ref, lnw_ref, lnb_ref = refs[:6]
    src_refs = refs[6:6 + n_cast]
    o_ref = refs[6 + n_cast]
    dst_refs = refs[7 + n_cast:7 + 2 * n_cast]
    xb_ref = refs[7 + 2 * n_cast]
    j = pl.program_id(1)
    for src_ref, dst_ref in zip(src_refs, dst_refs):
        dst_ref[...] = src_ref[...].astype(dst_ref.dtype)

    @pl.when(j == 0)
    def _():
        xb_ref[...] = x_ref[...].astype(xb_ref.dtype)
        o_ref[...] = jnp.zeros_like(o_ref)

    xb = xb_ref[...]
    g = jnp.dot(xb, wg_ref[...], preferred_element_type=F32)
    u = jnp.dot(xb, wu_ref[...], preferred_element_type=F32)
    h = (_silu(g) * u * 0.5).astype(xb_ref.dtype)
    dn = o_ref.shape[1] // n_split
    for n in range(n_split):
        cols = slice(n * dn, (n + 1) * dn)
        o_ref[:, cols] += jnp.dot(h, wd_ref[:, cols], preferred_element_type=F32)

    @pl.when(j == pl.num_programs(1) - 1)
    def _():
        _residual_ln_rows(x_ref, o_ref, lnw_ref, lnb_ref, alpha, rows)


def _ffn_ln(x, wg, wu, wd, lnw, lnb, alpha, cast=()):
    T, D = x.shape
    F = wg.shape[-1]
    tm = _tile(T, 512, 8)
    tf = _tile(F, 256, 128)
    nf = F // tf
    c_in, c_out, c_shape = _cast_jobs(cast, (T // tm) * nf, lambda i, j: i * nf + j)
    kern = functools.partial(_ffn_kernel, alpha=alpha, n_cast=len(cast), n_split=max(D // 512, 1), rows=LN_ROWS)
    return pl.pallas_call(
        kern,
        grid=(T // tm, nf),
        in_specs=[
            pl.BlockSpec((tm, D), lambda i, j: (i, 0)),
            pl.BlockSpec((D, tf), lambda i, j: (0, j)),
            pl.BlockSpec((D, tf), lambda i, j: (0, j)),
            pl.BlockSpec((tf, D), lambda i, j: (j, 0)),
            pl.BlockSpec((1, D), lambda i, j: (0, 0)),
            pl.BlockSpec((1, D), lambda i, j: (0, 0)),
        ] + c_in,
        out_specs=[pl.BlockSpec((tm, D), lambda i, j: (i, 0))] + c_out,
        out_shape=[jax.ShapeDtypeStruct((T, D), F32)] + c_shape,
        scratch_shapes=[pltpu.VMEM((tm, D), MXU_DTYPE)],
        compiler_params=_cparams(("arbitrary", "arbitrary")),
        name="ffn_ln",
    )(x, wg, wu, wd, lnw.reshape(1, D), lnb.reshape(1, D), *(a for a, _ in cast))


def _cast_kernel(src_ref, dst_ref):
    dst_ref[...] = src_ref[...].astype(dst_ref.dtype)


def _cast_weight(stack, l, k):
    rows, cols = stack.shape[2:]
    units = rows // 16
    nrb = max(d for d in range(1, units + 1) if units % d == 0 and (rows // d) * cols * 4 >= (2 << 20) or d == 1)
    br = rows // nrb
    return pl.pallas_call(
        _cast_kernel,
        grid=(nrb,),
        in_specs=[pl.BlockSpec((None, None, br, cols), lambda r: (l, k, r, 0))],
        out_specs=pl.BlockSpec((br, cols), lambda r: (r, 0)),
        out_shape=jax.ShapeDtypeStruct((rows, cols), MXU_DTYPE),
        compiler_params=_cparams(("parallel",)),
        name="cast_weight",
    )(stack)


def _proj_kernel(*refs, n_cast):
    x_ref, w_ref = refs[:2]
    src_refs = refs[2:2 + n_cast]
    o_ref = refs[2 + n_cast]
    dst_refs = refs[3 + n_cast:]
    o_ref[...] = jnp.dot(x_ref[...].astype(w_ref.dtype), w_ref[...], preferred_element_type=F32).astype(o_ref.dtype)
    for src_ref, dst_ref in zip(src_refs, dst_refs):
        dst_ref[...] = src_ref[...].astype(dst_ref.dtype)


def _proj(x, w, col0, ncols, out_dtype, cast=()):
    T, K = x.shape
    tm = _tile(T, 512, 8)
    tn = _tile(math.gcd(ncols, col0) if col0 else ncols, 2048, 128)
    j0 = col0 // tn
    n_i = T // tm
    c_in, c_out, c_shape = _cast_jobs(cast, (ncols // tn) * n_i, lambda j, i: j * n_i + i)
    return pl.pallas_call(
        functools.partial(_proj_kernel, n_cast=len(cast)),
        grid=(ncols // tn, n_i),
        in_specs=[pl.BlockSpec((tm, K), lambda j, i: (i, 0)),
                  pl.BlockSpec((K, tn), lambda j, i: (0, j0 + j), pipeline_mode=pl.Buffered(1))] + c_in,
        out_specs=[pl.BlockSpec((tm, tn), lambda j, i: (i, j))] + c_out,
        out_shape=[jax.ShapeDtypeStruct((T, ncols), out_dtype)] + c_shape,
        compiler_params=_cparams(("arbitrary", "arbitrary")),
        name="in_proj",
    )(x, w, *(a for a, _ in cast))


def _proj_ln_kernel(*refs, alpha, n_parts, n_split, rows):
    a_refs = refs[:n_parts]
    w_ref, x_ref, lnw_ref, lnb_ref, o_ref = refs[n_parts:]
    dn = o_ref.shape[1] // n_split
    for n in range(n_split):
        cols = slice(n * dn, (n + 1) * dn)
        k0 = 0
        acc = None
        for a_ref in a_refs:
            kp = a_ref.shape[1]
            part = jnp.dot(a_ref[...], w_ref[k0:k0 + kp, cols], preferred_element_type=F32)
            acc = part if acc is None else acc + part
            k0 += kp
        o_ref[:, cols] = acc
    _residual_ln_rows(x_ref, o_ref, lnw_ref, lnb_ref, alpha, rows)


def _proj_ln(parts, w, x, lnw, lnb, alpha):
    T = x.shape[0]
    K, D = w.shape
    assert sum(p.shape[1] for p in parts) == K
    tm = _tile(T, 256, 8)
    kern = functools.partial(_proj_ln_kernel, alpha=alpha, n_parts=len(parts), n_split=max(D // 512, 1),
                             rows=LN_ROWS)
    return pl.pallas_call(
        kern,
        grid=(T // tm,),
        in_specs=[pl.BlockSpec((tm, p.shape[1]), lambda i: (i, 0)) for p in parts] + [
            pl.BlockSpec((K, D), lambda i: (0, 0), pipeline_mode=pl.Buffered(1)),
            pl.BlockSpec((tm, D), lambda i: (i, 0)),
            pl.BlockSpec((1, D), lambda i: (0, 0)),
            pl.BlockSpec((1, D), lambda i: (0, 0)),
        ],
        out_specs=pl.BlockSpec((tm, D), lambda i: (i, 0)),
        out_shape=jax.ShapeDtypeStruct((T, D), F32),
        compiler_params=_cparams(("parallel",)),
        name="out_proj_ln",
    )(*parts, w, x, lnw.reshape(1, D), lnb.reshape(1, D))


def _retention_kernel(q_ref, k_ref, v_ref, g_ref, cos_ref, sin_ref, lg_ref, gnw_ref, o_ref, st_ref, *, n_chunks):
    c = RET_CHUNK

    @pl.when(pl.program_id(2) == 0)
    def _():
        st_ref[...] = jnp.zeros_like(st_ref)

    lg = lg_ref[0]
    ii = lax.broadcasted_iota(jnp.int32, (c, c), 0)
    jj = lax.broadcasted_iota(jnp.int32, (c, c), 1)
    rel = (ii - jj).astype(F32)
    dmask = jnp.where(rel >= 0, jnp.exp(lg * jnp.maximum(rel, 0.0)), 0.0)
    row = ii.astype(F32)
    zeta = jnp.exp(lg * (c - 1 - row))
    xi = jnp.exp(lg * (row + 1.0))
    chunk_decay = jnp.exp(lg * c)
    even_lane = (jj & 1) == 0

    def rotate(t, cs, sn):
        partner = jnp.where(even_lane, pltpu.roll(t, HEAD_DIM - 1, 1), pltpu.roll(t, 1, 1))
        return t * cs + partner * sn

    chunks = [slice(n * c, (n + 1) * c) for n in range(n_chunks)]
    qr = [rotate(q_ref[sl, :], cos_ref[sl, :], sin_ref[sl, :]) for sl in chunks]
    kr = [rotate(k_ref[sl, :], cos_ref[sl, :], sin_ref[sl, :]) * (HEAD_DIM ** -0.5) for sl in chunks]
    intra = [_mm(_mm_nt(qr[n], kr[n]) * dmask, v_ref[sl, :]) for n, sl in enumerate(chunks)]
    kv = [_mm_tn(kr[n] * zeta, v_ref[sl, :]) for n, sl in enumerate(chunks)]
    st = st_ref[...]
    for n, sl in enumerate(chunks):
        y = intra[n] + _mm(qr[n], st) * xi
        st = st * chunk_decay + kv[n]
        mu = jnp.mean(y, axis=-1, keepdims=True)
        yc = y - mu
        var = jnp.mean(yc * yc, axis=-1, keepdims=True)
        yn = yc * lax.rsqrt(var + LN_EPS) * gnw_ref[...]
        o_ref[sl, :] = (yn * _silu(g_ref[sl, :])).astype(o_ref.dtype)
    st_ref[...] = st


def _retention(proj, cos2, sin2, log_g, gn_w, batch, seq, heads):
    T = batch * seq
    ts = _tile(seq, 1024, RET_CHUNK)
    ns = seq // ts
    blk = lambda off: pl.BlockSpec((ts, HEAD_DIM), lambda b, h, s: (b * ns + s, off * heads + h))
    tab = pl.BlockSpec((ts, HEAD_DIM), lambda b, h, s: (s, 0))
    kern = functools.partial(_retention_kernel, n_chunks=ts // RET_CHUNK)
    return pl.pallas_call(
        kern,
        grid=(batch, heads, ns),
        in_specs=[blk(0), blk(1), blk(2), blk(3), tab, tab,
                  pl.BlockSpec((1, 1, HEAD_DIM), lambda b, h, s: (h, 0, 0)),
                  pl.BlockSpec((1, HEAD_DIM), lambda b, h, s: (0, h))],
        out_specs=pl.BlockSpec((ts, HEAD_DIM), lambda b, h, s: (b * ns + s, h)),
        out_shape=jax.ShapeDtypeStruct((T, heads * HEAD_DIM), MXU_DTYPE),
        scratch_shapes=[pltpu.VMEM((HEAD_DIM, HEAD_DIM), F32)],
        compiler_params=_cparams(("parallel", "parallel", "arbitrary")),
        name="retention",
    )(proj, proj, proj, proj, cos2, sin2, log_g, gn_w.reshape(1, heads * HEAD_DIM))


def _swa_kernel(q_ref, k_ref, v_ref, bias_ref, o_ref, *, tq, win):
    qi = pl.program_id(2)
    start = pl.multiple_of(jnp.maximum(qi * tq - SWA_FAR, 0), tq)
    kw = k_ref[pl.ds(start, win), :]
    vw = v_ref[pl.ds(start, win), :]
    rows = [slice(r * SWA_ROWS, (r + 1) * SWA_ROWS) for r in range(tq // SWA_ROWS)]
    s = [_mm_nt(q_ref[r, :], kw) * (HEAD_DIM ** -0.5) + bias_ref[0, r, :] for r in rows]
    p = [jnp.exp(si - jnp.max(si, axis=-1, keepdims=True)) for si in s]
    l = [jnp.sum(pi, axis=-1, keepdims=True) for pi in p]
    for r, pi, li in zip(rows, p, l):
        o_ref[r, :] = (_mm(pi, vw) / li).astype(o_ref.dtype)


def _swa_bias(tq, win):
    nvar = SWA_FAR // tq + 1
    v = lax.broadcasted_iota(jnp.int32, (nvar, tq, win), 0)
    i = lax.broadcasted_iota(jnp.int32, (nvar, tq, win), 1)
    c = lax.broadcasted_iota(jnp.int32, (nvar, tq, win), 2)
    dist = SWA_FAR - v * tq + i - c
    mult = jnp.zeros((nvar, tq, win), F32)
    for window, dilation in SWA_PATTERNS:
        mult += ((dist >= 0) & (dist <= window) & (dist % dilation == 0)).astype(F32)
    return jnp.where(mult > 0, jnp.log(jnp.maximum(mult, 1.0)), MASKED)


def _swa(proj, batch, seq, heads):
    T = batch * seq
    tq = _tile(seq, 512, 128)
    assert SWA_FAR % tq == 0 and seq >= SWA_FAR + tq
    win = SWA_FAR + tq
    nq = seq // tq
    nfar = SWA_FAR // tq
    kern = functools.partial(_swa_kernel, tq=tq, win=win)
    return pl.pallas_call(
        kern,
        grid=(batch, heads, nq),
        in_specs=[
            pl.BlockSpec((tq, HEAD_DIM), lambda b, h, i: (b * nq + i, h)),
            pl.BlockSpec((seq, HEAD_DIM), lambda b, h, i: (b, heads + h)),
            pl.BlockSpec((seq, HEAD_DIM), lambda b, h, i: (b, 2 * heads + h)),
            pl.BlockSpec((1, tq, win), lambda b, h, i: (jnp.maximum(nfar - i, 0), 0, 0)),
        ],
        out_specs=pl.BlockSpec((tq, HEAD_DIM), lambda b, h, i: (b * nq + i, h)),
        out_shape=jax.ShapeDtypeStruct((T, heads * HEAD_DIM), MXU_DTYPE),
        compiler_params=_cparams(("parallel", "parallel", "arbitrary")),
        name="dilated_swa",
    )(proj, proj, proj, _swa_bias(tq, win))


def _softplus(x):
    return jnp.maximum(x, 0.0) + jnp.log1p(jnp.exp(-jnp.abs(x)))


def _dn_gates_kernel(x_ref, w_ref, pc_ref, col_ref, row_ref, *, heads):
    c = DN_CHUNK
    xb = x_ref[...].astype(MXU_DTYPE)
    col = jnp.dot(xb, w_ref[...], preferred_element_type=F32)
    ii = lax.broadcasted_iota(jnp.int32, (c, c), 0)
    jj = lax.broadcasted_iota(jnp.int32, (c, c), 1)
    incl = (ii >= jj).astype(F32)
    g_col = -jnp.exp(pc_ref[0:1, :]) * _softplus(col + pc_ref[1:2, :])
    gc_col = jnp.dot(incl, g_col, preferred_element_type=F32, precision=lax.Precision.HIGHEST)
    lane = lax.broadcasted_iota(jnp.int32, col.shape, 1)
    out = jnp.where(lane < heads, gc_col, jax.nn.sigmoid(col))
    col_ref[...] = out
    row_ref[...] = out.T


def _dn_gates(x, w_small, a_log, dt_bias, heads):
    T, D = x.shape
    assert 2 * heads <= 128
    pad = 128 - 2 * heads
    w = jnp.pad(w_small, ((0, 0), (0, pad))).astype(MXU_DTYPE)
    zeros = jnp.zeros((128 - heads,), F32)
    params = jnp.stack([jnp.concatenate([a_log.astype(F32), zeros]), jnp.concatenate([dt_bias.astype(F32), zeros])])
    c = DN_CHUNK
    kern = functools.partial(_dn_gates_kernel, heads=heads)
    return pl.pallas_call(
        kern,
        grid=(T // c,),
        in_specs=[
            pl.BlockSpec((c, D), lambda i: (i, 0)),
            pl.BlockSpec((D, 128), lambda i: (0, 0)),
            pl.BlockSpec((2, 128), lambda i: (0, 0)),
        ],
        out_specs=[pl.BlockSpec((c, 128), lambda i: (i, 0)), pl.BlockSpec((128, c), lambda i: (0, i))],
        out_shape=[jax.ShapeDtypeStruct((T, 128), F32), jax.ShapeDtypeStruct((128, T), F32)],
        compiler_params=_cparams(("parallel",)),
        name="dn_gates",
    )(x, w, params)


def _conv_silu(x, prev, w):
    c = x.shape[0]
    taps = w.shape[0]
    xp = jnp.concatenate([prev, x], axis=0)
    acc = x * w[taps - 1:taps, :]
    for i in range(taps - 1):
        off = 8 - (taps - 1) + i
        acc = acc + xp[off:off + c, :] * w[i:i + 1, :]
    return _silu(acc)


def _l2norm(t):
    return t * lax.rsqrt(jnp.sum(t * t, axis=-1, keepdims=True) + NORM_EPS)


def _dn_kernel(q_ref, k_ref, v_ref, cwq_ref, cwk_ref, cwv_ref, gate_ref, col_ref, row_ref, nw_ref, o_ref,
               st_ref, tail_ref, *, group, heads):
    c = DN_CHUNK
    hs = range(group)

    @pl.when(pl.program_id(2) == 0)
    def _():
        st_ref[...] = jnp.zeros_like(st_ref)
        tail_ref[...] = jnp.zeros_like(tail_ref)

    ii = lax.broadcasted_iota(jnp.int32, (c, c), 0)
    jj = lax.broadcasted_iota(jnp.int32, (c, c), 1)
    eye = (ii == jj).astype(F32)
    n_levels = int(math.log2(c))
    level = [(ii >> 1) == (jj >> 1)] + [
        ((ii >> (ls + 1)) == (jj >> (ls + 1))) & (((ii >> ls) & 1) == 1) & (((jj >> ls) & 1) == 0)
        for ls in range(1, n_levels)]
    col = col_ref[...]
    lane = lax.broadcasted_iota(jnp.int32, col.shape, 1)
    cols = [slice(h * HEAD_DIM, (h + 1) * HEAD_DIM) for h in hs]
    head = [pl.program_id(1) * group + h for h in hs]
    q = [_l2norm(_conv_silu(q_ref[:, cols[h]], tail_ref[0, :, cols[h]], cwq_ref[:, cols[h]])) * (HEAD_DIM ** -0.5)
         for h in hs]
    k = [_l2norm(_conv_silu(k_ref[:, cols[h]], tail_ref[1, :, cols[h]], cwk_ref[:, cols[h]])) for h in hs]
    v = [_conv_silu(v_ref[:, cols[h]], tail_ref[2, :, cols[h]], cwv_ref[:, cols[h]]) for h in hs]
    for i, ref in enumerate((q_ref, k_ref, v_ref)):
        tail_ref[i] = ref[c - 8:c, :]
    gc_i =[jnp.sum(jnp.where(lane == head[h], col, 0.0), axis=-1, keepdims=True) for h in hs]
    beta = [jnp.sum(jnp.where(lane == head[h] + heads, col, 0.0), axis=-1, keepdims=True) for h in hs]
    gc_j = [row_ref[pl.ds(head[h], 1), :] for h in hs]
    gc_last = [gc_i[h][c - 1:c, :] for h in hs]
    decay = [jnp.exp(jnp.where(ii >= jj, gc_i[h] - gc_j[h], MASKED)) for h in hs]
    e_g = [jnp.exp(gc_i[h]) for h in hs]
    kb = [k[h] * beta[h] for h in hs]
    kq = [_mm_nt(jnp.concatenate([kb[h], q[h]], axis=0), k[h]) for h in hs]
    a = [jnp.where(ii > jj, kq[h][:c] * decay[h], 0.0) for h in hs]
    qk = [kq[h][c:] * decay[h] for h in hs]
    w0 = [jnp.where(level[0], a[h], 0.0) for h in hs]
    t = [eye - w0[h] for h in hs]
    z = [a[h] - _mm(a[h], w0[h]) for h in hs]
    for ls in range(1, n_levels):
        wl = [jnp.where(level[ls], z[h], 0.0) for h in hs]
        if ls < n_levels - 1:
            tz = [_mm(jnp.concatenate([t[h], z[h]], axis=0), wl[h]) for h in hs]
            t = [t[h] - tz[h][:c] for h in hs]
            z = [z[h] - tz[h][c:] for h in hs]
        else:
            t = [t[h] - _mm(t[h], wl[h]) for h in hs]
    uw = [_mm(t[h], jnp.concatenate([v[h] * beta[h], kb[h] * e_g[h]], axis=1)) for h in hs]
    st = [st_ref[h] for h in hs]
    ws_qs = [_mm(jnp.concatenate([uw[h][:, HEAD_DIM:], q[h] * e_g[h]], axis=0), st[h]) for h in hs]
    v_new = [uw[h][:, :HEAD_DIM] - ws_qs[h][:c] for h in hs]
    o = [ws_qs[h][c:] + _mm(qk[h], v_new[h]) for h in hs]
    for h in hs:
        st_ref[h] = st[h] * jnp.exp(gc_last[h]) + _mm_tn(k[h] * jnp.exp(gc_last[h] - gc_i[h]), v_new[h])
    for h in hs:
        on = o[h] * lax.rsqrt(jnp.mean(o[h] * o[h], axis=-1, keepdims=True) + NORM_EPS) * nw_ref[...]
        o_ref[:, cols[h]] = (on * _silu(gate_ref[:, cols[h]])).astype(o_ref.dtype)


def _delta_rule(qkv, conv_w, gate, g_col, g_row, norm_w, batch, seq, heads):
    T = batch * seq
    c = DN_CHUNK
    group = _tile(heads, DN_GROUP, 1)
    ng = heads // group
    nc = seq // c
    gw = group * HEAD_DIM
    taps = conv_w.shape[0]
    blk = lambda off: pl.BlockSpec((c, gw), lambda b, g, n: (b * nc + n, off * ng + g))
    cw = lambda off: pl.BlockSpec((taps, gw), lambda b, g, n: (0, off * ng + g))
    kern = functools.partial(_dn_kernel, group=group, heads=heads)
    return pl.pallas_call(
        kern,
        grid=(batch, ng, nc),
        in_specs=[blk(0), blk(1), blk(2), cw(0), cw(1), cw(2),
                  pl.BlockSpec((c, gw), lambda b, g, n: (b * nc + n, g)),
                  pl.BlockSpec((c, 128), lambda b, g, n: (b * nc + n, 0)),
                  pl.BlockSpec((128, c), lambda b, g, n: (0, b * nc + n)),
                  pl.BlockSpec((1, HEAD_DIM), lambda b, g, n: (0, 0))],
        out_specs=pl.BlockSpec((c, gw), lambda b, g, n: (b * nc + n, g)),
        out_shape=jax.ShapeDtypeStruct((T, heads * HEAD_DIM), MXU_DTYPE),
        scratch_shapes=[pltpu.VMEM((group, HEAD_DIM, HEAD_DIM), F32),
                        pltpu.VMEM((3, 8, gw), F32)],
        compiler_params=_cparams(("parallel", "parallel", "arbitrary")),
        name="delta_rule",
    )(qkv, qkv, qkv, conv_w, conv_w, conv_w, gate, g_col, g_row, norm_w.reshape(1, HEAD_DIM))


def _rope_tables(seq):
    pos = jnp.arange(seq, dtype=F32)
    inv_freq = 1.0 / (ROPE_BASE ** jnp.linspace(0.0, 1.0, HEAD_DIM // 2, dtype=F32))
    ang = pos[:, None] * inv_freq[None, :]
    cos, sin = jnp.cos(ang), jnp.sin(ang)
    return jnp.repeat(cos, 2, axis=1), jnp.stack([-sin, sin], axis=-1).reshape(seq, HEAD_DIM)


def _mixer_retention_swa(x, w_in, gn_w, w_out, lnw, lnb, alpha, batch, seq, ffn_cast):
    D = x.shape[1]
    width = D // 2
    heads = width // HEAD_DIM
    proj_ret, wg, wu = _proj(x, w_in, 0, 4 * width, F32, cast=ffn_cast[:2])
    proj_att, wd = _proj(x, w_in, 4 * width, 3 * width, MXU_DTYPE, cast=ffn_cast[2:])
    cos2, sin2 = _rope_tables(seq)
    log_g = jnp.log1p(-jnp.exp2(-5.0 - jnp.arange(heads, dtype=F32)))
    log_g = jnp.broadcast_to(log_g[:, None, None], (heads, 1, HEAD_DIM))
    y_ret = _retention(proj_ret, cos2, sin2, log_g, gn_w, batch, seq, heads)
    y_att = _swa(proj_att, batch, seq, heads)
    return _proj_ln([y_ret, y_att], w_out, x, lnw, lnb, alpha), (wg, wu, wd)


def _mixer_gated_deltanet(x, w_in, w_gates, conv_w, a_log, dt_bias, norm_w, w_out, lnw, lnb, alpha, batch, seq,
                          ffn_cast):
    D = x.shape[1]
    heads = D // HEAD_DIM
    qkv, wg, wu = _proj(x, w_in, 0, 3 * D, F32, cast=ffn_cast[:2])
    gate, wd = _proj(x, w_in, 3 * D, D, F32, cast=ffn_cast[2:])
    g_col, g_row = _dn_gates(x, w_gates, a_log, dt_bias, heads)
    o = _delta_rule(qkv, conv_w, gate, g_col, g_row, norm_w, batch, seq, heads)
    return _proj_ln([o], w_out, x, lnw, lnb, alpha), (wg, wu, wd)


def kernel(x, ffn_w_gate, ffn_w_up, ffn_w_down, ln_w, ln_b, ab_w_in, ab_gn_w, ab_w_out,
           dn_w_in, dn_conv_w, dn_a_log, dn_dt_bias, dn_norm_w, dn_w_out):
    batch, seq, D = x.shape
    depth = ffn_w_gate.shape[0]
    alpha = (2.0 * depth) ** 0.25
    h = x.reshape(batch * seq, D)
    stacks = (ffn_w_gate, ffn_w_up, ffn_w_down)
    pre = [_cast_weight(w, 0, 0) for w in stacks]
    for l in range(depth):
        i = l // 2
        post_cast = [(w, (l, 1)) for w in stacks]
        if l % 2 == 0:
            h, w_in, w_out = _ffn_ln(h, *pre, ln_w[l, 0], ln_b[l, 0], alpha, cast=[(ab_w_in, (i,)), (ab_w_out, (i,))])
            h, post = _mixer_retention_swa(h, w_in, ab_gn_w[i], w_out, ln_w[l, 1], ln_b[l, 1], alpha,
                                           batch, seq, post_cast)
        else:
            h, w_out = _ffn_ln(h, *pre, ln_w[l, 0], ln_b[l, 0], alpha, cast=[(dn_w_out, (i,))])
            h, post = _mixer_gated_deltanet(h, dn_w_in[i].astype(MXU_DTYPE), dn_w_in[i][:, 4 * D:], dn_conv_w[i],
                                            dn_a_log[i], dn_dt_bias[i], dn_norm_w[i], w_out, ln_w[l, 1], ln_b[l, 1],
                                            alpha, batch, seq, post_cast)
        next_cast = [(w, (l + 1, 0)) for w in stacks] if l + 1 < depth else []
        h, *pre = _ffn_ln(h, *post, ln_w[l, 2], ln_b[l, 2], alpha, cast=next_cast)
    return h.reshape(batch, seq, D)
```

```python
import functools
import math

import jax
import jax.numpy as jnp
from jax import lax
from jax.experimental import pallas as pl
from jax.experimental.pallas import tpu as pltpu

F32 = jnp.float32
MXU_DTYPE = jnp.bfloat16
HEAD_DIM = 128
LN_EPS = 1e-5
NORM_EPS = 1e-6
ROPE_BASE = 10000.0
SWA_PATTERNS = ((128, 1), (512, 4), (2048, 16))
SWA_FAR = max(w for w, _ in SWA_PATTERNS)
SWA_ROWS = 128
FFN_TILE = 256
DN_CHUNK = 128
LN_ROWS = 128
DN_GROUP = 8
RET_CHUNK = 128
MASKED = -1e30
V7X_VMEM_LIMIT_BYTES = 56 * 1024 * 1024


def _cparams(sem):
    return pltpu.CompilerParams(dimension_semantics=sem, vmem_limit_bytes=V7X_VMEM_LIMIT_BYTES)


def _mm(a, b):
    return jnp.dot(a.astype(MXU_DTYPE), b.astype(MXU_DTYPE), preferred_element_type=F32)


def _mm_nt(a, b):
    return lax.dot_general(a.astype(MXU_DTYPE), b.astype(MXU_DTYPE), (((1,), (1,)), ((), ())),
                           preferred_element_type=F32)


def _mm_tn(a, b):
    return lax.dot_general(a.astype(MXU_DTYPE), b.astype(MXU_DTYPE), (((0,), (0,)), ((), ())),
                           preferred_element_type=F32)


def _silu(x):
    return x * jax.nn.sigmoid(x)


def _layer_norm(y, w, b):
    mu = jnp.mean(y, axis=-1, keepdims=True)
    yc = y - mu
    var = jnp.mean(yc * yc, axis=-1, keepdims=True)
    return yc * lax.rsqrt(var + LN_EPS) * w + b


def _tile(n, pref, unit):
    if n <= pref:
        return n
    t = (pref // unit) * unit
    while t > unit and n % t:
        t -= unit
    assert n % t == 0, (n, pref, unit)
    return t


def _residual_ln_rows(x_ref, o_ref, ln_ref, alpha, rows):
    tm, d = o_ref.shape
    slabs = [slice(c * 128, (c + 1) * 128) for c in range(d // 128)]

    def body(r, carry):
        sl = pl.ds(pl.multiple_of(r * rows, rows), rows)
        s1 = jnp.zeros((rows, 128), F32)
        for cs in slabs:
            y = alpha * x_ref[sl, cs] + o_ref[sl, cs]
            o_ref[sl, cs] = y
            s1 = s1 + y
        mu = jnp.sum(s1, axis=-1, keepdims=True) * (1.0 / d)
        s2 = jnp.zeros((rows, 128), F32)
        for cs in slabs:
            yc = o_ref[sl, cs] - mu
            s2 = s2 + yc * yc
        rstd = lax.rsqrt(jnp.sum(s2, axis=-1, keepdims=True) * (1.0 / d) + LN_EPS)
        for cs in slabs:
            o_ref[sl, cs] = (o_ref[sl, cs] - mu) * rstd * ln_ref[0:1, cs] + ln_ref[1:2, cs]
        return carry

    lax.fori_loop(0, tm // rows, body, 0)


def _ffn_kernel(x_ref, wgu_ref, wd_ref, ln_ref, o_ref, xb_ref, *, alpha, n_split, rows):
    j = pl.program_id(1)
    tf = wd_ref.shape[0]

    @pl.when(j == 0)
    def _():
        xb_ref[...] = x_ref[...].astype(xb_ref.dtype)
        o_ref[...] = jnp.zeros_like(o_ref)

    gu = jnp.dot(xb_ref[...], wgu_ref[...], preferred_element_type=F32)
    h = (_silu(gu[:, :tf]) * gu[:, tf:] * 0.5).astype(xb_ref.dtype)
    dn = o_ref.shape[1] // n_split
    for n in range(n_split):
        cols = slice(n * dn, (n + 1) * dn)
        o_ref[:, cols] += jnp.dot(h, wd_ref[:, cols], preferred_element_type=F32)

    @pl.when(j == pl.num_programs(1) - 1)
    def _():
        _residual_ln_rows(x_ref, o_ref, ln_ref, alpha, rows)


def _ffn_ln(x, wgu, wd, lnw, lnb, alpha):
    T, D = x.shape
    nf, _, tf2 = wgu.shape
    tf = tf2 // 2
    tm = _tile(T, 1024, 8)
    kern = functools.partial(_ffn_kernel, alpha=alpha, n_split=max(D // 512, 1), rows=LN_ROWS)
    once = pl.Buffered(1)
    return pl.pallas_call(
        kern,
        grid=(T // tm, nf),
        in_specs=[
            pl.BlockSpec((tm, D), lambda i, j: (i, 0), pipeline_mode=once),
            pl.BlockSpec((None, D, tf2), lambda i, j: (j, 0, 0)),
            pl.BlockSpec((tf, D), lambda i, j: (j, 0)),
            pl.BlockSpec((2, D), lambda i, j: (0, 0)),
        ],
        out_specs=pl.BlockSpec((tm, D), lambda i, j: (i, 0), pipeline_mode=once),
        out_shape=jax.ShapeDtypeStruct((T, D), F32),
        scratch_shapes=[pltpu.VMEM((tm, D), MXU_DTYPE)],
        compiler_params=_cparams(("parallel", "arbitrary")),
        name="ffn_ln",
    )(x, wgu, wd, jnp.stack([lnw, lnb]))


def _cast_jobs(cast, n_steps, step):
    in_specs, out_specs, out_shapes = [], [], []
    for arrays, prefix in cast:
        rows, cols = arrays[0].shape[-2:]
        units = rows // 16
        nrb = max(d for d in range(1, min(units, n_steps) + 1) if units % d == 0)
        br = rows // nrb
        blk = lambda *ids, nrb=nrb: jnp.minimum(step(*ids), nrb - 1)
        for _ in arrays:
            in_specs.append(pl.BlockSpec((None,) * len(prefix) + (br, cols),
                                         lambda *ids, prefix=prefix, blk=blk: (*prefix, blk(*ids), 0)))
        if len(arrays) == 1:
            out_specs.append(pl.BlockSpec((br, cols), lambda *ids, blk=blk: (blk(*ids), 0)))
            out_shapes.append(jax.ShapeDtypeStruct((rows, cols), MXU_DTYPE))
        else:
            nf = cols // FFN_TILE
            out_specs.append(pl.BlockSpec((nf, br, 2 * FFN_TILE), lambda *ids, blk=blk: (0, blk(*ids), 0)))
            out_shapes.append(jax.ShapeDtypeStruct((nf, rows, 2 * FFN_TILE), MXU_DTYPE))
    return in_specs, out_specs, out_shapes


def _run_casts(src_refs, dst_refs, arity):
    s = 0
    for dst_ref, n_src in zip(dst_refs, arity):
        if n_src == 1:
            dst_ref[...] = src_refs[s][...].astype(dst_ref.dtype)
        else:
            gate_ref, up_ref = src_refs[s], src_refs[s + 1]
            for j in range(dst_ref.shape[0]):
                cols = slice(j * FFN_TILE, (j + 1) * FFN_TILE)
                dst_ref[j, :, :FFN_TILE] = gate_ref[:, cols].astype(dst_ref.dtype)
                dst_ref[j, :, FFN_TILE:] = up_ref[:, cols].astype(dst_ref.dtype)
        s += n_src


def _cast_kernel(*refs, arity):
    _run_casts(refs[:sum(arity)], refs[sum(arity):], arity)


def _cast_weight(arrays, prefix):
    rows, cols = arrays[0].shape[-2:]
    n_steps = max(1, (rows * cols * 4) // (2 << 20))
    c_in, c_out, c_shape = _cast_jobs([(arrays, prefix)], n_steps, lambda r: r)
    nrb = rows // c_out[0].block_shape[-2]
    return pl.pallas_call(
        functools.partial(_cast_kernel, arity=(len(arrays),)),
        grid=(nrb,),
        in_specs=c_in,
        out_specs=c_out,
        out_shape=c_shape,
        compiler_params=_cparams(("parallel",)),
        name="cast_weight",
    )(*arrays)[0]


def _proj_kernel(*refs, arity):
    n_src = sum(arity)
    x_ref, w_ref = refs[:2]
    o_ref = refs[2 + n_src]
    o_ref[...] = jnp.dot(x_ref[...].astype(w_ref.dtype), w_ref[...], preferred_element_type=F32).astype(o_ref.dtype)
    _run_casts(refs[2:2 + n_src], refs[3 + n_src:], arity)


def _proj(x, w, col0, ncols, out_dtype, cast=()):
    T, K = x.shape
    tm = _tile(T, 512, 8)
    tn = _tile(math.gcd(ncols, col0) if col0 else ncols, 2048, 128)
    j0 = col0 // tn
    n_i = T // tm
    c_in, c_out, c_shape = _cast_jobs(cast, (ncols // tn) * n_i, lambda j, i: j * n_i + i)
    return pl.pallas_call(
        functools.partial(_proj_kernel, arity=tuple(len(arrays) for arrays, _ in cast)),
        grid=(ncols // tn, n_i),
        in_specs=[pl.BlockSpec((tm, K), lambda j, i: (i, 0)),
                  pl.BlockSpec((K, tn), lambda j, i: (0, j0 + j), pipeline_mode=pl.Buffered(1))] + c_in,
        out_specs=[pl.BlockSpec((tm, tn), lambda j, i: (i, j))] + c_out,
        out_shape=[jax.ShapeDtypeStruct((T, ncols), out_dtype)] + c_shape,
        compiler_params=_cparams(("arbitrary", "arbitrary")),
        name="in_proj",
    )(x, w, *(a for arrays, _ in cast for a in arrays))


def _proj_ln_kernel(*refs, alpha, n_parts, n_split, rows):
    a_refs = refs[:n_parts]
    w_ref, x_ref, ln_ref, o_ref = refs[n_parts:]
    dn = o_ref.shape[1] // n_split
    for n in range(n_split):
        cols = slice(n * dn, (n + 1) * dn)
        k0 = 0
        acc = None
        for a_ref in a_refs:
            kp = a_ref.shape[1]
            part = jnp.dot(a_ref[...], w_ref[k0:k0 + kp, cols], preferred_element_type=F32)
            acc = part if acc is None else acc + part
            k0 += kp
        o_ref[:, cols] = acc
    _residual_ln_rows(x_ref, o_ref, ln_ref, alpha, rows)


def _proj_ln(parts, w, x, lnw, lnb, alpha):
    T = x.shape[0]
    K, D = w.shape
    assert sum(p.shape[1] for p in parts) == K
    tm = _tile(T, 256, 8)
    kern = functools.partial(_proj_ln_kernel, alpha=alpha, n_parts=len(parts), n_split=max(D // 512, 1),
                             rows=LN_ROWS)
    return pl.pallas_call(
        kern,
        grid=(T // tm,),
        in_specs=[pl.BlockSpec((tm, p.shape[1]), lambda i: (i, 0)) for p in parts] + [
            pl.BlockSpec((K, D), lambda i: (0, 0), pipeline_mode=pl.Buffered(1)),
            pl.BlockSpec((tm, D), lambda i: (i, 0)),
            pl.BlockSpec((2, D), lambda i: (0, 0)),
        ],
        out_specs=pl.BlockSpec((tm, D), lambda i: (i, 0)),
        out_shape=jax.ShapeDtypeStruct((T, D), F32),
        compiler_params=_cparams(("parallel",)),
        name="out_proj_ln",
    )(*parts, w, x, jnp.stack([lnw, lnb]))


def _retention_kernel(q_ref, k_ref, v_ref, g_ref, cos_ref, sin_ref, lg_ref, gnw_ref, o_ref, st_ref, *, n_chunks):
    c = RET_CHUNK

    @pl.when(pl.program_id(2) == 0)
    def _():
        st_ref[...] = jnp.zeros_like(st_ref)

    lg = lg_ref[0]
    ii = lax.broadcasted_iota(jnp.int32, (c, c), 0)
    jj = lax.broadcasted_iota(jnp.int32, (c, c), 1)
    rel = (ii - jj).astype(F32)
    dmask = jnp.where(rel >= 0, jnp.exp(lg * jnp.maximum(rel, 0.0)), 0.0)
    row = ii.astype(F32)
    zeta = jnp.exp(lg * (c - 1 - row))
    xi = jnp.exp(lg * (row + 1.0))
    chunk_decay = jnp.exp(lg * c)
    even_lane = (jj & 1) == 0

    def rotate(t, cs, sn):
        partner = jnp.where(even_lane, pltpu.roll(t, HEAD_DIM - 1, 1), pltpu.roll(t, 1, 1))
        return t * cs + partner * sn

    chunks = [slice(n * c, (n + 1) * c) for n in range(n_chunks)]
    qr = [rotate(q_ref[sl, :], cos_ref[sl, :], sin_ref[sl, :]) for sl in chunks]
    kr = [rotate(k_ref[sl, :], cos_ref[sl, :], sin_ref[sl, :]) * (HEAD_DIM ** -0.5) for sl in chunks]
    intra = [_mm(_mm_nt(qr[n], kr[n]) * dmask, v_ref[sl, :]) for n, sl in enumerate(chunks)]
    kv = [_mm_tn(kr[n] * zeta, v_ref[sl, :]) for n, sl in enumerate(chunks)]
    st = st_ref[...]
    for n, sl in enumerate(chunks):
        y = intra[n] + _mm(qr[n], st) * xi
        st = st * chunk_decay + kv[n]
        mu = jnp.mean(y, axis=-1, keepdims=True)
        yc = y - mu
        var = jnp.mean(yc * yc, axis=-1, keepdims=True)
        yn = yc * lax.rsqrt(var + LN_EPS) * gnw_ref[...]
        o_ref[sl, :] = (yn * _silu(g_ref[sl, :])).astype(o_ref.dtype)
    st_ref[...] = st


def _retention(proj, cos2, sin2, log_g, gn_w, batch, seq, heads):
    T = batch * seq
    ts = _tile(seq, 1024, RET_CHUNK)
    ns = seq // ts
    blk = lambda off: pl.BlockSpec((ts, HEAD_DIM), lambda b, h, s: (b * ns + s, off * heads + h))
    tab = pl.BlockSpec((ts, HEAD_DIM), lambda b, h, s: (s, 0))
    kern = functools.partial(_retention_kernel, n_chunks=ts // RET_CHUNK)
    return pl.pallas_call(
        kern,
        grid=(batch, heads, ns),
        in_specs=[blk(0), blk(1), blk(2), blk(3), tab, tab,
                  pl.BlockSpec((1, 1, HEAD_DIM), lambda b, h, s: (h, 0, 0)),
                  pl.BlockSpec((1, HEAD_DIM), lambda b, h, s: (0, h))],
        out_specs=pl.BlockSpec((ts, HEAD_DIM), lambda b, h, s: (b * ns + s, h)),
        out_shape=jax.ShapeDtypeStruct((T, heads * HEAD_DIM), MXU_DTYPE),
        scratch_shapes=[pltpu.VMEM((HEAD_DIM, HEAD_DIM), F32)],
        compiler_params=_cparams(("parallel", "parallel", "arbitrary")),
        name="retention",
    )(proj, proj, proj, proj, cos2, sin2, log_g, gn_w.reshape(1, heads * HEAD_DIM))


def _swa_kernel(q_ref, k_ref, v_ref, bias_ref, o_ref, *, tq, win):
    qi = pl.program_id(2)
    start = pl.multiple_of(jnp.maximum(qi * tq - SWA_FAR, 0), tq)
    kw = k_ref[pl.ds(start, win), :]
    vw = v_ref[pl.ds(start, win), :]
    rows = [slice(r * SWA_ROWS, (r + 1) * SWA_ROWS) for r in range(tq // SWA_ROWS)]
    s = [_mm_nt(q_ref[r, :], kw) * (HEAD_DIM ** -0.5) + bias_ref[0, r, :] for r in rows]
    p = [jnp.exp(si - jnp.max(si, axis=-1, keepdims=True)) for si in s]
    l = [jnp.sum(pi, axis=-1, keepdims=True) for pi in p]
    for r, pi, li in zip(rows, p, l):
        o_ref[r, :] = (_mm(pi, vw) / li).astype(o_ref.dtype)


def _swa_bias(tq, win):
    nvar = SWA_FAR // tq + 1
    v = lax.broadcasted_iota(jnp.int32, (nvar, tq, win), 0)
    i = lax.broadcasted_iota(jnp.int32, (nvar, tq, win), 1)
    c = lax.broadcasted_iota(jnp.int32, (nvar, tq, win), 2)
    dist = SWA_FAR - v * tq + i - c
    mult = jnp.zeros((nvar, tq, win), F32)
    for window, dilation in SWA_PATTERNS:
        mult += ((dist >= 0) & (dist <= window) & (dist % dilation == 0)).astype(F32)
    return jnp.where(mult > 0, jnp.log(jnp.maximum(mult, 1.0)), MASKED)


def _swa(proj, batch, seq, heads):
    T = batch * seq
    tq = _tile(seq, 512, 128)
    assert SWA_FAR % tq == 0 and seq >= SWA_FAR + tq
    win = SWA_FAR + tq
    nq = seq // tq
    nfar = SWA_FAR // tq
    kern = functools.partial(_swa_kernel, tq=tq, win=win)
    return pl.pallas_call(
        kern,
        grid=(batch, heads, nq),
        in_specs=[
            pl.BlockSpec((tq, HEAD_DIM), lambda b, h, i: (b * nq + i, h)),
            pl.BlockSpec((seq, HEAD_DIM), lambda b, h, i: (b, heads + h)),
            pl.BlockSpec((seq, HEAD_DIM), lambda b, h, i: (b, 2 * heads + h)),
            pl.BlockSpec((1, tq, win), lambda b, h, i: (jnp.maximum(nfar - i, 0), 0, 0)),
        ],
        out_specs=pl.BlockSpec((tq, HEAD_DIM), lambda b, h, i: (b * nq + i, h)),
        out_shape=jax.ShapeDtypeStruct((T, heads * HEAD_DIM), MXU_DTYPE),
        compiler_params=_cparams(("parallel", "parallel", "arbitrary")),
        name="dilated_swa",
    )(proj, proj, proj, _swa_bias(tq, win))


def _softplus(x):
    return jnp.maximum(x, 0.0) + jnp.log1p(jnp.exp(-jnp.abs(x)))


def _dn_gates_kernel(x_ref, w_ref, pc_ref, col_ref, row_ref, *, heads):
    c = DN_CHUNK
    xb = x_ref[...].astype(MXU_DTYPE)
    col = jnp.dot(xb, w_ref[...], preferred_element_type=F32)
    ii = lax.broadcasted_iota(jnp.int32, (c, c), 0)
    jj = lax.broadcasted_iota(jnp.int32, (c, c), 1)
    incl = (ii >= jj).astype(F32)
    g_col = -jnp.exp(pc_ref[0:1, :]) * _softplus(col + pc_ref[1:2, :])
    gc_col = jnp.dot(incl, g_col, preferred_element_type=F32, precision=lax.Precision.HIGHEST)
    lane = lax.broadcasted_iota(jnp.int32, col.shape, 1)
    out = jnp.where(lane < heads, gc_col, jax.nn.sigmoid(col))
    col_ref[...] = out
    row_ref[...] = out.T


def _dn_gates(x, w_small, a_log, dt_bias, heads):
    T, D = x.shape
    assert 2 * heads <= 128
    pad = 128 - 2 * heads
    w = jnp.pad(w_small, ((0, 0), (0, pad))).astype(MXU_DTYPE)
    zeros = jnp.zeros((128 - heads,), F32)
    params = jnp.stack([jnp.concatenate([a_log.astype(F32), zeros]), jnp.concatenate([dt_bias.astype(F32), zeros])])
    c = DN_CHUNK
    kern = functools.partial(_dn_gates_kernel, heads=heads)
    return pl.pallas_call(
        kern,
        grid=(T // c,),
        in_specs=[
            pl.BlockSpec((c, D), lambda i: (i, 0)),
            pl.BlockSpec((D, 128), lambda i: (0, 0)),
            pl.BlockSpec((2, 128), lambda i: (0, 0)),
        ],
        out_specs=[pl.BlockSpec((c, 128), lambda i: (i, 0)), pl.BlockSpec((128, c), lambda i: (0, i))],
        out_shape=[jax.ShapeDtypeStruct((T, 128), F32), jax.ShapeDtypeStruct((128, T), F32)],
        compiler_params=_cparams(("parallel",)),
        name="dn_gates",
    )(x, w, params)


def _conv_silu(x, prev, w):
    c = x.shape[0]
    taps = w.shape[0]
    xp = jnp.concatenate([prev, x], axis=0)
    acc = x * w[taps - 1:taps, :]
    for i in range(taps - 1):
        off = 8 - (taps - 1) + i
        acc = acc + xp[off:off + c, :] * w[i:i + 1, :]
    return _silu(acc)


def _l2norm(t):
    return t * lax.rsqrt(jnp.sum(t * t, axis=-1, keepdims=True) + NORM_EPS)


def _dn_kernel(q_ref, k_ref, v_ref, cwq_ref, cwk_ref, cwv_ref, gate_ref, col_ref, row_ref, nw_ref, o_ref,
               st_ref, tail_ref, *, group, heads):
    c = DN_CHUNK
    hs = range(group)

    @pl.when(pl.program_id(2) == 0)
    def _():
        st_ref[...] = jnp.zeros_like(st_ref)
        tail_ref[...] = jnp.zeros_like(tail_ref)

    ii = lax.broadcasted_iota(jnp.int32, (c, c), 0)
    jj = lax.broadcasted_iota(jnp.int32, (c, c), 1)
    eye = (ii == jj).astype(F32)
    n_levels = int(math.log2(c))
    level = [(ii >> 1) == (jj >> 1)] + [
        ((ii >> (ls + 1)) == (jj >> (ls + 1))) & (((ii >> ls) & 1) == 1) & (((jj >> ls) & 1) == 0)
        for ls in range(1, n_levels)]
    col = col_ref[...]
    lane = lax.broadcasted_iota(jnp.int32, col.shape, 1)
    cols = [slice(h * HEAD_DIM, (h + 1) * HEAD_DIM) for h in hs]
    head = [pl.program_id(1) * group + h for h in hs]
    q = [_l2norm(_conv_silu(q_ref[:, cols[h]], tail_ref[0, :, cols[h]], cwq_ref[:, cols[h]])) * (HEAD_DIM ** -0.5)
         for h in hs]
    k = [_l2norm(_conv_silu(k_ref[:, cols[h]], tail_ref[1, :, cols[h]], cwk_ref[:, cols[h]])) for h in hs]
    v = [_conv_silu(v_ref[:, cols[h]], tail_ref[2, :, cols[h]], cwv_ref[:, cols[h]]) for h in hs]
    for i, ref in enumerate((q_ref, k_ref, v_ref)):
        tail_ref[i] = ref[c - 8:c, :]
    gc_i =[jnp.sum(jnp.where(lane == head[h], col, 0.0), axis=-1, keepdims=True) for h in hs]
    beta = [jnp.sum(jnp.where(lane == head[h] + heads, col, 0.0), axis=-1, keepdims=True) for h in hs]
    gc_j = [row_ref[pl.ds(head[h], 1), :] for h in hs]
    gc_last = [gc_i[h][c - 1:c, :] for h in hs]
    decay = [jnp.exp(jnp.where(ii >= jj, gc_i[h] - gc_j[h], MASKED)) for h in hs]
    e_g = [jnp.exp(gc_i[h]) for h in hs]
    kb = [k[h] * beta[h] for h in hs]
    kq = [_mm_nt(jnp.concatenate([kb[h], q[h]], axis=0), k[h]) for h in hs]
    a = [jnp.where(ii > jj, kq[h][:c] * decay[h], 0.0) for h in hs]
    qk = [kq[h][c:] * decay[h] for h in hs]
    w0 = [jnp.where(level[0], a[h], 0.0) for h in hs]
    t = [eye - w0[h] for h in hs]
    z = [a[h] - _mm(a[h], w0[h]) for h in hs]
    for ls in range(1, n_levels):
        wl = [jnp.where(level[ls], z[h], 0.0) for h in hs]
        if ls < n_levels - 1:
            tz = [_mm(jnp.concatenate([t[h], z[h]], axis=0), wl[h]) for h in hs]
            t = [t[h] - tz[h][:c] for h in hs]
            z = [z[h] - tz[h][c:] for h in hs]
        else:
            t = [t[h] - _mm(t[h], wl[h]) for h in hs]
    uw = [_mm(t[h], jnp.concatenate([v[h] * beta[h], kb[h] * e_g[h]], axis=1)) for h in hs]
    st = [st_ref[h] for h in hs]
    ws_qs = [_mm(jnp.concatenate([uw[h][:, HEAD_DIM:], q[h] * e_g[h]], axis=0), st[h]) for h in hs]
    v_new = [uw[h][:, :HEAD_DIM] - ws_qs[h][:c] for h in hs]
    o = [ws_qs[h][c:] + _mm(qk[h], v_new[h]) for h in hs]
    for h in hs:
        st_ref[h] = st[h] * jnp.exp(gc_last[h]) + _mm_tn(k[h] * jnp.exp(gc_last[h] - gc_i[h]), v_new[h])
    for h in hs:
        on = o[h] * lax.rsqrt(jnp.mean(o[h] * o[h], axis=-1, keepdims=True) + NORM_EPS) * nw_ref[...]
        o_ref[:, cols[h]] = (on * _silu(gate_ref[:, cols[h]])).astype(o_ref.dtype)


def _delta_rule(qkv, conv_w, gate, g_col, g_row, norm_w, batch, seq, heads):
    T = batch * seq
    c = DN_CHUNK
    group = _tile(heads, DN_GROUP, 1)
    ng = heads // group
    nc = seq // c
    gw = group * HEAD_DIM
    taps = conv_w.shape[0]
    blk = lambda off: pl.BlockSpec((c, gw), lambda b, g, n: (b * nc + n, off * ng + g))
    cw = lambda off: pl.BlockSpec((taps, gw), lambda b, g, n: (0, off * ng + g))
    kern = functools.partial(_dn_kernel, group=group, heads=heads)
    return pl.pallas_call(
        kern,
        grid=(batch, ng, nc),
        in_specs=[blk(0), blk(1), blk(2), cw(0), cw(1), cw(2),
                  pl.BlockSpec((c, gw), lambda b, g, n: (b * nc + n, g)),
                  pl.BlockSpec((c, 128), lambda b, g, n: (b * nc + n, 0)),
                  pl.BlockSpec((128, c), lambda b, g, n: (0, b * nc + n)),
                  pl.BlockSpec((1, HEAD_DIM), lambda b, g, n: (0, 0))],
        out_specs=pl.BlockSpec((c, gw), lambda b, g, n: (b * nc + n, g)),
        out_shape=jax.ShapeDtypeStruct((T, heads * HEAD_DIM), MXU_DTYPE),
        scratch_shapes=[pltpu.VMEM((group, HEAD_DIM, HEAD_DIM), F32),
                        pltpu.VMEM((3, 8, gw), F32)],
        compiler_params=_cparams(("parallel", "parallel", "arbitrary")),
        name="delta_rule",
    )(qkv, qkv, qkv, conv_w, conv_w, conv_w, gate, g_col, g_row, norm_w.reshape(1, HEAD_DIM))


def _rope_tables(seq):
    pos = jnp.arange(seq, dtype=F32)
    inv_freq = 1.0 / (ROPE_BASE ** jnp.linspace(0.0, 1.0, HEAD_DIM // 2, dtype=F32))
    ang = pos[:, None] * inv_freq[None, :]
    cos, sin = jnp.cos(ang), jnp.sin(ang)
    return jnp.repeat(cos, 2, axis=1), jnp.stack([-sin, sin], axis=-1).reshape(seq, HEAD_DIM)


def _mixer_retention_swa(x, w_in, gn_w, w_out, lnw, lnb, alpha, batch, seq, ffn_cast, early_cast):
    D = x.shape[1]
    width = D // 2
    heads = width // HEAD_DIM
    proj_ret, wgu = _proj(x, w_in, 0, 4 * width, F32, cast=ffn_cast[:1])
    proj_att, wd, *early = _proj(x, w_in, 4 * width, 3 * width, MXU_DTYPE, cast=ffn_cast[1:] + early_cast)
    cos2, sin2 = _rope_tables(seq)
    log_g = jnp.log1p(-jnp.exp2(-5.0 - jnp.arange(heads, dtype=F32)))
    log_g = jnp.broadcast_to(log_g[:, None, None], (heads, 1, HEAD_DIM))
    y_ret = _retention(proj_ret, cos2, sin2, log_g, gn_w, batch, seq, heads)
    y_att = _swa(proj_att, batch, seq, heads)
    return _proj_ln([y_ret, y_att], w_out, x, lnw, lnb, alpha), (wgu, wd), early


def _mixer_gated_deltanet(x, w_in, w_gates, conv_w, a_log, dt_bias, norm_w, w_out, lnw, lnb, alpha, batch, seq,
                          ffn_cast):
    D = x.shape[1]
    heads = D // HEAD_DIM
    qkv, wgu = _proj(x, w_in, 0, 3 * D, F32, cast=ffn_cast[:1])
    gate, wd = _proj(x, w_in, 3 * D, D, F32, cast=ffn_cast[1:])
    g_col, g_row = _dn_gates(x, w_gates, a_log, dt_bias, heads)
    o = _delta_rule(qkv, conv_w, gate, g_col, g_row, norm_w, batch, seq, heads)
    return _proj_ln([o], w_out, x, lnw, lnb, alpha), (wgu, wd)


def kernel(x, ffn_w_gate, ffn_w_up, ffn_w_down, ln_w, ln_b, ab_w_in, ab_gn_w, ab_w_out,
           dn_w_in, dn_conv_w, dn_a_log, dn_dt_bias, dn_norm_w, dn_w_out):
    batch, seq, D = x.shape
    depth = ffn_w_gate.shape[0]
    alpha = (2.0 * depth) ** 0.25
    h = x.reshape(batch * seq, D)
    ffn_items = lambda at: [((ffn_w_gate, ffn_w_up), at), ((ffn_w_down,), at)]
    pre = [_cast_weight(*item) for item in ffn_items((0, 0))]
    for l in range(depth):
        h = _ffn_ln(h, *pre, ln_w[l, 0], ln_b[l, 0], alpha)
        i = l // 2
        next_at = (l + 1, 0) if l + 1 < depth else None
        early = []
        if l % 2 == 0:
            early_cast = ffn_items(next_at)[1:] if next_at else []
            h, post, early = _mixer_retention_swa(h, ab_w_in[i].astype(MXU_DTYPE), ab_gn_w[i],
                                                  ab_w_out[i].astype(MXU_DTYPE), ln_w[l, 1], ln_b[l, 1], alpha,
                                                  batch, seq, ffn_items((l, 1)), early_cast)
        else:
            h, post = _mixer_gated_deltanet(h, dn_w_in[i].astype(MXU_DTYPE), dn_w_in[i][:, 4 * D:], dn_conv_w[i],
                                            dn_a_log[i], dn_dt_bias[i], dn_norm_w[i], dn_w_out[i].astype(MXU_DTYPE),
                                            ln_w[l, 1], ln_b[l, 1], alpha, batch, seq, ffn_items((l, 1)))
        h = _ffn_ln(h, *post, ln_w[l, 2], ln_b[l, 2], alpha)
        if next_at:
            gate_up, down = ffn_items(next_at)
            pre = [_cast_weight(*gate_up), early[0] if early else _cast_weight(*down)]
    return h.reshape(batch, seq, D)
```

```python
import functools
import math

import jax
import jax.numpy as jnp
from jax import lax
from jax.experimental import pallas as pl
from jax.experimental.pallas import tpu as pltpu

F32 = jnp.float32
MXU_DTYPE = jnp.bfloat16
HEAD_DIM = 128
LN_EPS = 1e-5
NORM_EPS = 1e-6
ROPE_BASE = 10000.0
SWA_PATTERNS = ((128, 1), (512, 4), (2048, 16))
SWA_FAR = max(w for w, _ in SWA_PATTERNS)
SWA_ROWS = 128
FFN_TILE = 256
DN_CHUNK = 128
LN_ROWS = 128
DN_GROUP = 8
RET_CHUNK = 128
MASKED = -1e30
V7X_VMEM_LIMIT_BYTES = 56 * 1024 * 1024


def _cparams(sem):
    return pltpu.CompilerParams(dimension_semantics=sem, vmem_limit_bytes=V7X_VMEM_LIMIT_BYTES)


def _mm(a, b):
    return jnp.dot(a.astype(MXU_DTYPE), b.astype(MXU_DTYPE), preferred_element_type=F32)


def _mm_nt(a, b):
    return lax.dot_general(a.astype(MXU_DTYPE), b.astype(MXU_DTYPE), (((1,), (1,)), ((), ())),
                           preferred_element_type=F32)


def _mm_tn(a, b):
    return lax.dot_general(a.astype(MXU_DTYPE), b.astype(MXU_DTYPE), (((0,), (0,)), ((), ())),
                           preferred_element_type=F32)


def _silu(x):
    return x * jax.nn.sigmoid(x)


def _layer_norm(y, w, b):
    mu = jnp.mean(y, axis=-1, keepdims=True)
    yc = y - mu
    var = jnp.mean(yc * yc, axis=-1, keepdims=True)
    return yc * lax.rsqrt(var + LN_EPS) * w + b


def _tile(n, pref, unit):
    if n <= pref:
        return n
    t = (pref // unit) * unit
    while t > unit and n % t:
        t -= unit
    assert n % t == 0, (n, pref, unit)
    return t


def _residual_ln_rows(x_ref, o_ref, ln_ref, alpha, rows):
    tm, d = o_ref.shape
    slabs = [slice(c * 128, (c + 1) * 128) for c in range(d // 128)]

    def body(r, carry):
        sl = pl.ds(pl.multiple_of(r * rows, rows), rows)
        s1 = jnp.zeros((rows, 128), F32)
        for cs in slabs:
            y = alpha * x_ref[sl, cs] + o_ref[sl, cs]
            o_ref[sl, cs] = y
            s1 = s1 + y
        mu = jnp.sum(s1, axis=-1, keepdims=True) * (1.0 / d)
        s2 = jnp.zeros((rows, 128), F32)
        for cs in slabs:
            yc = o_ref[sl, cs] - mu
            s2 = s2 + yc * yc
        rstd = lax.rsqrt(jnp.sum(s2, axis=-1, keepdims=True) * (1.0 / d) + LN_EPS)
        for cs in slabs:
            o_ref[sl, cs] = (o_ref[sl, cs] - mu) * rstd * ln_ref[0:1, cs] + ln_ref[1:2, cs]
        return carry

    lax.fori_loop(0, tm // rows, body, 0)


def _ffn_kernel(x_ref, wgu_ref, wd_ref, ln_ref, o_ref, xb_ref, *, alpha, n_split, rows):
    j = pl.program_id(1)
    tf = wd_ref.shape[0]

    @pl.when(j == 0)
    def _():
        xb_ref[...] = x_ref[...].astype(xb_ref.dtype)
        o_ref[...] = jnp.zeros_like(o_ref)

    gu = jnp.dot(xb_ref[...], wgu_ref[...], preferred_element_type=F32)
    h = (_silu(gu[:, :tf]) * gu[:, tf:] * 0.5).astype(xb_ref.dtype)
    dn = o_ref.shape[1] // n_split
    for n in range(n_split):
        cols = slice(n * dn, (n + 1) * dn)
        o_ref[:, cols] += jnp.dot(h, wd_ref[:, cols], preferred_element_type=F32)

    @pl.when(j == pl.num_programs(1) - 1)
    def _():
        _residual_ln_rows(x_ref, o_ref, ln_ref, alpha, rows)


def _ffn_ln(x, wgu, wd, lnw, lnb, alpha):
    T, D = x.shape
    nf, _, tf2 = wgu.shape
    tf = tf2 // 2
    tm = _tile(T, 1024, 8)
    kern = functools.partial(_ffn_kernel, alpha=alpha, n_split=max(D // 512, 1), rows=LN_ROWS)
    once = pl.Buffered(1)
    return pl.pallas_call(
        kern,
        grid=(T // tm, nf),
        in_specs=[
            pl.BlockSpec((tm, D), lambda i, j: (i, 0), pipeline_mode=once),
            pl.BlockSpec((None, D, tf2), lambda i, j: (j, 0, 0)),
            pl.BlockSpec((tf, D), lambda i, j: (j, 0)),
            pl.BlockSpec((2, D), lambda i, j: (0, 0)),
        ],
        out_specs=pl.BlockSpec((tm, D), lambda i, j: (i, 0), pipeline_mode=once),
        out_shape=jax.ShapeDtypeStruct((T, D), F32),
        scratch_shapes=[pltpu.VMEM((tm, D), MXU_DTYPE)],
        compiler_params=_cparams(("parallel", "arbitrary")),
        name="ffn_ln",
    )(x, wgu, wd, jnp.stack([lnw, lnb]))


def _cast_jobs(cast, n_steps, step):
    in_specs, out_specs, out_shapes = [], [], []
    for arrays, prefix in cast:
        rows, cols = arrays[0].shape[-2:]
        units = rows // 16
        nrb = max(d for d in range(1, min(units, n_steps) + 1) if units % d == 0)
        br = rows // nrb
        blk = lambda *ids, nrb=nrb: jnp.minimum(step(*ids), nrb - 1)
        for _ in arrays:
            in_specs.append(pl.BlockSpec((None,) * len(prefix) + (br, cols),
                                         lambda *ids, prefix=prefix, blk=blk: (*prefix, blk(*ids), 0)))
        if len(arrays) == 1:
            out_specs.append(pl.BlockSpec((br, cols), lambda *ids, blk=blk: (blk(*ids), 0)))
            out_shapes.append(jax.ShapeDtypeStruct((rows, cols), MXU_DTYPE))
        else:
            nf = cols // FFN_TILE
            out_specs.append(pl.BlockSpec((nf, br, 2 * FFN_TILE), lambda *ids, blk=blk: (0, blk(*ids), 0)))
            out_shapes.append(jax.ShapeDtypeStruct((nf, rows, 2 * FFN_TILE), MXU_DTYPE))
    return in_specs, out_specs, out_shapes


def _run_casts(src_refs, dst_refs, arity):
    s = 0
    for dst_ref, n_src in zip(dst_refs, arity):
        if n_src == 1:
            dst_ref[...] = src_refs[s][...].astype(dst_ref.dtype)
        else:
            gate_ref, up_ref = src_refs[s], src_refs[s + 1]
            for j in range(dst_ref.shape[0]):
                cols = slice(j * FFN_TILE, (j + 1) * FFN_TILE)
                dst_ref[j, :, :FFN_TILE] = gate_ref[:, cols].astype(dst_ref.dtype)
                dst_ref[j, :, FFN_TILE:] = up_ref[:, cols].astype(dst_ref.dtype)
        s += n_src


def _cast_kernel(*refs, arity):
    _run_casts(refs[:sum(arity)], refs[sum(arity):], arity)


def _cast_weight(arrays, prefix):
    rows, cols = arrays[0].shape[-2:]
    n_steps = max(1, (rows * cols * 4) // (2 << 20))
    c_in, c_out, c_shape = _cast_jobs([(arrays, prefix)], n_steps, lambda r: r)
    nrb = rows // c_out[0].block_shape[-2]
    return pl.pallas_call(
        functools.partial(_cast_kernel, arity=(len(arrays),)),
        grid=(nrb,),
        in_specs=c_in,
        out_specs=c_out,
        out_shape=c_shape,
        compiler_params=_cparams(("parallel",)),
        name="cast_weight",
    )(*arrays)[0]


def _proj_kernel(*refs, arity):
    n_src = sum(arity)
    x_ref, w_ref = refs[:2]
    o_ref = refs[2 + n_src]
    o_ref[...] = jnp.dot(x_ref[...].astype(w_ref.dtype), w_ref[...], preferred_element_type=F32).astype(o_ref.dtype)
    _run_casts(refs[2:2 + n_src], refs[3 + n_src:], arity)


def _proj(x, w, col0, ncols, out_dtype, cast=()):
    T, K = x.shape
    tm = _tile(T, 512, 8)
    tn = _tile(math.gcd(ncols, col0) if col0 else ncols, 2048, 128)
    j0 = col0 // tn
    n_i = T // tm
    c_in, c_out, c_shape = _cast_jobs(cast, (ncols // tn) * n_i, lambda j, i: j * n_i + i)
    return pl.pallas_call(
        functools.partial(_proj_kernel, arity=tuple(len(arrays) for arrays, _ in cast)),
        grid=(ncols // tn, n_i),
        in_specs=[pl.BlockSpec((tm, K), lambda j, i: (i, 0)),
                  pl.BlockSpec((K, tn), lambda j, i: (0, j0 + j), pipeline_mode=pl.Buffered(1))] + c_in,
        out_specs=[pl.BlockSpec((tm, tn), lambda j, i: (i, j))] + c_out,
        out_shape=[jax.ShapeDtypeStruct((T, ncols), out_dtype)] + c_shape,
        compiler_params=_cparams(("arbitrary", "arbitrary")),
        name="in_proj",
    )(x, w, *(a for arrays, _ in cast for a in arrays))


def _proj_ln_kernel(*refs, alpha, n_parts, n_split, rows):
    a_refs = refs[:n_parts]
    w_ref, x_ref, ln_ref, o_ref = refs[n_parts:]
    dn = o_ref.shape[1] // n_split
    for n in range(n_split):
        cols = slice(n * dn, (n + 1) * dn)
        k0 = 0
        acc = None
        for a_ref in a_refs:
            kp = a_ref.shape[1]
            part = jnp.dot(a_ref[...], w_ref[k0:k0 + kp, cols], preferred_element_type=F32)
            acc = part if acc is None else acc + part
            k0 += kp
        o_ref[:, cols] = acc
    _residual_ln_rows(x_ref, o_ref, ln_ref, alpha, rows)


def _proj_ln(parts, w, x, lnw, lnb, alpha):
    T = x.shape[0]
    K, D = w.shape
    assert sum(p.shape[1] for p in parts) == K
    tm = _tile(T, 256, 8)
    kern = functools.partial(_proj_ln_kernel, alpha=alpha, n_parts=len(parts), n_split=max(D // 512, 1),
                             rows=LN_ROWS)
    return pl.pallas_call(
        kern,
        grid=(T // tm,),
        in_specs=[pl.BlockSpec((tm, p.shape[1]), lambda i: (i, 0)) for p in parts] + [
            pl.BlockSpec((K, D), lambda i: (0, 0), pipeline_mode=pl.Buffered(1)),
            pl.BlockSpec((tm, D), lambda i: (i, 0)),
            pl.BlockSpec((2, D), lambda i: (0, 0)),
        ],
        out_specs=pl.BlockSpec((tm, D), lambda i: (i, 0)),
        out_shape=jax.ShapeDtypeStruct((T, D), F32),
        compiler_params=_cparams(("parallel",)),
        name="out_proj_ln",
    )(*parts, w, x, jnp.stack([lnw, lnb]))


def _retention_kernel(q_ref, k_ref, v_ref, g_ref, cos_ref, sin_ref, lg_ref, gnw_ref, o_ref, st_ref, *, n_chunks):
    c = RET_CHUNK

    @pl.when(pl.program_id(2) == 0)
    def _():
        st_ref[...] = jnp.zeros_like(st_ref)

    lg = lg_ref[0]
    ii = lax.broadcasted_iota(jnp.int32, (c, c), 0)
    jj = lax.broadcasted_iota(jnp.int32, (c, c), 1)
    rel = (ii - jj).astype(F32)
    dmask = jnp.where(rel >= 0, jnp.exp(lg * jnp.maximum(rel, 0.0)), 0.0)
    row = ii.astype(F32)
    zeta = jnp.exp(lg * (c - 1 - row))
    xi = jnp.exp(lg * (row + 1.0))
    chunk_decay = jnp.exp(lg * c)
    even_lane = (jj & 1) == 0

    def rotate(t, cs, sn):
        partner = jnp.where(even_lane, pltpu.roll(t, HEAD_DIM - 1, 1), pltpu.roll(t, 1, 1))
        return t * cs + partner * sn

    chunks = [slice(n * c, (n + 1) * c) for n in range(n_chunks)]
    qr = [rotate(q_ref[sl, :], cos_ref[sl, :], sin_ref[sl, :]) for sl in chunks]
    kr = [rotate(k_ref[sl, :], cos_ref[sl, :], sin_ref[sl, :]) * (HEAD_DIM ** -0.5) for sl in chunks]
    intra = [_mm(_mm_nt(qr[n], kr[n]) * dmask, v_ref[sl, :]) for n, sl in enumerate(chunks)]
    kv = [_mm_tn(kr[n] * zeta, v_ref[sl, :]) for n, sl in enumerate(chunks)]
    st = st_ref[...]
    for n, sl in enumerate(chunks):
        y = intra[n] + _mm(qr[n], st) * xi
        st = st * chunk_decay + kv[n]
        mu = jnp.mean(y, axis=-1, keepdims=True)
        yc = y - mu
        var = jnp.mean(yc * yc, axis=-1, keepdims=True)
        yn = yc * lax.rsqrt(var + LN_EPS) * gnw_ref[...]
        o_ref[sl, :] = (yn * _silu(g_ref[sl, :])).astype(o_ref.dtype)
    st_ref[...] = st


def _retention(proj, cos2, sin2, log_g, gn_w, batch, seq, heads):
    T = batch * seq
    ts = _tile(seq, 1024, RET_CHUNK)
    ns = seq // ts
    blk = lambda off: pl.BlockSpec((ts, HEAD_DIM), lambda b, h, s: (b * ns + s, off * heads + h))
    tab = pl.BlockSpec((ts, HEAD_DIM), lambda b, h, s: (s, 0))
    kern = functools.partial(_retention_kernel, n_chunks=ts // RET_CHUNK)
    return pl.pallas_call(
        kern,
        grid=(batch, heads, ns),
        in_specs=[blk(0), blk(1), blk(2), blk(3), tab, tab,
                  pl.BlockSpec((1, 1, HEAD_DIM), lambda b, h, s: (h, 0, 0)),
                  pl.BlockSpec((1, HEAD_DIM), lambda b, h, s: (0, h))],
        out_specs=pl.BlockSpec((ts, HEAD_DIM), lambda b, h, s: (b * ns + s, h)),
        out_shape=jax.ShapeDtypeStruct((T, heads * HEAD_DIM), MXU_DTYPE),
        scratch_shapes=[pltpu.VMEM((HEAD_DIM, HEAD_DIM), F32)],
        compiler_params=_cparams(("parallel", "parallel", "arbitrary")),
        name="retention",
    )(proj, proj, proj, proj, cos2, sin2, log_g, gn_w.reshape(1, heads * HEAD_DIM))


def _swa_kernel(q_ref, k_ref, v_ref, bias_ref, o_ref, *, tq, span):
    qi = pl.program_id(2)
    start = jnp.maximum(qi * tq - SWA_FAR, 0)
    clamped = qi * tq < SWA_FAR
    groups = range(tq // SWA_ROWS)
    rows = [slice(r * SWA_ROWS, (r + 1) * SWA_ROWS) for r in groups]
    first = [pl.multiple_of(start + jnp.where(clamped, 0, r * SWA_ROWS), SWA_ROWS) for r in groups]
    s = [_mm_nt(q_ref[rows[r], :], k_ref[pl.ds(first[r], span), :]) * (HEAD_DIM ** -0.5) + bias_ref[0, rows[r], :]
         for r in groups]
    p = [jnp.exp(si - jnp.max(si, axis=-1, keepdims=True)) for si in s]
    l = [jnp.sum(pi, axis=-1, keepdims=True) for pi in p]
    for r in groups:
        o_ref[rows[r], :] = (_mm(p[r], v_ref[pl.ds(first[r], span), :]) / l[r]).astype(o_ref.dtype)


def _swa_bias(tq, span):
    nvar = SWA_FAR // tq + 1
    v = lax.broadcasted_iota(jnp.int32, (nvar, tq, span), 0)
    i = lax.broadcasted_iota(jnp.int32, (nvar, tq, span), 1)
    c = lax.broadcasted_iota(jnp.int32, (nvar, tq, span), 2)
    first = jnp.where(v == 0, (i // SWA_ROWS) * SWA_ROWS, 0)
    dist = SWA_FAR - v * tq + i - (first + c)
    mult = jnp.zeros((nvar, tq, span), F32)
    for window, dilation in SWA_PATTERNS:
        mult += ((dist >= 0) & (dist <= window) & (dist % dilation == 0)).astype(F32)
    return jnp.where(mult > 0, jnp.log(jnp.maximum(mult, 1.0)), MASKED)


def _swa(proj, batch, seq, heads):
    T = batch * seq
    tq = _tile(seq, 512, SWA_ROWS)
    assert SWA_FAR % tq == 0 and seq >= SWA_FAR + tq
    span = SWA_FAR + SWA_ROWS
    nq = seq // tq
    nfar = SWA_FAR // tq
    kern = functools.partial(_swa_kernel, tq=tq, span=span)
    return pl.pallas_call(
        kern,
        grid=(batch, heads, nq),
        in_specs=[
            pl.BlockSpec((tq, HEAD_DIM), lambda b, h, i: (b * nq + i, h)),
            pl.BlockSpec((seq, HEAD_DIM), lambda b, h, i: (b, heads + h)),
            pl.BlockSpec((seq, HEAD_DIM), lambda b, h, i: (b, 2 * heads + h)),
            pl.BlockSpec((1, tq, span), lambda b, h, i: (jnp.maximum(nfar - i, 0), 0, 0)),
        ],
        out_specs=pl.BlockSpec((tq, HEAD_DIM), lambda b, h, i: (b * nq + i, h)),
        out_shape=jax.ShapeDtypeStruct((T, heads * HEAD_DIM), MXU_DTYPE),
        compiler_params=_cparams(("parallel", "parallel", "arbitrary")),
        name="dilated_swa",
    )(proj, proj, proj, _swa_bias(tq, span))


def _softplus(x):
    return jnp.maximum(x, 0.0) + jnp.log1p(jnp.exp(-jnp.abs(x)))


def _dn_gates_kernel(x_ref, w_ref, pc_ref, col_ref, row_ref, *, heads):
    c = DN_CHUNK
    xb = x_ref[...].astype(MXU_DTYPE)
    col = jnp.dot(xb, w_ref[...], preferred_element_type=F32)
    ii = lax.broadcasted_iota(jnp.int32, (c, c), 0)
    jj = lax.broadcasted_iota(jnp.int32, (c, c), 1)
    incl = (ii >= jj).astype(F32)
    g_col = -jnp.exp(pc_ref[0:1, :]) * _softplus(col + pc_ref[1:2, :])
    gc_col = jnp.dot(incl, g_col, preferred_element_type=F32, precision=lax.Precision.HIGHEST)
    lane = lax.broadcasted_iota(jnp.int32, col.shape, 1)
    out = jnp.where(lane < heads, gc_col, jax.nn.sigmoid(col))
    col_ref[...] = out
    row_ref[...] = out.T


def _dn_gates(x, w_small, a_log, dt_bias, heads):
    T, D = x.shape
    assert 2 * heads <= 128
    pad = 128 - 2 * heads
    w = jnp.pad(w_small, ((0, 0), (0, pad))).astype(MXU_DTYPE)
    zeros = jnp.zeros((128 - heads,), F32)
    params = jnp.stack([jnp.concatenate([a_log.astype(F32), zeros]), jnp.concatenate([dt_bias.astype(F32), zeros])])
    c = DN_CHUNK
    kern = functools.partial(_dn_gates_kernel, heads=heads)
    return pl.pallas_call(
        kern,
        grid=(T // c,),
        in_specs=[
            pl.BlockSpec((c, D), lambda i: (i, 0)),
            pl.BlockSpec((D, 128), lambda i: (0, 0)),
            pl.BlockSpec((2, 128), lambda i: (0, 0)),
        ],
        out_specs=[pl.BlockSpec((c, 128), lambda i: (i, 0)), pl.BlockSpec((128, c), lambda i: (0, i))],
        out_shape=[jax.ShapeDtypeStruct((T, 128), F32), jax.ShapeDtypeStruct((128, T), F32)],
        compiler_params=_cparams(("parallel",)),
        name="dn_gates",
    )(x, w, params)


def _conv_silu(x, prev, w):
    c = x.shape[0]
    taps = w.shape[0]
    xp = jnp.concatenate([prev, x], axis=0)
    acc = x * w[taps - 1:taps, :]
    for i in range(taps - 1):
        off = 8 - (taps - 1) + i
        acc = acc + xp[off:off + c, :] * w[i:i + 1, :]
    return _silu(acc)


def _l2norm(t):
    return t * lax.rsqrt(jnp.sum(t * t, axis=-1, keepdims=True) + NORM_EPS)


def _dn_kernel(q_ref, k_ref, v_ref, cwq_ref, cwk_ref, cwv_ref, gate_ref, col_ref, row_ref, nw_ref, o_ref,
               st_ref, tail_ref, *, group, heads):
    c = DN_CHUNK
    hs = range(group)

    @pl.when(pl.program_id(2) == 0)
    def _():
        st_ref[...] = jnp.zeros_like(st_ref)
        tail_ref[...] = jnp.zeros_like(tail_ref)

    ii = lax.broadcasted_iota(jnp.int32, (c, c), 0)
    jj = lax.broadcasted_iota(jnp.int32, (c, c), 1)
    eye = (ii == jj).astype(F32)
    n_levels = int(math.log2(c))
    level = [(ii >> 1) == (jj >> 1)] + [
        ((ii >> (ls + 1)) == (jj >> (ls + 1))) & (((ii >> ls) & 1) == 1) & (((jj >> ls) & 1) == 0)
        for ls in range(1, n_levels)]
    col = col_ref[...]
    lane = lax.broadcasted_iota(jnp.int32, col.shape, 1)
    cols = [slice(h * HEAD_DIM, (h + 1) * HEAD_DIM) for h in hs]
    head = [pl.program_id(1) * group + h for h in hs]
    q = [_l2norm(_conv_silu(q_ref[:, cols[h]], tail_ref[0, :, cols[h]], cwq_ref[:, cols[h]])) * (HEAD_DIM ** -0.5)
         for h in hs]
    k = [_l2norm(_conv_silu(k_ref[:, cols[h]], tail_ref[1, :, cols[h]], cwk_ref[:, cols[h]])) for h in hs]
    v = [_conv_silu(v_ref[:, cols[h]], tail_ref[2, :, cols[h]], cwv_ref[:, cols[h]]) for h in hs]
    for i, ref in enumerate((q_ref, k_ref, v_ref)):
        tail_ref[i] = ref[c - 8:c, :]
    gc_i =[jnp.sum(jnp.where(lane == head[h], col, 0.0), axis=-1, keepdims=True) for h in hs]
    beta = [jnp.sum(jnp.where(lane == head[h] + heads, col, 0.0), axis=-1, keepdims=True) for h in hs]
    gc_j = [row_ref[pl.ds(head[h], 1), :] for h in hs]
    gc_last = [gc_i[h][c - 1:c, :] for h in hs]
    decay = [jnp.exp(jnp.where(ii >= jj, gc_i[h] - gc_j[h], MASKED)) for h in hs]
    e_g = [jnp.exp(gc_i[h]) for h in hs]
    kb = [k[h] * beta[h] for h in hs]
    kq = [_mm_nt(jnp.concatenate([kb[h], q[h]], axis=0), k[h]) for h in hs]
    a = [jnp.where(ii > jj, kq[h][:c] * decay[h], 0.0) for h in hs]
    qk = [kq[h][c:] * decay[h] for h in hs]
    w0 = [jnp.where(level[0], a[h], 0.0) for h in hs]
    t = [eye - w0[h] for h in hs]
    z = [a[h] - _mm(a[h], w0[h]) for h in hs]
    for ls in range(1, n_levels):
        wl = [jnp.where(level[ls], z[h], 0.0) for h in hs]
        if ls < n_levels - 1:
            tz = [_mm(jnp.concatenate([t[h], z[h]], axis=0), wl[h]) for h in hs]
            t = [t[h] - tz[h][:c] for h in hs]
            z = [z[h] - tz[h][c:] for h in hs]
        else:
            t = [t[h] - _mm(t[h], wl[h]) for h in hs]
    uw = [_mm(t[h], jnp.concatenate([v[h] * beta[h], kb[h] * e_g[h]], axis=1)) for h in hs]
    st = [st_ref[h] for h in hs]
    ws_qs = [_mm(jnp.concatenate([uw[h][:, HEAD_DIM:], q[h] * e_g[h]], axis=0), st[h]) for h in hs]
    v_new = [uw[h][:, :HEAD_DIM] - ws_qs[h][:c] for h in hs]
    o = [ws_qs[h][c:] + _mm(qk[h], v_new[h]) for h in hs]
    for h in hs:
        st_ref[h] = st[h] * jnp.exp(gc_last[h]) + _mm_tn(k[h] * jnp.exp(gc_last[h] - gc_i[h]), v_new[h])
    for h in hs:
        on = o[h] * lax.rsqrt(jnp.mean(o[h] * o[h], axis=-1, keepdims=True) + NORM_EPS) * nw_ref[...]
        o_ref[:, cols[h]] = (on * _silu(gate_ref[:, cols[h]])).astype(o_ref.dtype)


def _delta_rule(qkv, conv_w, gate, g_col, g_row, norm_w, batch, seq, heads):
    T = batch * seq
    c = DN_CHUNK
    group = _tile(heads, DN_GROUP, 1)
    ng = heads // group
    nc = seq // c
    gw = group * HEAD_DIM
    taps = conv_w.shape[0]
    blk = lambda off: pl.BlockSpec((c, gw), lambda b, g, n: (b * nc + n, off * ng + g))
    cw = lambda off: pl.BlockSpec((taps, gw), lambda b, g, n: (0, off * ng + g))
    kern = functools.partial(_dn_kernel, group=group, heads=heads)
    return pl.pallas_call(
        kern,
        grid=(batch, ng, nc),
        in_specs=[blk(0), blk(1), blk(2), cw(0), cw(1), cw(2),
                  pl.BlockSpec((c, gw), lambda b, g, n: (b * nc + n, g)),
                  pl.BlockSpec((c, 128), lambda b, g, n: (b * nc + n, 0)),
                  pl.BlockSpec((128, c), lambda b, g, n: (0, b * nc + n)),
                  pl.BlockSpec((1, HEAD_DIM), lambda b, g, n: (0, 0))],
        out_specs=pl.BlockSpec((c, gw), lambda b, g, n: (b * nc + n, g)),
        out_shape=jax.ShapeDtypeStruct((T, heads * HEAD_DIM), MXU_DTYPE),
        scratch_shapes=[pltpu.VMEM((group, HEAD_DIM, HEAD_DIM), F32),
                        pltpu.VMEM((3, 8, gw), F32)],
        compiler_params=_cparams(("parallel", "parallel", "arbitrary")),
        name="delta_rule",
    )(qkv, qkv, qkv, conv_w, conv_w, conv_w, gate, g_col, g_row, norm_w.reshape(1, HEAD_DIM))


def _rope_tables(seq):
    pos = jnp.arange(seq, dtype=F32)
    inv_freq = 1.0 / (ROPE_BASE ** jnp.linspace(0.0, 1.0, HEAD_DIM // 2, dtype=F32))
    ang = pos[:, None] * inv_freq[None, :]
    cos, sin = jnp.cos(ang), jnp.sin(ang)
    return jnp.repeat(cos, 2, axis=1), jnp.stack([-sin, sin], axis=-1).reshape(seq, HEAD_DIM)


def _mixer_retention_swa(x, w_in, gn_w, w_out, lnw, lnb, alpha, batch, seq, ffn_cast, early_cast):
    D = x.shape[1]
    width = D // 2
    heads = width // HEAD_DIM
    proj_ret, wgu = _proj(x, w_in, 0, 4 * width, F32, cast=ffn_cast[:1])
    proj_att, wd, *early = _proj(x, w_in, 4 * width, 3 * width, MXU_DTYPE, cast=ffn_cast[1:] + early_cast)
    cos2, sin2 = _rope_tables(seq)
    log_g = jnp.log1p(-jnp.exp2(-5.0 - jnp.arange(heads, dtype=F32)))
    log_g = jnp.broadcast_to(log_g[:, None, None], (heads, 1, HEAD_DIM))
    y_ret = _retention(proj_ret, cos2, sin2, log_g, gn_w, batch, seq, heads)
    y_att = _swa(proj_att, batch, seq, heads)
    return _proj_ln([y_ret, y_att], w_out, x, lnw, lnb, alpha), (wgu, wd), early


def _mixer_gated_deltanet(x, w_in, w_gates, conv_w, a_log, dt_bias, norm_w, w_out, lnw, lnb, alpha, batch, seq,
                          ffn_cast):
    D = x.shape[1]
    heads = D // HEAD_DIM
    qkv, wgu = _proj(x, w_in, 0, 3 * D, F32, cast=ffn_cast[:1])
    gate, wd = _proj(x, w_in, 3 * D, D, F32, cast=ffn_cast[1:])
    g_col, g_row = _dn_gates(x, w_gates, a_log, dt_bias, heads)
    o = _delta_rule(qkv, conv_w, gate, g_col, g_row, norm_w, batch, seq, heads)
    return _proj_ln([o], w_out, x, lnw, lnb, alpha), (wgu, wd)


def kernel(x, ffn_w_gate, ffn_w_up, ffn_w_down, ln_w, ln_b, ab_w_in, ab_gn_w, ab_w_out,
           dn_w_in, dn_conv_w, dn_a_log, dn_dt_bias, dn_norm_w, dn_w_out):
    batch, seq, D = x.shape
    depth = ffn_w_gate.shape[0]
    alpha = (2.0 * depth) ** 0.25
    h = x.reshape(batch * seq, D)
    ffn_items = lambda at: [((ffn_w_gate, ffn_w_up), at), ((ffn_w_down,), at)]
    pre = [_cast_weight(*item) for item in ffn_items((0, 0))]
    for l in range(depth):
        h = _ffn_ln(h, *pre, ln_w[l, 0], ln_b[l, 0], alpha)
        i = l // 2
        next_at = (l + 1, 0) if l + 1 < depth else None
        early = []
        if l % 2 == 0:
            early_cast = ffn_items(next_at)[1:] if next_at else []
            h, post, early = _mixer_retention_swa(h, ab_w_in[i].astype(MXU_DTYPE), ab_gn_w[i],
                                                  ab_w_out[i].astype(MXU_DTYPE), ln_w[l, 1], ln_b[l, 1], alpha,
                                                  batch, seq, ffn_items((l, 1)), early_cast)
        else:
            h, post = _mixer_gated_deltanet(h, dn_w_in[i].astype(MXU_DTYPE), dn_w_in[i][:, 4 * D:], dn_conv_w[i],
                                            dn_a_log[i], dn_dt_bias[i], dn_norm_w[i], dn_w_out[i].astype(MXU_DTYPE),
                                            ln_w[l, 1], ln_b[l, 1], alpha, batch, seq, ffn_items((l, 1)))
        h = _ffn_ln(h, *post, ln_w[l, 2], ln_b[l, 2], alpha)
        if next_at:
            gate_up, down = ffn_items(next_at)
            pre = [_cast_weight(*gate_up), early[0] if early else _cast_weight(*down)]
    return h.reshape(batch, seq, D)
```

```python
import functools
import math

import jax
import jax.numpy as jnp
from jax import lax
from jax.experimental import pallas as pl
from jax.experimental.pallas import tpu as pltpu

F32 = jnp.float32
MXU_DTYPE = jnp.bfloat16
HEAD_DIM = 128
LN_EPS = 1e-5
NORM_EPS = 1e-6
ROPE_BASE = 10000.0
SWA_PATTERNS = ((128, 1), (512, 4), (2048, 16))
SWA_FAR = max(w for w, _ in SWA_PATTERNS)
SWA_ROWS = 128
FFN_TILE = 256
DN_CHUNK = 128
LN_ROWS = 128
DN_GROUP = 8
RET_CHUNK = 128
MASKED = -1e30
V7X_VMEM_LIMIT_BYTES = 56 * 1024 * 1024


def _cparams(sem):
    return pltpu.CompilerParams(dimension_semantics=sem, vmem_limit_bytes=V7X_VMEM_LIMIT_BYTES)


def _mm(a, b):
    return jnp.dot(a.astype(MXU_DTYPE), b.astype(MXU_DTYPE), preferred_element_type=F32)


def _mm_nt(a, b):
    return lax.dot_general(a.astype(MXU_DTYPE), b.astype(MXU_DTYPE), (((1,), (1,)), ((), ())),
                           preferred_element_type=F32)


def _mm_tn(a, b):
    return lax.dot_general(a.astype(MXU_DTYPE), b.astype(MXU_DTYPE), (((0,), (0,)), ((), ())),
                           preferred_element_type=F32)


def _silu(x):
    return x * jax.nn.sigmoid(x)


def _layer_norm(y, w, b):
    mu = jnp.mean(y, axis=-1, keepdims=True)
    yc = y - mu
    var = jnp.mean(yc * yc, axis=-1, keepdims=True)
    return yc * lax.rsqrt(var + LN_EPS) * w + b


def _tile(n, pref, unit):
    if n <= pref:
        return n
    t = (pref // unit) * unit
    while t > unit and n % t:
        t -= unit
    assert n % t == 0, (n, pref, unit)
    return t


def _residual_ln_rows(x_ref, o_ref, ln_ref, alpha, rows):
    tm, d = o_ref.shape
    slabs = [slice(c * 128, (c + 1) * 128) for c in range(d // 128)]

    def body(r, carry):
        sl = pl.ds(pl.multiple_of(r * rows, rows), rows)
        s1 = jnp.zeros((rows, 128), F32)
        for cs in slabs:
            y = alpha * x_ref[sl, cs] + o_ref[sl, cs]
            o_ref[sl, cs] = y
            s1 = s1 + y
        mu = jnp.sum(s1, axis=-1, keepdims=True) * (1.0 / d)
        s2 = jnp.zeros((rows, 128), F32)
        for cs in slabs:
            yc = o_ref[sl, cs] - mu
            s2 = s2 + yc * yc
        rstd = lax.rsqrt(jnp.sum(s2, axis=-1, keepdims=True) * (1.0 / d) + LN_EPS)
        for cs in slabs:
            o_ref[sl, cs] = (o_ref[sl, cs] - mu) * rstd * ln_ref[0:1, cs] + ln_ref[1:2, cs]
        return carry

    lax.fori_loop(0, tm // rows, body, 0)


def _ffn_kernel(x_ref, wgu_ref, wd_ref, ln_ref, o_ref, xb_ref, *, alpha, n_split, rows):
    j = pl.program_id(1)
    tf = wd_ref.shape[0]

    @pl.when(j == 0)
    def _():
        xb_ref[...] = x_ref[...].astype(xb_ref.dtype)
        o_ref[...] = jnp.zeros_like(o_ref)

    gu = jnp.dot(xb_ref[...], wgu_ref[...], preferred_element_type=F32)
    h = (_silu(gu[:, :tf]) * gu[:, tf:] * 0.5).astype(xb_ref.dtype)
    dn = o_ref.shape[1] // n_split
    for n in range(n_split):
        cols = slice(n * dn, (n + 1) * dn)
        o_ref[:, cols] += jnp.dot(h, wd_ref[:, cols], preferred_element_type=F32)

    @pl.when(j == pl.num_programs(1) - 1)
    def _():
        _residual_ln_rows(x_ref, o_ref, ln_ref, alpha, rows)


def _ffn_ln(x, wgu, wd, lnw, lnb, alpha):
    T, D = x.shape
    nf, _, tf2 = wgu.shape
    tf = tf2 // 2
    tm = _tile(T, 1024, 8)
    kern = functools.partial(_ffn_kernel, alpha=alpha, n_split=max(D // 512, 1), rows=LN_ROWS)
    once = pl.Buffered(1)
    return pl.pallas_call(
        kern,
        grid=(T // tm, nf),
        in_specs=[
            pl.BlockSpec((tm, D), lambda i, j: (i, 0), pipeline_mode=once),
            pl.BlockSpec((None, D, tf2), lambda i, j: (j, 0, 0)),
            pl.BlockSpec((tf, D), lambda i, j: (j, 0)),
            pl.BlockSpec((2, D), lambda i, j: (0, 0)),
        ],
        out_specs=pl.BlockSpec((tm, D), lambda i, j: (i, 0), pipeline_mode=once),
        out_shape=jax.ShapeDtypeStruct((T, D), F32),
        scratch_shapes=[pltpu.VMEM((tm, D), MXU_DTYPE)],
        compiler_params=_cparams(("parallel", "arbitrary")),
        name="ffn_ln",
    )(x, wgu, wd, jnp.stack([lnw, lnb]))


def _cast_jobs(cast, n_steps, step):
    in_specs, out_specs, out_shapes = [], [], []
    for arrays, prefix in cast:
        rows, cols = arrays[0].shape[-2:]
        units = rows // 16
        nrb = max(d for d in range(1, min(units, n_steps) + 1) if units % d == 0)
        br = rows // nrb
        blk = lambda *ids, nrb=nrb: jnp.minimum(step(*ids), nrb - 1)
        for _ in arrays:
            in_specs.append(pl.BlockSpec((None,) * len(prefix) + (br, cols),
                                         lambda *ids, prefix=prefix, blk=blk: (*prefix, blk(*ids), 0)))
        if len(arrays) == 1:
            out_specs.append(pl.BlockSpec((br, cols), lambda *ids, blk=blk: (blk(*ids), 0)))
            out_shapes.append(jax.ShapeDtypeStruct((rows, cols), MXU_DTYPE))
        else:
            nf = cols // FFN_TILE
            out_specs.append(pl.BlockSpec((nf, br, 2 * FFN_TILE), lambda *ids, blk=blk: (0, blk(*ids), 0)))
            out_shapes.append(jax.ShapeDtypeStruct((nf, rows, 2 * FFN_TILE), MXU_DTYPE))
    return in_specs, out_specs, out_shapes


def _run_casts(src_refs, dst_refs, arity):
    s = 0
    for dst_ref, n_src in zip(dst_refs, arity):
        if n_src == 1:
            dst_ref[...] = src_refs[s][...].astype(dst_ref.dtype)
        else:
            gate_ref, up_ref = src_refs[s], src_refs[s + 1]
            for j in range(dst_ref.shape[0]):
                cols = slice(j * FFN_TILE, (j + 1) * FFN_TILE)
                dst_ref[j, :, :FFN_TILE] = gate_ref[:, cols].astype(dst_ref.dtype)
                dst_ref[j, :, FFN_TILE:] = up_ref[:, cols].astype(dst_ref.dtype)
        s += n_src


def _cast_kernel(*refs, arity):
    _run_casts(refs[:sum(arity)], refs[sum(arity):], arity)


def _cast_weight(arrays, prefix):
    rows, cols = arrays[0].shape[-2:]
    n_steps = max(1, (rows * cols * 4) // (2 << 20))
    c_in, c_out, c_shape = _cast_jobs([(arrays, prefix)], n_steps, lambda r: r)
    nrb = rows // c_out[0].block_shape[-2]
    return pl.pallas_call(
        functools.partial(_cast_kernel, arity=(len(arrays),)),
        grid=(nrb,),
        in_specs=c_in,
        out_specs=c_out,
        out_shape=c_shape,
        compiler_params=_cparams(("parallel",)),
        name="cast_weight",
    )(*arrays)[0]


def _proj_kernel(*refs, arity):
    n_src = sum(arity)
    x_ref, w_ref = refs[:2]
    o_ref = refs[2 + n_src]
    o_ref[...] = jnp.dot(x_ref[...].astype(w_ref.dtype), w_ref[...], preferred_element_type=F32).astype(o_ref.dtype)
    _run_casts(refs[2:2 + n_src], refs[3 + n_src:], arity)


def _proj(x, w, col0, ncols, out_dtype, cast=()):
    T, K = x.shape
    tm = _tile(T, 512, 8)
    tn = _tile(math.gcd(ncols, col0) if col0 else ncols, 2048, 128)
    j0 = col0 // tn
    n_i = T // tm
    c_in, c_out, c_shape = _cast_jobs(cast, (ncols // tn) * n_i, lambda j, i: j * n_i + i)
    return pl.pallas_call(
        functools.partial(_proj_kernel, arity=tuple(len(arrays) for arrays, _ in cast)),
        grid=(ncols // tn, n_i),
        in_specs=[pl.BlockSpec((tm, K), lambda j, i: (i, 0)),
                  pl.BlockSpec((K, tn), lambda j, i: (0, j0 + j), pipeline_mode=pl.Buffered(1))] + c_in,
        out_specs=[pl.BlockSpec((tm, tn), lambda j, i: (i, j))] + c_out,
        out_shape=[jax.ShapeDtypeStruct((T, ncols), out_dtype)] + c_shape,
        compiler_params=_cparams(("arbitrary", "arbitrary")),
        name="in_proj",
    )(x, w, *(a for arrays, _ in cast for a in arrays))


def _proj_ln_kernel(*refs, alpha, n_parts, n_split, rows):
    a_refs = refs[:n_parts]
    w_ref, x_ref, ln_ref, o_ref = refs[n_parts:]
    dn = o_ref.shape[1] // n_split
    for n in range(n_split):
        cols = slice(n * dn, (n + 1) * dn)
        k0 = 0
        acc = None
        for a_ref in a_refs:
            kp = a_ref.shape[1]
            part = jnp.dot(a_ref[...], w_ref[k0:k0 + kp, cols], preferred_element_type=F32)
            acc = part if acc is None else acc + part
            k0 += kp
        o_ref[:, cols] = acc
    _residual_ln_rows(x_ref, o_ref, ln_ref, alpha, rows)


def _proj_ln(parts, w, x, lnw, lnb, alpha):
    T = x.shape[0]
    K, D = w.shape
    assert sum(p.shape[1] for p in parts) == K
    tm = _tile(T, 256, 8)
    kern = functools.partial(_proj_ln_kernel, alpha=alpha, n_parts=len(parts), n_split=max(D // 512, 1),
                             rows=LN_ROWS)
    return pl.pallas_call(
        kern,
        grid=(T // tm,),
        in_specs=[pl.BlockSpec((tm, p.shape[1]), lambda i: (i, 0)) for p in parts] + [
            pl.BlockSpec((K, D), lambda i: (0, 0), pipeline_mode=pl.Buffered(1)),
            pl.BlockSpec((tm, D), lambda i: (i, 0)),
            pl.BlockSpec((2, D), lambda i: (0, 0)),
        ],
        out_specs=pl.BlockSpec((tm, D), lambda i: (i, 0)),
        out_shape=jax.ShapeDtypeStruct((T, D), F32),
        compiler_params=_cparams(("parallel",)),
        name="out_proj_ln",
    )(*parts, w, x, jnp.stack([lnw, lnb]))


def _retention_kernel(q_ref, k_ref, v_ref, g_ref, cos_ref, sin_ref, lg_ref, gnw_ref, o_ref, st_ref, *, n_chunks):
    c = RET_CHUNK

    @pl.when(pl.program_id(2) == 0)
    def _():
        st_ref[...] = jnp.zeros_like(st_ref)

    lg = lg_ref[0]
    ii = lax.broadcasted_iota(jnp.int32, (c, c), 0)
    jj = lax.broadcasted_iota(jnp.int32, (c, c), 1)
    rel = (ii - jj).astype(F32)
    dmask = jnp.where(rel >= 0, jnp.exp(lg * jnp.maximum(rel, 0.0)), 0.0)
    row = ii.astype(F32)
    zeta = jnp.exp(lg * (c - 1 - row))
    xi = jnp.exp(lg * (row + 1.0))
    chunk_decay = jnp.exp(lg * c)
    even_lane = (jj & 1) == 0

    def rotate(t, cs, sn):
        partner = jnp.where(even_lane, pltpu.roll(t, HEAD_DIM - 1, 1), pltpu.roll(t, 1, 1))
        return t * cs + partner * sn

    chunks = [slice(n * c, (n + 1) * c) for n in range(n_chunks)]
    qr = [rotate(q_ref[sl, :], cos_ref[sl, :], sin_ref[sl, :]) for sl in chunks]
    kr = [rotate(k_ref[sl, :], cos_ref[sl, :], sin_ref[sl, :]) * (HEAD_DIM ** -0.5) for sl in chunks]
    intra = [_mm(_mm_nt(qr[n], kr[n]) * dmask, v_ref[sl, :]) for n, sl in enumerate(chunks)]
    kv = [_mm_tn(kr[n] * zeta, v_ref[sl, :]) for n, sl in enumerate(chunks)]
    st = st_ref[...]
    for n, sl in enumerate(chunks):
        y = intra[n] + _mm(qr[n], st) * xi
        st = st * chunk_decay + kv[n]
        mu = jnp.mean(y, axis=-1, keepdims=True)
        yc = y - mu
        var = jnp.mean(yc * yc, axis=-1, keepdims=True)
        yn = yc * lax.rsqrt(var + LN_EPS) * gnw_ref[...]
        o_ref[sl, :] = (yn * _silu(g_ref[sl, :])).astype(o_ref.dtype)
    st_ref[...] = st


def _retention(proj, cos2, sin2, log_g, gn_w, batch, seq, heads):
    T = batch * seq
    ts = _tile(seq, 1024, RET_CHUNK)
    ns = seq // ts
    blk = lambda off: pl.BlockSpec((ts, HEAD_DIM), lambda b, h, s: (b * ns + s, off * heads + h))
    tab = pl.BlockSpec((ts, HEAD_DIM), lambda b, h, s: (s, 0))
    kern = functools.partial(_retention_kernel, n_chunks=ts // RET_CHUNK)
    return pl.pallas_call(
        kern,
        grid=(batch, heads, ns),
        in_specs=[blk(0), blk(1), blk(2), blk(3), tab, tab,
                  pl.BlockSpec((1, 1, HEAD_DIM), lambda b, h, s: (h, 0, 0)),
                  pl.BlockSpec((1, HEAD_DIM), lambda b, h, s: (0, h))],
        out_specs=pl.BlockSpec((ts, HEAD_DIM), lambda b, h, s: (b * ns + s, h)),
        out_shape=jax.ShapeDtypeStruct((T, heads * HEAD_DIM), MXU_DTYPE),
        scratch_shapes=[pltpu.VMEM((HEAD_DIM, HEAD_DIM), F32)],
        compiler_params=_cparams(("parallel", "parallel", "arbitrary")),
        name="retention",
    )(proj, proj, proj, proj, cos2, sin2, log_g, gn_w.reshape(1, heads * HEAD_DIM))


def _swa_kernel(q_ref, k_ref, v_ref, bias_ref, o_ref, *, tq, span):
    qi = pl.program_id(2)
    start = jnp.maximum(qi * tq - SWA_FAR, 0)
    clamped = qi * tq < SWA_FAR
    groups = range(tq // SWA_ROWS)
    rows = [slice(r * SWA_ROWS, (r + 1) * SWA_ROWS) for r in groups]
    first = [pl.multiple_of(start + jnp.where(clamped, 0, r * SWA_ROWS), SWA_ROWS) for r in groups]
    s = [_mm_nt(q_ref[rows[r], :], k_ref[pl.ds(first[r], span), :]) * (HEAD_DIM ** -0.5) + bias_ref[0, rows[r], :]
         for r in groups]
    p = [jnp.exp(si - jnp.max(si, axis=-1, keepdims=True)) for si in s]
    l = [jnp.sum(pi, axis=-1, keepdims=True) for pi in p]
    for r in groups:
        o_ref[rows[r], :] = (_mm(p[r], v_ref[pl.ds(first[r], span), :]) / l[r]).astype(o_ref.dtype)


def _swa_bias(tq, span):
    nvar = SWA_FAR // tq + 1
    v = lax.broadcasted_iota(jnp.int32, (nvar, tq, span), 0)
    i = lax.broadcasted_iota(jnp.int32, (nvar, tq, span), 1)
    c = lax.broadcasted_iota(jnp.int32, (nvar, tq, span), 2)
    first = jnp.where(v == 0, (i // SWA_ROWS) * SWA_ROWS, 0)
    dist = SWA_FAR - v * tq + i - (first + c)
    mult = jnp.zeros((nvar, tq, span), F32)
    for window, dilation in SWA_PATTERNS:
        mult += ((dist >= 0) & (dist <= window) & (dist % dilation == 0)).astype(F32)
    return jnp.where(mult > 0, jnp.log(jnp.maximum(mult, 1.0)), MASKED)


def _swa(proj, batch, seq, heads):
    T = batch * seq
    tq = _tile(seq, 1024, SWA_ROWS)
    assert SWA_FAR % tq == 0 and seq >= SWA_FAR + tq
    span = SWA_FAR + SWA_ROWS
    nq = seq // tq
    nfar = SWA_FAR // tq
    kern = functools.partial(_swa_kernel, tq=tq, span=span)
    return pl.pallas_call(
        kern,
        grid=(batch, heads, nq),
        in_specs=[
            pl.BlockSpec((tq, HEAD_DIM), lambda b, h, i: (b * nq + i, h)),
            pl.BlockSpec((seq, HEAD_DIM), lambda b, h, i: (b, heads + h)),
            pl.BlockSpec((seq, HEAD_DIM), lambda b, h, i: (b, 2 * heads + h)),
            pl.BlockSpec((1, tq, span), lambda b, h, i: (jnp.maximum(nfar - i, 0), 0, 0)),
        ],
        out_specs=pl.BlockSpec((tq, HEAD_DIM), lambda b, h, i: (b * nq + i, h)),
        out_shape=jax.ShapeDtypeStruct((T, heads * HEAD_DIM), MXU_DTYPE),
        compiler_params=_cparams(("parallel", "parallel", "arbitrary")),
        name="dilated_swa",
    )(proj, proj, proj, _swa_bias(tq, span))


def _softplus(x):
    return jnp.maximum(x, 0.0) + jnp.log1p(jnp.exp(-jnp.abs(x)))


def _dn_gates_kernel(x_ref, w_ref, pc_ref, col_ref, row_ref, *, heads):
    c = DN_CHUNK
    xb = x_ref[...].astype(MXU_DTYPE)
    col = jnp.dot(xb, w_ref[...], preferred_element_type=F32)
    ii = lax.broadcasted_iota(jnp.int32, (c, c), 0)
    jj = lax.broadcasted_iota(jnp.int32, (c, c), 1)
    incl = (ii >= jj).astype(F32)
    g_col = -jnp.exp(pc_ref[0:1, :]) * _softplus(col + pc_ref[1:2, :])
    gc_col = jnp.dot(incl, g_col, preferred_element_type=F32, precision=lax.Precision.HIGHEST)
    lane = lax.broadcasted_iota(jnp.int32, col.shape, 1)
    out = jnp.where(lane < heads, gc_col, jax.nn.sigmoid(col))
    col_ref[...] = out
    row_ref[...] = out.T


def _dn_gates(x, w_small, a_log, dt_bias, heads):
    T, D = x.shape
    assert 2 * heads <= 128
    pad = 128 - 2 * heads
    w = jnp.pad(w_small, ((0, 0), (0, pad))).astype(MXU_DTYPE)
    zeros = jnp.zeros((128 - heads,), F32)
    params = jnp.stack([jnp.concatenate([a_log.astype(F32), zeros]), jnp.concatenate([dt_bias.astype(F32), zeros])])
    c = DN_CHUNK
    kern = functools.partial(_dn_gates_kernel, heads=heads)
    return pl.pallas_call(
        kern,
        grid=(T // c,),
        in_specs=[
            pl.BlockSpec((c, D), lambda i: (i, 0)),
            pl.BlockSpec((D, 128), lambda i: (0, 0)),
            pl.BlockSpec((2, 128), lambda i: (0, 0)),
        ],
        out_specs=[pl.BlockSpec((c, 128), lambda i: (i, 0)), pl.BlockSpec((128, c), lambda i: (0, i))],
        out_shape=[jax.ShapeDtypeStruct((T, 128), F32), jax.ShapeDtypeStruct((128, T), F32)],
        compiler_params=_cparams(("parallel",)),
        name="dn_gates",
    )(x, w, params)


def _conv_silu(x, prev, w):
    c = x.shape[0]
    taps = w.shape[0]
    xp = jnp.concatenate([prev, x], axis=0)
    acc = x * w[taps - 1:taps, :]
    for i in range(taps - 1):
        off = 8 - (taps - 1) + i
        acc = acc + xp[off:off + c, :] * w[i:i + 1, :]
    return _silu(acc)


def _l2norm(t):
    return t * lax.rsqrt(jnp.sum(t * t, axis=-1, keepdims=True) + NORM_EPS)


def _dn_kernel(q_ref, k_ref, v_ref, cwq_ref, cwk_ref, cwv_ref, gate_ref, col_ref, row_ref, nw_ref, o_ref,
               st_ref, tail_ref, *, group, heads):
    c = DN_CHUNK
    hs = range(group)

    @pl.when(pl.program_id(2) == 0)
    def _():
        st_ref[...] = jnp.zeros_like(st_ref)
        tail_ref[...] = jnp.zeros_like(tail_ref)

    ii = lax.broadcasted_iota(jnp.int32, (c, c), 0)
    jj = lax.broadcasted_iota(jnp.int32, (c, c), 1)
    eye = (ii == jj).astype(F32)
    n_levels = int(math.log2(c))
    level = [(ii >> 1) == (jj >> 1)] + [
        ((ii >> (ls + 1)) == (jj >> (ls + 1))) & (((ii >> ls) & 1) == 1) & (((jj >> ls) & 1) == 0)
        for ls in range(1, n_levels)]
    col = col_ref[...]
    lane = lax.broadcasted_iota(jnp.int32, col.shape, 1)
    cols = [slice(h * HEAD_DIM, (h + 1) * HEAD_DIM) for h in hs]
    head = [pl.program_id(1) * group + h for h in hs]
    q = [_l2norm(_conv_silu(q_ref[:, cols[h]], tail_ref[0, :, cols[h]], cwq_ref[:, cols[h]])) * (HEAD_DIM ** -0.5)
         for h in hs]
    k = [_l2norm(_conv_silu(k_ref[:, cols[h]], tail_ref[1, :, cols[h]], cwk_ref[:, cols[h]])) for h in hs]
    v = [_conv_silu(v_ref[:, cols[h]], tail_ref[2, :, cols[h]], cwv_ref[:, cols[h]]) for h in hs]
    for i, ref in enumerate((q_ref, k_ref, v_ref)):
        tail_ref[i] = ref[c - 8:c, :]
    gc_i =[jnp.sum(jnp.where(lane == head[h], col, 0.0), axis=-1, keepdims=True) for h in hs]
    beta = [jnp.sum(jnp.where(lane == head[h] + heads, col, 0.0), axis=-1, keepdims=True) for h in hs]
    gc_j = [row_ref[pl.ds(head[h], 1), :] for h in hs]
    gc_last = [gc_i[h][c - 1:c, :] for h in hs]
    decay = [jnp.exp(jnp.where(ii >= jj, gc_i[h] - gc_j[h], MASKED)) for h in hs]
    e_g = [jnp.exp(gc_i[h]) for h in hs]
    kb = [k[h] * beta[h] for h in hs]
    kq = [_mm_nt(jnp.concatenate([kb[h], q[h]], axis=0), k[h]) for h in hs]
    a = [jnp.where(ii > jj, kq[h][:c] * decay[h], 0.0) for h in hs]
    qk = [kq[h][c:] * decay[h] for h in hs]
    w0 = [jnp.where(level[0], a[h], 0.0) for h in hs]
    t = [eye - w0[h] for h in hs]
    z = [a[h] - _mm(a[h], w0[h]) for h in hs]
    for ls in range(1, n_levels):
        wl = [jnp.where(level[ls], z[h], 0.0) for h in hs]
        if ls < n_levels - 1:
            tz = [_mm(jnp.concatenate([t[h], z[h]], axis=0), wl[h]) for h in hs]
            t = [t[h] - tz[h][:c] for h in hs]
            z = [z[h] - tz[h][c:] for h in hs]
        else:
            t = [t[h] - _mm(t[h], wl[h]) for h in hs]
    uw = [_mm(t[h], jnp.concatenate([v[h] * beta[h], kb[h] * e_g[h]], axis=1)) for h in hs]
    st = [st_ref[h] for h in hs]
    ws_qs = [_mm(jnp.concatenate([uw[h][:, HEAD_DIM:], q[h] * e_g[h]], axis=0), st[h]) for h in hs]
    v_new = [uw[h][:, :HEAD_DIM] - ws_qs[h][:c] for h in hs]
    o = [ws_qs[h][c:] + _mm(qk[h], v_new[h]) for h in hs]
    for h in hs:
        st_ref[h] = st[h] * jnp.exp(gc_last[h]) + _mm_tn(k[h] * jnp.exp(gc_last[h] - gc_i[h]), v_new[h])
    for h in hs:
        on = o[h] * lax.rsqrt(jnp.mean(o[h] * o[h], axis=-1, keepdims=True) + NORM_EPS) * nw_ref[...]
        o_ref[:, cols[h]] = (on * _silu(gate_ref[:, cols[h]])).astype(o_ref.dtype)


def _delta_rule(qkv, conv_w, gate, g_col, g_row, norm_w, batch, seq, heads):
    T = batch * seq
    c = DN_CHUNK
    group = _tile(heads, DN_GROUP, 1)
    ng = heads // group
    nc = seq // c
    gw = group * HEAD_DIM
    taps = conv_w.shape[0]
    blk = lambda off: pl.BlockSpec((c, gw), lambda b, g, n: (b * nc + n, off * ng + g))
    cw = lambda off: pl.BlockSpec((taps, gw), lambda b, g, n: (0, off * ng + g))
    kern = functools.partial(_dn_kernel, group=group, heads=heads)
    return pl.pallas_call(
        kern,
        grid=(batch, ng, nc),
        in_specs=[blk(0), blk(1), blk(2), cw(0), cw(1), cw(2),
                  pl.BlockSpec((c, gw), lambda b, g, n: (b * nc + n, g)),
                  pl.BlockSpec((c, 128), lambda b, g, n: (b * nc + n, 0)),
                  pl.BlockSpec((128, c), lambda b, g, n: (0, b * nc + n)),
                  pl.BlockSpec((1, HEAD_DIM), lambda b, g, n: (0, 0))],
        out_specs=pl.BlockSpec((c, gw), lambda b, g, n: (b * nc + n, g)),
        out_shape=jax.ShapeDtypeStruct((T, heads * HEAD_DIM), MXU_DTYPE),
        scratch_shapes=[pltpu.VMEM((group, HEAD_DIM, HEAD_DIM), F32),
                        pltpu.VMEM((3, 8, gw), F32)],
        compiler_params=_cparams(("parallel", "parallel", "arbitrary")),
        name="delta_rule",
    )(qkv, qkv, qkv, conv_w, conv_w, conv_w, gate, g_col, g_row, norm_w.reshape(1, HEAD_DIM))


def _rope_tables(seq):
    pos = jnp.arange(seq, dtype=F32)
    inv_freq = 1.0 / (ROPE_BASE ** jnp.linspace(0.0, 1.0, HEAD_DIM // 2, dtype=F32))
    ang = pos[:, None] * inv_freq[None, :]
    cos, sin = jnp.cos(ang), jnp.sin(ang)
    return jnp.repeat(cos, 2, axis=1), jnp.stack([-sin, sin], axis=-1).reshape(seq, HEAD_DIM)


def _mixer_retention_swa(x, w_in, gn_w, w_out, lnw, lnb, alpha, batch, seq, ffn_cast, early_cast):
    D = x.shape[1]
    width = D // 2
    heads = width // HEAD_DIM
    proj_ret, wgu = _proj(x, w_in, 0, 4 * width, F32, cast=ffn_cast[:1])
    proj_att, wd, *early = _proj(x, w_in, 4 * width, 3 * width, MXU_DTYPE, cast=ffn_cast[1:] + early_cast)
    cos2, sin2 = _rope_tables(seq)
    log_g = jnp.log1p(-jnp.exp2(-5.0 - jnp.arange(heads, dtype=F32)))
    log_g = jnp.broadcast_to(log_g[:, None, None], (heads, 1, HEAD_DIM))
    y_ret = _retention(proj_ret, cos2, sin2, log_g, gn_w, batch, seq, heads)
    y_att = _swa(proj_att, batch, seq, heads)
    return _proj_ln([y_ret, y_att], w_out, x, lnw, lnb, alpha), (wgu, wd), early


def _mixer_gated_deltanet(x, w_in, w_gates, conv_w, a_log, dt_bias, norm_w, w_out, lnw, lnb, alpha, batch, seq,
                          ffn_cast):
    D = x.shape[1]
    heads = D // HEAD_DIM
    qkv, wgu = _proj(x, w_in, 0, 3 * D, F32, cast=ffn_cast[:1])
    gate, wd = _proj(x, w_in, 3 * D, D, F32, cast=ffn_cast[1:])
    g_col, g_row = _dn_gates(x, w_gates, a_log, dt_bias, heads)
    o = _delta_rule(qkv, conv_w, gate, g_col, g_row, norm_w, batch, seq, heads)
    return _proj_ln([o], w_out, x, lnw, lnb, alpha), (wgu, wd)


def kernel(x, ffn_w_gate, ffn_w_up, ffn_w_down, ln_w, ln_b, ab_w_in, ab_gn_w, ab_w_out,
           dn_w_in, dn_conv_w, dn_a_log, dn_dt_bias, dn_norm_w, dn_w_out):
    batch, seq, D = x.shape
    depth = ffn_w_gate.shape[0]
    alpha = (2.0 * depth) ** 0.25
    h = x.reshape(batch * seq, D)
    ffn_items = lambda at: [((ffn_w_gate, ffn_w_up), at), ((ffn_w_down,), at)]
    pre = [_cast_weight(*item) for item in ffn_items((0, 0))]
    for l in range(depth):
        h = _ffn_ln(h, *pre, ln_w[l, 0], ln_b[l, 0], alpha)
        i = l // 2
        next_at = (l + 1, 0) if l + 1 < depth else None
        early = []
        if l % 2 == 0:
            early_cast = ffn_items(next_at)[1:] if next_at else []
            h, post, early = _mixer_retention_swa(h, ab_w_in[i].astype(MXU_DTYPE), ab_gn_w[i],
                                                  ab_w_out[i].astype(MXU_DTYPE), ln_w[l, 1], ln_b[l, 1], alpha,
                                                  batch, seq, ffn_items((l, 1)), early_cast)
        else:
            h, post = _mixer_gated_deltanet(h, dn_w_in[i].astype(MXU_DTYPE), dn_w_in[i][:, 4 * D:], dn_conv_w[i],
                                            dn_a_log[i], dn_dt_bias[i], dn_norm_w[i], dn_w_out[i].astype(MXU_DTYPE),
                                            ln_w[l, 1], ln_b[l, 1], alpha, batch, seq, ffn_items((l, 1)))
        h = _ffn_ln(h, *post, ln_w[l, 2], ln_b[l, 2], alpha)
        if next_at:
            gate_up, down = ffn_items(next_at)
            pre = [_cast_weight(*gate_up), early[0] if early else _cast_weight(*down)]
    return h.reshape(batch, seq, D)
```

```python
import functools
import math

import jax
import jax.numpy as jnp
from jax import lax
from jax.experimental import pallas as pl
from jax.experimental.pallas import tpu as pltpu

F32 = jnp.float32
MXU_DTYPE = jnp.bfloat16
HEAD_DIM = 128
LN_EPS = 1e-5
NORM_EPS = 1e-6
ROPE_BASE = 10000.0
SWA_PATTERNS = ((128, 1), (512, 4), (2048, 16))
SWA_FAR = max(w for w, _ in SWA_PATTERNS)
SWA_ROWS = 128
FFN_TILE = 256
DN_CHUNK = 128
LN_ROWS = 128
DN_GROUP = 8
RET_CHUNK = 128
MASKED = -1e30
V7X_VMEM_LIMIT_BYTES = 56 * 1024 * 1024


def _cparams(sem):
    return pltpu.CompilerParams(dimension_semantics=sem, vmem_limit_bytes=V7X_VMEM_LIMIT_BYTES)


def _mm(a, b):
    return jnp.dot(a.astype(MXU_DTYPE), b.astype(MXU_DTYPE), preferred_element_type=F32)


def _mm_nt(a, b):
    return lax.dot_general(a.astype(MXU_DTYPE), b.astype(MXU_DTYPE), (((1,), (1,)), ((), ())),
                           preferred_element_type=F32)


def _mm_tn(a, b):
    return lax.dot_general(a.astype(MXU_DTYPE), b.astype(MXU_DTYPE), (((0,), (0,)), ((), ())),
                           preferred_element_type=F32)


def _silu(x):
    return x * jax.nn.sigmoid(x)


def _layer_norm(y, w, b):
    mu = jnp.mean(y, axis=-1, keepdims=True)
    yc = y - mu
    var = jnp.mean(yc * yc, axis=-1, keepdims=True)
    return yc * lax.rsqrt(var + LN_EPS) * w + b


def _tile(n, pref, unit):
    if n <= pref:
        return n
    t = (pref // unit) * unit
    while t > unit and n % t:
        t -= unit
    assert n % t == 0, (n, pref, unit)
    return t


def _residual_ln_rows(x_ref, o_ref, ln_ref, alpha, rows):
    tm, d = o_ref.shape
    slabs = [slice(c * 128, (c + 1) * 128) for c in range(d // 128)]

    def body(r, carry):
        sl = pl.ds(pl.multiple_of(r * rows, rows), rows)
        s1 = jnp.zeros((rows, 128), F32)
        for cs in slabs:
            y = alpha * x_ref[sl, cs] + o_ref[sl, cs]
            o_ref[sl, cs] = y
            s1 = s1 + y
        mu = jnp.sum(s1, axis=-1, keepdims=True) * (1.0 / d)
        s2 = jnp.zeros((rows, 128), F32)
        for cs in slabs:
            yc = o_ref[sl, cs] - mu
            s2 = s2 + yc * yc
        rstd = lax.rsqrt(jnp.sum(s2, axis=-1, keepdims=True) * (1.0 / d) + LN_EPS)
        for cs in slabs:
            o_ref[sl, cs] = (o_ref[sl, cs] - mu) * rstd * ln_ref[0:1, cs] + ln_ref[1:2, cs]
        return carry

    lax.fori_loop(0, tm // rows, body, 0)


def _ffn_kernel(x_ref, wgu_ref, wd_ref, ln_ref, o_ref, xb_ref, *, alpha, n_split, rows):
    j = pl.program_id(1)
    tf = wd_ref.shape[0]

    @pl.when(j == 0)
    def _():
        xb_ref[...] = x_ref[...].astype(xb_ref.dtype)
        o_ref[...] = jnp.zeros_like(o_ref)

    gu = jnp.dot(xb_ref[...], wgu_ref[...], preferred_element_type=F32)
    h = (_silu(gu[:, :tf]) * gu[:, tf:] * 0.5).astype(xb_ref.dtype)
    dn = o_ref.shape[1] // n_split
    for n in range(n_split):
        cols = slice(n * dn, (n + 1) * dn)
        o_ref[:, cols] += jnp.dot(h, wd_ref[:, cols], preferred_element_type=F32)

    @pl.when(j == pl.num_programs(1) - 1)
    def _():
        _residual_ln_rows(x_ref, o_ref, ln_ref, alpha, rows)


def _ffn_ln(x, wgu, wd, lnw, lnb, alpha):
    T, D = x.shape
    nf, _, tf2 = wgu.shape
    tf = tf2 // 2
    tm = _tile(T, 1024, 8)
    kern = functools.partial(_ffn_kernel, alpha=alpha, n_split=max(D // 512, 1), rows=LN_ROWS)
    once = pl.Buffered(1)
    return pl.pallas_call(
        kern,
        grid=(T // tm, nf),
        in_specs=[
            pl.BlockSpec((tm, D), lambda i, j: (i, 0), pipeline_mode=once),
            pl.BlockSpec((None, D, tf2), lambda i, j: (j, 0, 0)),
            pl.BlockSpec((tf, D), lambda i, j: (j, 0)),
            pl.BlockSpec((2, D), lambda i, j: (0, 0)),
        ],
        out_specs=pl.BlockSpec((tm, D), lambda i, j: (i, 0), pipeline_mode=once),
        out_shape=jax.ShapeDtypeStruct((T, D), F32),
        scratch_shapes=[pltpu.VMEM((tm, D), MXU_DTYPE)],
        compiler_params=_cparams(("parallel", "arbitrary")),
        name="ffn_ln",
    )(x, wgu, wd, jnp.stack([lnw, lnb]))


def _cast_jobs(cast, n_steps, step):
    in_specs, out_specs, out_shapes = [], [], []
    for arrays, prefix in cast:
        rows, cols = arrays[0].shape[-2:]
        units = rows // 16
        nrb = max(d for d in range(1, min(units, n_steps) + 1) if units % d == 0)
        br = rows // nrb
        blk = lambda *ids, nrb=nrb: jnp.minimum(step(*ids), nrb - 1)
        for _ in arrays:
            in_specs.append(pl.BlockSpec((None,) * len(prefix) + (br, cols),
                                         lambda *ids, prefix=prefix, blk=blk: (*prefix, blk(*ids), 0)))
        if len(arrays) == 1:
            out_specs.append(pl.BlockSpec((br, cols), lambda *ids, blk=blk: (blk(*ids), 0)))
            out_shapes.append(jax.ShapeDtypeStruct((rows, cols), MXU_DTYPE))
        else:
            nf = cols // FFN_TILE
            out_specs.append(pl.BlockSpec((nf, br, 2 * FFN_TILE), lambda *ids, blk=blk: (0, blk(*ids), 0)))
            out_shapes.append(jax.ShapeDtypeStruct((nf, rows, 2 * FFN_TILE), MXU_DTYPE))
    return in_specs, out_specs, out_shapes


def _run_casts(src_refs, dst_refs, arity):
    s = 0
    for dst_ref, n_src in zip(dst_refs, arity):
        if n_src == 1:
            dst_ref[...] = src_refs[s][...].astype(dst_ref.dtype)
        else:
            gate_ref, up_ref = src_refs[s], src_refs[s + 1]
            for j in range(dst_ref.shape[0]):
                cols = slice(j * FFN_TILE, (j + 1) * FFN_TILE)
                dst_ref[j, :, :FFN_TILE] = gate_ref[:, cols].astype(dst_ref.dtype)
                dst_ref[j, :, FFN_TILE:] = up_ref[:, cols].astype(dst_ref.dtype)
        s += n_src


def _cast_kernel(*refs, arity):
    _run_casts(refs[:sum(arity)], refs[sum(arity):], arity)


def _cast_weight(arrays, prefix):
    rows, cols = arrays[0].shape[-2:]
    n_steps = max(1, (rows * cols * 4) // (2 << 20))
    c_in, c_out, c_shape = _cast_jobs([(arrays, prefix)], n_steps, lambda r: r)
    nrb = rows // c_out[0].block_shape[-2]
    return pl.pallas_call(
        functools.partial(_cast_kernel, arity=(len(arrays),)),
        grid=(nrb,),
        in_specs=c_in,
        out_specs=c_out,
        out_shape=c_shape,
        compiler_params=_cparams(("parallel",)),
        name="cast_weight",
    )(*arrays)[0]


def _proj_kernel(*refs, arity):
    n_src = sum(arity)
    x_ref, w_ref = refs[:2]
    o_ref = refs[2 + n_src]
    o_ref[...] = jnp.dot(x_ref[...].astype(w_ref.dtype), w_ref[...], preferred_element_type=F32).astype(o_ref.dtype)
    _run_casts(refs[2:2 + n_src], refs[3 + n_src:], arity)


def _proj(x, w, col0, ncols, out_dtype, cast=()):
    T, K = x.shape
    tm = _tile(T, 512, 8)
    tn = _tile(math.gcd(ncols, col0) if col0 else ncols, 2048, 128)
    j0 = col0 // tn
    n_i = T // tm
    c_in, c_out, c_shape = _cast_jobs(cast, (ncols // tn) * n_i, lambda j, i: j * n_i + i)
    return pl.pallas_call(
        functools.partial(_proj_kernel, arity=tuple(len(arrays) for arrays, _ in cast)),
        grid=(ncols // tn, n_i),
        in_specs=[pl.BlockSpec((tm, K), lambda j, i: (i, 0)),
                  pl.BlockSpec((K, tn), lambda j, i: (0, j0 + j), pipeline_mode=pl.Buffered(1))] + c_in,
        out_specs=[pl.BlockSpec((tm, tn), lambda j, i: (i, j))] + c_out,
        out_shape=[jax.ShapeDtypeStruct((T, ncols), out_dtype)] + c_shape,
        compiler_params=_cparams(("arbitrary", "arbitrary")),
        name="in_proj",
    )(x, w, *(a for arrays, _ in cast for a in arrays))


def _proj_ln_kernel(*refs, alpha, n_parts, n_split, rows):
    a_refs = refs[:n_parts]
    w_ref, x_ref, ln_ref, o_ref = refs[n_parts:]
    dn = o_ref.shape[1] // n_split
    for n in range(n_split):
        cols = slice(n * dn, (n + 1) * dn)
        k0 = 0
        acc = None
        for a_ref in a_refs:
            kp = a_ref.shape[1]
            part = jnp.dot(a_ref[...], w_ref[k0:k0 + kp, cols], preferred_element_type=F32)
            acc = part if acc is None else acc + part
            k0 += kp
        o_ref[:, cols] = acc
    _residual_ln_rows(x_ref, o_ref, ln_ref, alpha, rows)


def _proj_ln(parts, w, x, lnw, lnb, alpha):
    T = x.shape[0]
    K, D = w.shape
    assert sum(p.shape[1] for p in parts) == K
    tm = _tile(T, 256, 8)
    kern = functools.partial(_proj_ln_kernel, alpha=alpha, n_parts=len(parts), n_split=max(D // 512, 1),
                             rows=LN_ROWS)
    return pl.pallas_call(
        kern,
        grid=(T // tm,),
        in_specs=[pl.BlockSpec((tm, p.shape[1]), lambda i: (i, 0)) for p in parts] + [
            pl.BlockSpec((K, D), lambda i: (0, 0), pipeline_mode=pl.Buffered(1)),
            pl.BlockSpec((tm, D), lambda i: (i, 0)),
            pl.BlockSpec((2, D), lambda i: (0, 0)),
        ],
        out_specs=pl.BlockSpec((tm, D), lambda i: (i, 0)),
        out_shape=jax.ShapeDtypeStruct((T, D), F32),
        compiler_params=_cparams(("parallel",)),
        name="out_proj_ln",
    )(*parts, w, x, jnp.stack([lnw, lnb]))


def _retention_kernel(q_ref, k_ref, v_ref, g_ref, cos_ref, sin_ref, lg_ref, gnw_ref, o_ref, st_ref, *, n_chunks):
    c = RET_CHUNK

    @pl.when(pl.program_id(2) == 0)
    def _():
        st_ref[...] = jnp.zeros_like(st_ref)

    lg = lg_ref[0]
    ii = lax.broadcasted_iota(jnp.int32, (c, c), 0)
    jj = lax.broadcasted_iota(jnp.int32, (c, c), 1)
    rel = (ii - jj).astype(F32)
    dmask = jnp.where(rel >= 0, jnp.exp(lg * jnp.maximum(rel, 0.0)), 0.0)
    row = ii.astype(F32)
    zeta = jnp.exp(lg * (c - 1 - row))
    xi = jnp.exp(lg * (row + 1.0))
    chunk_decay = jnp.exp(lg * c)
    even_lane = (jj & 1) == 0

    def rotate(t, cs, sn):
        partner = jnp.where(even_lane, pltpu.roll(t, HEAD_DIM - 1, 1), pltpu.roll(t, 1, 1))
        return t * cs + partner * sn

    chunks = [slice(n * c, (n + 1) * c) for n in range(n_chunks)]
    qr = [rotate(q_ref[sl, :], cos_ref[sl, :], sin_ref[sl, :]) for sl in chunks]
    kr = [rotate(k_ref[sl, :], cos_ref[sl, :], sin_ref[sl, :]) * (HEAD_DIM ** -0.5) for sl in chunks]
    intra = [_mm(_mm_nt(qr[n], kr[n]) * dmask, v_ref[sl, :]) for n, sl in enumerate(chunks)]
    kv = [_mm_tn(kr[n] * zeta, v_ref[sl, :]) for n, sl in enumerate(chunks)]
    st = st_ref[...]
    for n, sl in enumerate(chunks):
        y = intra[n] + _mm(qr[n], st) * xi
        st = st * chunk_decay + kv[n]
        mu = jnp.mean(y, axis=-1, keepdims=True)
        yc = y - mu
        var = jnp.mean(yc * yc, axis=-1, keepdims=True)
        yn = yc * lax.rsqrt(var + LN_EPS) * gnw_ref[...]
        o_ref[sl, :] = (yn * _silu(g_ref[sl, :])).astype(o_ref.dtype)
    st_ref[...] = st


def _retention(proj, cos2, sin2, log_g, gn_w, batch, seq, heads):
    T = batch * seq
    ts = _tile(seq, 2048, RET_CHUNK)
    ns = seq // ts
    blk = lambda off: pl.BlockSpec((ts, HEAD_DIM), lambda b, h, s: (b * ns + s, off * heads + h))
    tab = pl.BlockSpec((ts, HEAD_DIM), lambda b, h, s: (s, 0))
    kern = functools.partial(_retention_kernel, n_chunks=ts // RET_CHUNK)
    return pl.pallas_call(
        kern,
        grid=(batch, heads, ns),
        in_specs=[blk(0), blk(1), blk(2), blk(3), tab, tab,
                  pl.BlockSpec((1, 1, HEAD_DIM), lambda b, h, s: (h, 0, 0)),
                  pl.BlockSpec((1, HEAD_DIM), lambda b, h, s: (0, h))],
        out_specs=pl.BlockSpec((ts, HEAD_DIM), lambda b, h, s: (b * ns + s, h)),
        out_shape=jax.ShapeDtypeStruct((T, heads * HEAD_DIM), MXU_DTYPE),
        scratch_shapes=[pltpu.VMEM((HEAD_DIM, HEAD_DIM), F32)],
        compiler_params=_cparams(("parallel", "parallel", "arbitrary")),
        name="retention",
    )(proj, proj, proj, proj, cos2, sin2, log_g, gn_w.reshape(1, heads * HEAD_DIM))


def _swa_kernel(q_ref, k_ref, v_ref, bias_ref, o_ref, *, tq, span):
    qi = pl.program_id(2)
    start = jnp.maximum(qi * tq - SWA_FAR, 0)
    clamped = qi * tq < SWA_FAR
    groups = range(tq // SWA_ROWS)
    rows = [slice(r * SWA_ROWS, (r + 1) * SWA_ROWS) for r in groups]
    first = [pl.multiple_of(start + jnp.where(clamped, 0, r * SWA_ROWS), SWA_ROWS) for r in groups]
    s = [_mm_nt(q_ref[rows[r], :], k_ref[pl.ds(first[r], span), :]) * (HEAD_DIM ** -0.5) + bias_ref[0, rows[r], :]
         for r in groups]
    p = [jnp.exp(si - jnp.max(si, axis=-1, keepdims=True)) for si in s]
    l = [jnp.sum(pi, axis=-1, keepdims=True) for pi in p]
    for r in groups:
        o_ref[rows[r], :] = (_mm(p[r], v_ref[pl.ds(first[r], span), :]) / l[r]).astype(o_ref.dtype)


def _swa_bias(tq, span):
    nvar = SWA_FAR // tq + 1
    v = lax.broadcasted_iota(jnp.int32, (nvar, tq, span), 0)
    i = lax.broadcasted_iota(jnp.int32, (nvar, tq, span), 1)
    c = lax.broadcasted_iota(jnp.int32, (nvar, tq, span), 2)
    first = jnp.where(v == 0, (i // SWA_ROWS) * SWA_ROWS, 0)
    dist = SWA_FAR - v * tq + i - (first + c)
    mult = jnp.zeros((nvar, tq, span), F32)
    for window, dilation in SWA_PATTERNS:
        mult += ((dist >= 0) & (dist <= window) & (dist % dilation == 0)).astype(F32)
    return jnp.where(mult > 0, jnp.log(jnp.maximum(mult, 1.0)), MASKED)


def _swa(proj, batch, seq, heads):
    T = batch * seq
    tq = _tile(seq, 1024, SWA_ROWS)
    assert SWA_FAR % tq == 0 and seq >= SWA_FAR + tq
    span = SWA_FAR + SWA_ROWS
    nq = seq // tq
    nfar = SWA_FAR // tq
    kern = functools.partial(_swa_kernel, tq=tq, span=span)
    return pl.pallas_call(
        kern,
        grid=(batch, heads, nq),
        in_specs=[
            pl.BlockSpec((tq, HEAD_DIM), lambda b, h, i: (b * nq + i, h)),
            pl.BlockSpec((seq, HEAD_DIM), lambda b, h, i: (b, heads + h)),
            pl.BlockSpec((seq, HEAD_DIM), lambda b, h, i: (b, 2 * heads + h)),
            pl.BlockSpec((1, tq, span), lambda b, h, i: (jnp.maximum(nfar - i, 0), 0, 0)),
        ],
        out_specs=pl.BlockSpec((tq, HEAD_DIM), lambda b, h, i: (b * nq + i, h)),
        out_shape=jax.ShapeDtypeStruct((T, heads * HEAD_DIM), MXU_DTYPE),
        compiler_params=_cparams(("parallel", "parallel", "arbitrary")),
        name="dilated_swa",
    )(proj, proj, proj, _swa_bias(tq, span))


def _softplus(x):
    return jnp.maximum(x, 0.0) + jnp.log1p(jnp.exp(-jnp.abs(x)))


def _dn_gates_kernel(x_ref, w_ref, pc_ref, col_ref, row_ref, *, heads):
    c = DN_CHUNK
    ii = lax.broadcasted_iota(jnp.int32, (c, c), 0)
    jj = lax.broadcasted_iota(jnp.int32, (c, c), 1)
    incl = (ii >= jj).astype(F32)
    lane = lax.broadcasted_iota(jnp.int32, (c, 128), 1)
    for n in range(x_ref.shape[0] // c):
        rows = slice(n * c, (n + 1) * c)
        xb = x_ref[rows, :].astype(MXU_DTYPE)
        col = jnp.dot(xb, w_ref[...], preferred_element_type=F32)
        g_col = -jnp.exp(pc_ref[0:1, :]) * _softplus(col + pc_ref[1:2, :])
        gc_col = jnp.dot(incl, g_col, preferred_element_type=F32, precision=lax.Precision.HIGHEST)
        out = jnp.where(lane < heads, gc_col, jax.nn.sigmoid(col))
        col_ref[rows, :] = out
        row_ref[:, rows] = out.T


def _dn_gates(x, w_small, a_log, dt_bias, heads):
    T, D = x.shape
    assert 2 * heads <= 128
    pad = 128 - 2 * heads
    w = jnp.pad(w_small, ((0, 0), (0, pad))).astype(MXU_DTYPE)
    zeros = jnp.zeros((128 - heads,), F32)
    params = jnp.stack([jnp.concatenate([a_log.astype(F32), zeros]), jnp.concatenate([dt_bias.astype(F32), zeros])])
    ts = _tile(T, 4 * DN_CHUNK, DN_CHUNK)
    kern = functools.partial(_dn_gates_kernel, heads=heads)
    return pl.pallas_call(
        kern,
        grid=(T // ts,),
        in_specs=[
            pl.BlockSpec((ts, D), lambda i: (i, 0)),
            pl.BlockSpec((D, 128), lambda i: (0, 0)),
            pl.BlockSpec((2, 128), lambda i: (0, 0)),
        ],
        out_specs=[pl.BlockSpec((ts, 128), lambda i: (i, 0)), pl.BlockSpec((128, ts), lambda i: (0, i))],
        out_shape=[jax.ShapeDtypeStruct((T, 128), F32), jax.ShapeDtypeStruct((128, T), F32)],
        compiler_params=_cparams(("parallel",)),
        name="dn_gates",
    )(x, w, params)


def _conv_silu(x, prev, w):
    c = x.shape[0]
    taps = w.shape[0]
    xp = jnp.concatenate([prev, x], axis=0)
    acc = x * w[taps - 1:taps, :]
    for i in range(taps - 1):
        off = 8 - (taps - 1) + i
        acc = acc + xp[off:off + c, :] * w[i:i + 1, :]
    return _silu(acc)


def _l2norm(t):
    return t * lax.rsqrt(jnp.sum(t * t, axis=-1, keepdims=True) + NORM_EPS)


def _dn_kernel(q_ref, k_ref, v_ref, cwq_ref, cwk_ref, cwv_ref, gate_ref, col_ref, row_ref, nw_ref, o_ref,
               st_ref, tail_ref, *, group, heads):
    c = DN_CHUNK
    hs = range(group)

    @pl.when(pl.program_id(2) == 0)
    def _():
        st_ref[...] = jnp.zeros_like(st_ref)
        tail_ref[...] = jnp.zeros_like(tail_ref)

    ii = lax.broadcasted_iota(jnp.int32, (c, c), 0)
    jj = lax.broadcasted_iota(jnp.int32, (c, c), 1)
    eye = (ii == jj).astype(F32)
    n_levels = int(math.log2(c))
    level = [(ii >> 1) == (jj >> 1)] + [
        ((ii >> (ls + 1)) == (jj >> (ls + 1))) & (((ii >> ls) & 1) == 1) & (((jj >> ls) & 1) == 0)
        for ls in range(1, n_levels)]
    col = col_ref[...]
    lane = lax.broadcasted_iota(jnp.int32, col.shape, 1)
    cols = [slice(h * HEAD_DIM, (h + 1) * HEAD_DIM) for h in hs]
    head = [pl.program_id(1) * group + h for h in hs]
    q = [_l2norm(_conv_silu(q_ref[:, cols[h]], tail_ref[0, :, cols[h]], cwq_ref[:, cols[h]])) * (HEAD_DIM ** -0.5)
         for h in hs]
    k = [_l2norm(_conv_silu(k_ref[:, cols[h]], tail_ref[1, :, cols[h]], cwk_ref[:, cols[h]])) for h in hs]
    v = [_conv_silu(v_ref[:, cols[h]], tail_ref[2, :, cols[h]], cwv_ref[:, cols[h]]) for h in hs]
    for i, ref in enumerate((q_ref, k_ref, v_ref)):
        tail_ref[i] = ref[c - 8:c, :]
    gc_i =[jnp.sum(jnp.where(lane == head[h], col, 0.0), axis=-1, keepdims=True) for h in hs]
    beta = [jnp.sum(jnp.where(lane == head[h] + heads, col, 0.0), axis=-1, keepdims=True) for h in hs]
    gc_j = [row_ref[pl.ds(head[h], 1), :] for h in hs]
    gc_last = [gc_i[h][c - 1:c, :] for h in hs]
    decay = [jnp.exp(jnp.where(ii >= jj, gc_i[h] - gc_j[h], MASKED)) for h in hs]
    e_g = [jnp.exp(gc_i[h]) for h in hs]
    kb = [k[h] * beta[h] for h in hs]
    kq = [_mm_nt(jnp.concatenate([kb[h], q[h]], axis=0), k[h]) for h in hs]
    a = [jnp.where(ii > jj, kq[h][:c] * decay[h], 0.0) for h in hs]
    qk = [kq[h][c:] * decay[h] for h in hs]
    w0 = [jnp.where(level[0], a[h], 0.0) for h in hs]
    t = [eye - w0[h] for h in hs]
    z = [a[h] - _mm(a[h], w0[h]) for h in hs]
    for ls in range(1, n_levels):
        wl = [jnp.where(level[ls], z[h], 0.0) for h in hs]
        if ls < n_levels - 1:
            tz = [_mm(jnp.concatenate([t[h], z[h]], axis=0), wl[h]) for h in hs]
            t = [t[h] - tz[h][:c] for h in hs]
            z = [z[h] - tz[h][c:] for h in hs]
        else:
            t = [t[h] - _mm(t[h], wl[h]) for h in hs]
    uw = [_mm(t[h], jnp.concatenate([v[h] * beta[h], kb[h] * e_g[h]], axis=1)) for h in hs]
    st = [st_ref[h] for h in hs]
    ws_qs = [_mm(jnp.concatenate([uw[h][:, HEAD_DIM:], q[h] * e_g[h]], axis=0), st[h]) for h in hs]
    v_new = [uw[h][:, :HEAD_DIM] - ws_qs[h][:c] for h in hs]
    o = [ws_qs[h][c:] + _mm(qk[h], v_new[h]) for h in hs]
    for h in hs:
        st_ref[h] = st[h] * jnp.exp(gc_last[h]) + _mm_tn(k[h] * jnp.exp(gc_last[h] - gc_i[h]), v_new[h])
    for h in hs:
        on = o[h] * lax.rsqrt(jnp.mean(o[h] * o[h], axis=-1, keepdims=True) + NORM_EPS) * nw_ref[...]
        o_ref[:, cols[h]] = (on * _silu(gate_ref[:, cols[h]])).astype(o_ref.dtype)


def _delta_rule(qkv, conv_w, gate, g_col, g_row, norm_w, batch, seq, heads):
    T = batch * seq
    c = DN_CHUNK
    group = _tile(heads, DN_GROUP, 1)
    ng = heads // group
    nc = seq // c
    gw = group * HEAD_DIM
    taps = conv_w.shape[0]
    blk = lambda off: pl.BlockSpec((c, gw), lambda b, g, n: (b * nc + n, off * ng + g))
    cw = lambda off: pl.BlockSpec((taps, gw), lambda b, g, n: (0, off * ng + g))
    kern = functools.partial(_dn_kernel, group=group, heads=heads)
    return pl.pallas_call(
        kern,
        grid=(batch, ng, nc),
        in_specs=[blk(0), blk(1), blk(2), cw(0), cw(1), cw(2),
                  pl.BlockSpec((c, gw), lambda b, g, n: (b * nc + n, g)),
                  pl.BlockSpec((c, 128), lambda b, g, n: (b * nc + n, 0)),
                  pl.BlockSpec((128, c), lambda b, g, n: (0, b * nc + n)),
                  pl.BlockSpec((1, HEAD_DIM), lambda b, g, n: (0, 0))],
        out_specs=pl.BlockSpec((c, gw), lambda b, g, n: (b * nc + n, g)),
        out_shape=jax.ShapeDtypeStruct((T, heads * HEAD_DIM), MXU_DTYPE),
        scratch_shapes=[pltpu.VMEM((group, HEAD_DIM, HEAD_DIM), F32),
                        pltpu.VMEM((3, 8, gw), F32)],
        compiler_params=_cparams(("parallel", "parallel", "arbitrary")),
        name="delta_rule",
    )(qkv, qkv, qkv, conv_w, conv_w, conv_w, gate, g_col, g_row, norm_w.reshape(1, HEAD_DIM))


def _rope_tables(seq):
    pos = jnp.arange(seq, dtype=F32)
    inv_freq = 1.0 / (ROPE_BASE ** jnp.linspace(0.0, 1.0, HEAD_DIM // 2, dtype=F32))
    ang = pos[:, None] * inv_freq[None, :]
    cos, sin = jnp.cos(ang), jnp.sin(ang)
    return jnp.repeat(cos, 2, axis=1), jnp.stack([-sin, sin], axis=-1).reshape(seq, HEAD_DIM)


def _mixer_retention_swa(x, w_in, gn_w, w_out, lnw, lnb, alpha, batch, seq, ffn_cast, early_cast):
    D = x.shape[1]
    width = D // 2
    heads = width // HEAD_DIM
    proj_ret, wgu = _proj(x, w_in, 0, 4 * width, F32, cast=ffn_cast[:1])
    proj_att, wd, *early = _proj(x, w_in, 4 * width, 3 * width, MXU_DTYPE, cast=ffn_cast[1:] + early_cast)
    cos2, sin2 = _rope_tables(seq)
    log_g = jnp.log1p(-jnp.exp2(-5.0 - jnp.arange(heads, dtype=F32)))
    log_g = jnp.broadcast_to(log_g[:, None, None], (heads, 1, HEAD_DIM))
    y_ret = _retention(proj_ret, cos2, sin2, log_g, gn_w, batch, seq, heads)
    y_att = _swa(proj_att, batch, seq, heads)
    return _proj_ln([y_ret, y_att], w_out, x, lnw, lnb, alpha), (wgu, wd), early


def _mixer_gated_deltanet(x, w_in, w_gates, conv_w, a_log, dt_bias, norm_w, w_out, lnw, lnb, alpha, batch, seq,
                          ffn_cast):
    D = x.shape[1]
    heads = D // HEAD_DIM
    qkv, wgu = _proj(x, w_in, 0, 3 * D, F32, cast=ffn_cast[:1])
    gate, wd = _proj(x, w_in, 3 * D, D, F32, cast=ffn_cast[1:])
    g_col, g_row = _dn_gates(x, w_gates, a_log, dt_bias, heads)
    o = _delta_rule(qkv, conv_w, gate, g_col, g_row, norm_w, batch, seq, heads)
    return _proj_ln([o], w_out, x, lnw, lnb, alpha), (wgu, wd)


def kernel(x, ffn_w_gate, ffn_w_up, ffn_w_down, ln_w, ln_b, ab_w_in, ab_gn_w, ab_w_out,
           dn_w_in, dn_conv_w, dn_a_log, dn_dt_bias, dn_norm_w, dn_w_out):
    batch, seq, D = x.shape
    depth = ffn_w_gate.shape[0]
    alpha = (2.0 * depth) ** 0.25
    h = x.reshape(batch * seq, D)
    ffn_items = lambda at: [((ffn_w_gate, ffn_w_up), at), ((ffn_w_down,), at)]
    pre = [_cast_weight(*item) for item in ffn_items((0, 0))]
    for l in range(depth):
        h = _ffn_ln(h, *pre, ln_w[l, 0], ln_b[l, 0], alpha)
        i = l // 2
        next_at = (l + 1, 0) if l + 1 < depth else None
        early = []
        if l % 2 == 0:
            early_cast = ffn_items(next_at)[1:] if next_at else []
            h, post, early = _mixer_retention_swa(h, ab_w_in[i].astype(MXU_DTYPE), ab_gn_w[i],
                                                  ab_w_out[i].astype(MXU_DTYPE), ln_w[l, 1], ln_b[l, 1], alpha,
                                                  batch, seq, ffn_items((l, 1)), early_cast)
        else:
            h, post = _mixer_gated_deltanet(h, dn_w_in[i].astype(MXU_DTYPE), dn_w_in[i][:, 4 * D:], dn_conv_w[i],
                                            dn_a_log[i], dn_dt_bias[i], dn_norm_w[i], dn_w_out[i].astype(MXU_DTYPE),
                                            ln_w[l, 1], ln_b[l, 1], alpha, batch, seq, ffn_items((l, 1)))
        h = _ffn_ln(h, *post, ln_w[l, 2], ln_b[l, 2], alpha)
        if next_at:
            gate_up, down = ffn_items(next_at)
            pre = [_cast_weight(*gate_up), early[0] if early else _cast_weight(*down)]
    return h.reshape(batch, seq, D)
```

```python
import functools
import math

import jax
import jax.numpy as jnp
from jax import lax
from jax.experimental import pallas as pl
from jax.experimental.pallas import tpu as pltpu

F32 = jnp.float32
MXU_DTYPE = jnp.bfloat16
HEAD_DIM = 128
LN_EPS = 1e-5
NORM_EPS = 1e-6
ROPE_BASE = 10000.0
SWA_PATTERNS = ((128, 1), (512, 4), (2048, 16))
SWA_FAR = max(w for w, _ in SWA_PATTERNS)
SWA_ROWS = 128
FFN_TILE = 256
DN_CHUNK = 128
LN_ROWS = 128
DN_GROUP = 16
RET_CHUNK = 128
MASKED = -1e30
V7X_VMEM_LIMIT_BYTES = 56 * 1024 * 1024


def _cparams(sem):
    return pltpu.CompilerParams(dimension_semantics=sem, vmem_limit_bytes=V7X_VMEM_LIMIT_BYTES)


def _mm(a, b):
    return jnp.dot(a.astype(MXU_DTYPE), b.astype(MXU_DTYPE), preferred_element_type=F32)


def _mm_nt(a, b):
    return lax.dot_general(a.astype(MXU_DTYPE), b.astype(MXU_DTYPE), (((1,), (1,)), ((), ())),
                           preferred_element_type=F32)


def _mm_tn(a, b):
    return lax.dot_general(a.astype(MXU_DTYPE), b.astype(MXU_DTYPE), (((0,), (0,)), ((), ())),
                           preferred_element_type=F32)


def _silu(x):
    return x * jax.nn.sigmoid(x)


def _layer_norm(y, w, b):
    mu = jnp.mean(y, axis=-1, keepdims=True)
    yc = y - mu
    var = jnp.mean(yc * yc, axis=-1, keepdims=True)
    return yc * lax.rsqrt(var + LN_EPS) * w + b


def _tile(n, pref, unit):
    if n <= pref:
        return n
    t = (pref // unit) * unit
    while t > unit and n % t:
        t -= unit
    assert n % t == 0, (n, pref, unit)
    return t


def _residual_ln_rows(x_ref, o_ref, ln_ref, alpha, rows):
    tm, d = o_ref.shape
    slabs = [slice(c * 128, (c + 1) * 128) for c in range(d // 128)]

    def body(r, carry):
        sl = pl.ds(pl.multiple_of(r * rows, rows), rows)
        s1 = jnp.zeros((rows, 128), F32)
        for cs in slabs:
            y = alpha * x_ref[sl, cs] + o_ref[sl, cs]
            o_ref[sl, cs] = y
            s1 = s1 + y
        mu = jnp.sum(s1, axis=-1, keepdims=True) * (1.0 / d)
        s2 = jnp.zeros((rows, 128), F32)
        for cs in slabs:
            yc = o_ref[sl, cs] - mu
            s2 = s2 + yc * yc
        rstd = lax.rsqrt(jnp.sum(s2, axis=-1, keepdims=True) * (1.0 / d) + LN_EPS)
        for cs in slabs:
            o_ref[sl, cs] = (o_ref[sl, cs] - mu) * rstd * ln_ref[0:1, cs] + ln_ref[1:2, cs]
        return carry

    lax.fori_loop(0, tm // rows, body, 0)


def _ffn_kernel(x_ref, wgu_ref, wd_ref, ln_ref, o_ref, xb_ref, *, alpha, n_split, rows):
    j = pl.program_id(1)
    tf = wd_ref.shape[0]

    @pl.when(j == 0)
    def _():
        xb_ref[...] = x_ref[...].astype(xb_ref.dtype)
        o_ref[...] = jnp.zeros_like(o_ref)

    gu = jnp.dot(xb_ref[...], wgu_ref[...], preferred_element_type=F32)
    h = (_silu(gu[:, :tf]) * gu[:, tf:] * 0.5).astype(xb_ref.dtype)
    dn = o_ref.shape[1] // n_split
    for n in range(n_split):
        cols = slice(n * dn, (n + 1) * dn)
        o_ref[:, cols] += jnp.dot(h, wd_ref[:, cols], preferred_element_type=F32)

    @pl.when(j == pl.num_programs(1) - 1)
    def _():
        _residual_ln_rows(x_ref, o_ref, ln_ref, alpha, rows)


def _ffn_ln(x, wgu, wd, lnw, lnb, alpha):
    T, D = x.shape
    nf, _, tf2 = wgu.shape
    tf = tf2 // 2
    tm = _tile(T, 1024, 8)
    kern = functools.partial(_ffn_kernel, alpha=alpha, n_split=max(D // 512, 1), rows=LN_ROWS)
    once = pl.Buffered(1)
    return pl.pallas_call(
        kern,
        grid=(T // tm, nf),
        in_specs=[
            pl.BlockSpec((tm, D), lambda i, j: (i, 0), pipeline_mode=once),
            pl.BlockSpec((None, D, tf2), lambda i, j: (j, 0, 0)),
            pl.BlockSpec((tf, D), lambda i, j: (j, 0)),
            pl.BlockSpec((2, D), lambda i, j: (0, 0)),
        ],
        out_specs=pl.BlockSpec((tm, D), lambda i, j: (i, 0), pipeline_mode=once),
        out_shape=jax.ShapeDtypeStruct((T, D), F32),
        scratch_shapes=[pltpu.VMEM((tm, D), MXU_DTYPE)],
        compiler_params=_cparams(("parallel", "arbitrary")),
        name="ffn_ln",
    )(x, wgu, wd, jnp.stack([lnw, lnb]))


def _cast_jobs(cast, n_steps, step):
    in_specs, out_specs, out_shapes = [], [], []
    for arrays, prefix in cast:
        rows, cols = arrays[0].shape[-2:]
        units = rows // 16
        nrb = max(d for d in range(1, min(units, n_steps) + 1) if units % d == 0)
        br = rows // nrb
        blk = lambda *ids, nrb=nrb: jnp.minimum(step(*ids), nrb - 1)
        for _ in arrays:
            in_specs.append(pl.BlockSpec((None,) * len(prefix) + (br, cols),
                                         lambda *ids, prefix=prefix, blk=blk: (*prefix, blk(*ids), 0)))
        if len(arrays) == 1:
            out_specs.append(pl.BlockSpec((br, cols), lambda *ids, blk=blk: (blk(*ids), 0)))
            out_shapes.append(jax.ShapeDtypeStruct((rows, cols), MXU_DTYPE))
        else:
            nf = cols // FFN_TILE
            out_specs.append(pl.BlockSpec((nf, br, 2 * FFN_TILE), lambda *ids, blk=blk: (0, blk(*ids), 0)))
            out_shapes.append(jax.ShapeDtypeStruct((nf, rows, 2 * FFN_TILE), MXU_DTYPE))
    return in_specs, out_specs, out_shapes


def _run_casts(src_refs, dst_refs, arity):
    s = 0
    for dst_ref, n_src in zip(dst_refs, arity):
        if n_src == 1:
            dst_ref[...] = src_refs[s][...].astype(dst_ref.dtype)
        else:
            gate_ref, up_ref = src_refs[s], src_refs[s + 1]
            for j in range(dst_ref.shape[0]):
                cols = slice(j * FFN_TILE, (j + 1) * FFN_TILE)
                dst_ref[j, :, :FFN_TILE] = gate_ref[:, cols].astype(dst_ref.dtype)
                dst_ref[j, :, FFN_TILE:] = up_ref[:, cols].astype(dst_ref.dtype)
        s += n_src


def _cast_kernel(*refs, arity):
    _run_casts(refs[:sum(arity)], refs[sum(arity):], arity)


def _cast_weight(arrays, prefix):
    rows, cols = arrays[0].shape[-2:]
    n_steps = max(1, (rows * cols * 4) // (2 << 20))
    c_in, c_out, c_shape = _cast_jobs([(arrays, prefix)], n_steps, lambda r: r)
    nrb = rows // c_out[0].block_shape[-2]
    return pl.pallas_call(
        functools.partial(_cast_kernel, arity=(len(arrays),)),
        grid=(nrb,),
        in_specs=c_in,
        out_specs=c_out,
        out_shape=c_shape,
        compiler_params=_cparams(("parallel",)),
        name="cast_weight",
    )(*arrays)[0]


def _proj_kernel(*refs, arity):
    n_src = sum(arity)
    x_ref, w_ref = refs[:2]
    o_ref = refs[2 + n_src]
    o_ref[...] = jnp.dot(x_ref[...].astype(w_ref.dtype), w_ref[...], preferred_element_type=F32).astype(o_ref.dtype)
    _run_casts(refs[2:2 + n_src], refs[3 + n_src:], arity)


def _proj(x, w, col0, ncols, out_dtype, cast=()):
    T, K = x.shape
    tm = _tile(T, 512, 8)
    tn = _tile(math.gcd(ncols, col0) if col0 else ncols, 2048, 128)
    j0 = col0 // tn
    n_i = T // tm
    c_in, c_out, c_shape = _cast_jobs(cast, (ncols // tn) * n_i, lambda j, i: j * n_i + i)
    return pl.pallas_call(
        functools.partial(_proj_kernel, arity=tuple(len(arrays) for arrays, _ in cast)),
        grid=(ncols // tn, n_i),
        in_specs=[pl.BlockSpec((tm, K), lambda j, i: (i, 0)),
                  pl.BlockSpec((K, tn), lambda j, i: (0, j0 + j), pipeline_mode=pl.Buffered(1))] + c_in,
        out_specs=[pl.BlockSpec((tm, tn), lambda j, i: (i, j))] + c_out,
        out_shape=[jax.ShapeDtypeStruct((T, ncols), out_dtype)] + c_shape,
        compiler_params=_cparams(("arbitrary", "arbitrary")),
        name="in_proj",
    )(x, w, *(a for arrays, _ in cast for a in arrays))


def _proj_ln_kernel(*refs, alpha, n_parts, n_split, rows):
    a_refs = refs[:n_parts]
    w_ref, x_ref, ln_ref, o_ref = refs[n_parts:]
    dn = o_ref.shape[1] // n_split
    for n in range(n_split):
        cols = slice(n * dn, (n + 1) * dn)
        k0 = 0
        acc = None
        for a_ref in a_refs:
            kp = a_ref.shape[1]
            part = jnp.dot(a_ref[...], w_ref[k0:k0 + kp, cols], preferred_element_type=F32)
            acc = part if acc is None else acc + part
            k0 += kp
        o_ref[:, cols] = acc
    _residual_ln_rows(x_ref, o_ref, ln_ref, alpha, rows)


def _proj_ln(parts, w, x, lnw, lnb, alpha):
    T = x.shape[0]
    K, D = w.shape
    assert sum(p.shape[1] for p in parts) == K
    tm = _tile(T, 256, 8)
    kern = functools.partial(_proj_ln_kernel, alpha=alpha, n_parts=len(parts), n_split=max(D // 512, 1),
                             rows=LN_ROWS)
    return pl.pallas_call(
        kern,
        grid=(T // tm,),
        in_specs=[pl.BlockSpec((tm, p.shape[1]), lambda i: (i, 0)) for p in parts] + [
            pl.BlockSpec((K, D), lambda i: (0, 0), pipeline_mode=pl.Buffered(1)),
            pl.BlockSpec((tm, D), lambda i: (i, 0)),
            pl.BlockSpec((2, D), lambda i: (0, 0)),
        ],
        out_specs=pl.BlockSpec((tm, D), lambda i: (i, 0)),
        out_shape=jax.ShapeDtypeStruct((T, D), F32),
        compiler_params=_cparams(("parallel",)),
        name="out_proj_ln",
    )(*parts, w, x, jnp.stack([lnw, lnb]))


def _retention_kernel(q_ref, k_ref, v_ref, g_ref, cos_ref, sin_ref, lg_ref, gnw_ref, o_ref, st_ref, *, n_chunks):
    c = RET_CHUNK

    @pl.when(pl.program_id(2) == 0)
    def _():
        st_ref[...] = jnp.zeros_like(st_ref)

    lg = lg_ref[0]
    ii = lax.broadcasted_iota(jnp.int32, (c, c), 0)
    jj = lax.broadcasted_iota(jnp.int32, (c, c), 1)
    rel = (ii - jj).astype(F32)
    dmask = jnp.where(rel >= 0, jnp.exp(lg * jnp.maximum(rel, 0.0)), 0.0)
    row = ii.astype(F32)
    zeta = jnp.exp(lg * (c - 1 - row))
    xi = jnp.exp(lg * (row + 1.0))
    chunk_decay = jnp.exp(lg * c)
    even_lane = (jj & 1) == 0

    def rotate(t, cs, sn):
        partner = jnp.where(even_lane, pltpu.roll(t, HEAD_DIM - 1, 1), pltpu.roll(t, 1, 1))
        return t * cs + partner * sn

    chunks = [slice(n * c, (n + 1) * c) for n in range(n_chunks)]
    qr = [rotate(q_ref[sl, :], cos_ref[sl, :], sin_ref[sl, :]) for sl in chunks]
    kr = [rotate(k_ref[sl, :], cos_ref[sl, :], sin_ref[sl, :]) * (HEAD_DIM ** -0.5) for sl in chunks]
    intra = [_mm(_mm_nt(qr[n], kr[n]) * dmask, v_ref[sl, :]) for n, sl in enumerate(chunks)]
    kv = [_mm_tn(kr[n] * zeta, v_ref[sl, :]) for n, sl in enumerate(chunks)]
    st = st_ref[...]
    for n, sl in enumerate(chunks):
        y = intra[n] + _mm(qr[n], st) * xi
        st = st * chunk_decay + kv[n]
        mu = jnp.mean(y, axis=-1, keepdims=True)
        yc = y - mu
        var = jnp.mean(yc * yc, axis=-1, keepdims=True)
        yn = yc * lax.rsqrt(var + LN_EPS) * gnw_ref[...]
        o_ref[sl, :] = (yn * _silu(g_ref[sl, :])).astype(o_ref.dtype)
    st_ref[...] = st


def _retention(proj, cos2, sin2, log_g, gn_w, batch, seq, heads):
    T = batch * seq
    ts = _tile(seq, 2048, RET_CHUNK)
    ns = seq // ts
    blk = lambda off: pl.BlockSpec((ts, HEAD_DIM), lambda b, h, s: (b * ns + s, off * heads + h))
    tab = pl.BlockSpec((ts, HEAD_DIM), lambda b, h, s: (s, 0))
    kern = functools.partial(_retention_kernel, n_chunks=ts // RET_CHUNK)
    return pl.pallas_call(
        kern,
        grid=(batch, heads, ns),
        in_specs=[blk(0), blk(1), blk(2), blk(3), tab, tab,
                  pl.BlockSpec((1, 1, HEAD_DIM), lambda b, h, s: (h, 0, 0)),
                  pl.BlockSpec((1, HEAD_DIM), lambda b, h, s: (0, h))],
        out_specs=pl.BlockSpec((ts, HEAD_DIM), lambda b, h, s: (b * ns + s, h)),
        out_shape=jax.ShapeDtypeStruct((T, heads * HEAD_DIM), MXU_DTYPE),
        scratch_shapes=[pltpu.VMEM((HEAD_DIM, HEAD_DIM), F32)],
        compiler_params=_cparams(("parallel", "parallel", "arbitrary")),
        name="retention",
    )(proj, proj, proj, proj, cos2, sin2, log_g, gn_w.reshape(1, heads * HEAD_DIM))


def _swa_kernel(q_ref, k_ref, v_ref, bias_ref, o_ref, *, tq, span):
    qi = pl.program_id(2)
    start = jnp.maximum(qi * tq - SWA_FAR, 0)
    clamped = qi * tq < SWA_FAR
    groups = range(tq // SWA_ROWS)
    rows = [slice(r * SWA_ROWS, (r + 1) * SWA_ROWS) for r in groups]
    first = [pl.multiple_of(start + jnp.where(clamped, 0, r * SWA_ROWS), SWA_ROWS) for r in groups]
    s = [_mm_nt(q_ref[rows[r], :], k_ref[pl.ds(first[r], span), :]) * (HEAD_DIM ** -0.5) + bias_ref[0, rows[r], :]
         for r in groups]
    p = [jnp.exp(si - jnp.max(si, axis=-1, keepdims=True)) for si in s]
    l = [jnp.sum(pi, axis=-1, keepdims=True) for pi in p]
    for r in groups:
        o_ref[rows[r], :] = (_mm(p[r], v_ref[pl.ds(first[r], span), :]) / l[r]).astype(o_ref.dtype)


def _swa_bias(tq, span):
    nvar = SWA_FAR // tq + 1
    v = lax.broadcasted_iota(jnp.int32, (nvar, tq, span), 0)
    i = lax.broadcasted_iota(jnp.int32, (nvar, tq, span), 1)
    c = lax.broadcasted_iota(jnp.int32, (nvar, tq, span), 2)
    first = jnp.where(v == 0, (i // SWA_ROWS) * SWA_ROWS, 0)
    dist = SWA_FAR - v * tq + i - (first + c)
    mult = jnp.zeros((nvar, tq, span), F32)
    for window, dilation in SWA_PATTERNS:
        mult += ((dist >= 0) & (dist <= window) & (dist % dilation == 0)).astype(F32)
    return jnp.where(mult > 0, jnp.log(jnp.maximum(mult, 1.0)), MASKED)


def _swa(proj, batch, seq, heads):
    T = batch * seq
    tq = _tile(seq, 1024, SWA_ROWS)
    assert SWA_FAR % tq == 0 and seq >= SWA_FAR + tq
    span = SWA_FAR + SWA_ROWS
    nq = seq // tq
    nfar = SWA_FAR // tq
    kern = functools.partial(_swa_kernel, tq=tq, span=span)
    return pl.pallas_call(
        kern,
        grid=(batch, heads, nq),
        in_specs=[
            pl.BlockSpec((tq, HEAD_DIM), lambda b, h, i: (b * nq + i, h)),
            pl.BlockSpec((seq, HEAD_DIM), lambda b, h, i: (b, heads + h)),
            pl.BlockSpec((seq, HEAD_DIM), lambda b, h, i: (b, 2 * heads + h)),
            pl.BlockSpec((1, tq, span), lambda b, h, i: (jnp.maximum(nfar - i, 0), 0, 0)),
        ],
        out_specs=pl.BlockSpec((tq, HEAD_DIM), lambda b, h, i: (b * nq + i, h)),
        out_shape=jax.ShapeDtypeStruct((T, heads * HEAD_DIM), MXU_DTYPE),
        compiler_params=_cparams(("parallel", "parallel", "arbitrary")),
        name="dilated_swa",
    )(proj, proj, proj, _swa_bias(tq, span))


def _softplus(x):
    return jnp.maximum(x, 0.0) + jnp.log1p(jnp.exp(-jnp.abs(x)))


def _dn_gates_kernel(x_ref, w_ref, pc_ref, col_ref, row_ref, *, heads):
    c = DN_CHUNK
    ii = lax.broadcasted_iota(jnp.int32, (c, c), 0)
    jj = lax.broadcasted_iota(jnp.int32, (c, c), 1)
    incl = (ii >= jj).astype(F32)
    lane = lax.broadcasted_iota(jnp.int32, (c, 128), 1)
    for n in range(x_ref.shape[0] // c):
        rows = slice(n * c, (n + 1) * c)
        xb = x_ref[rows, :].astype(MXU_DTYPE)
        col = jnp.dot(xb, w_ref[...], preferred_element_type=F32)
        g_col = -jnp.exp(pc_ref[0:1, :]) * _softplus(col + pc_ref[1:2, :])
        gc_col = jnp.dot(incl, g_col, preferred_element_type=F32, precision=lax.Precision.HIGHEST)
        out = jnp.where(lane < heads, gc_col, jax.nn.sigmoid(col))
        col_ref[rows, :] = out
        row_ref[:, rows] = out.T


def _dn_gates(x, w_small, a_log, dt_bias, heads):
    T, D = x.shape
    assert 2 * heads <= 128
    pad = 128 - 2 * heads
    w = jnp.pad(w_small, ((0, 0), (0, pad))).astype(MXU_DTYPE)
    zeros = jnp.zeros((128 - heads,), F32)
    params = jnp.stack([jnp.concatenate([a_log.astype(F32), zeros]), jnp.concatenate([dt_bias.astype(F32), zeros])])
    ts = _tile(T, 4 * DN_CHUNK, DN_CHUNK)
    kern = functools.partial(_dn_gates_kernel, heads=heads)
    return pl.pallas_call(
        kern,
        grid=(T // ts,),
        in_specs=[
            pl.BlockSpec((ts, D), lambda i: (i, 0)),
            pl.BlockSpec((D, 128), lambda i: (0, 0)),
            pl.BlockSpec((2, 128), lambda i: (0, 0)),
        ],
        out_specs=[pl.BlockSpec((ts, 128), lambda i: (i, 0)), pl.BlockSpec((128, ts), lambda i: (0, i))],
        out_shape=[jax.ShapeDtypeStruct((T, 128), F32), jax.ShapeDtypeStruct((128, T), F32)],
        compiler_params=_cparams(("parallel",)),
        name="dn_gates",
    )(x, w, params)


def _conv_silu(x, prev, w):
    c = x.shape[0]
    taps = w.shape[0]
    xp = jnp.concatenate([prev, x], axis=0)
    acc = x * w[taps - 1:taps, :]
    for i in range(taps - 1):
        off = 8 - (taps - 1) + i
        acc = acc + xp[off:off + c, :] * w[i:i + 1, :]
    return _silu(acc)


def _l2norm(t):
    return t * lax.rsqrt(jnp.sum(t * t, axis=-1, keepdims=True) + NORM_EPS)


def _dn_kernel(q_ref, k_ref, v_ref, cwq_ref, cwk_ref, cwv_ref, gate_ref, col_ref, row_ref, nw_ref, o_ref,
               st_ref, tail_ref, *, group, heads):
    c = DN_CHUNK
    hs = range(group)

    @pl.when(pl.program_id(2) == 0)
    def _():
        st_ref[...] = jnp.zeros_like(st_ref)
        tail_ref[...] = jnp.zeros_like(tail_ref)

    ii = lax.broadcasted_iota(jnp.int32, (c, c), 0)
    jj = lax.broadcasted_iota(jnp.int32, (c, c), 1)
    eye = (ii == jj).astype(F32)
    n_levels = int(math.log2(c))
    level = [(ii >> 1) == (jj >> 1)] + [
        ((ii >> (ls + 1)) == (jj >> (ls + 1))) & (((ii >> ls) & 1) == 1) & (((jj >> ls) & 1) == 0)
        for ls in range(1, n_levels)]
    col = col_ref[...]
    lane = lax.broadcasted_iota(jnp.int32, col.shape, 1)
    cols = [slice(h * HEAD_DIM, (h + 1) * HEAD_DIM) for h in hs]
    head = [pl.program_id(1) * group + h for h in hs]
    q = [_l2norm(_conv_silu(q_ref[:, cols[h]], tail_ref[0, :, cols[h]], cwq_ref[:, cols[h]])) * (HEAD_DIM ** -0.5)
         for h in hs]
    k = [_l2norm(_conv_silu(k_ref[:, cols[h]], tail_ref[1, :, cols[h]], cwk_ref[:, cols[h]])) for h in hs]
    v = [_conv_silu(v_ref[:, cols[h]], tail_ref[2, :, cols[h]], cwv_ref[:, cols[h]]) for h in hs]
    for i, ref in enumerate((q_ref, k_ref, v_ref)):
        tail_ref[i] = ref[c - 8:c, :]
    gc_i =[jnp.sum(jnp.where(lane == head[h], col, 0.0), axis=-1, keepdims=True) for h in hs]
    beta = [jnp.sum(jnp.where(lane == head[h] + heads, col, 0.0), axis=-1, keepdims=True) for h in hs]
    gc_j = [row_ref[pl.ds(head[h], 1), :] for h in hs]
    gc_last = [gc_i[h][c - 1:c, :] for h in hs]
    decay = [jnp.exp(jnp.where(ii >= jj, gc_i[h] - gc_j[h], MASKED)) for h in hs]
    e_g = [jnp.exp(gc_i[h]) for h in hs]
    kb = [k[h] * beta[h] for h in hs]
    kq = [_mm_nt(jnp.concatenate([kb[h], q[h]], axis=0), k[h]) for h in hs]
    a = [jnp.where(ii > jj, kq[h][:c] * decay[h], 0.0) for h in hs]
    qk = [kq[h][c:] * decay[h] for h in hs]
    w0 = [jnp.where(level[0], a[h], 0.0) for h in hs]
    t = [eye - w0[h] for h in hs]
    z = [a[h] - _mm(a[h], w0[h]) for h in hs]
    for ls in range(1, n_levels):
        wl = [jnp.where(level[ls], z[h], 0.0) for h in hs]
        if ls < n_levels - 1:
            tz = [_mm(jnp.concatenate([t[h], z[h]], axis=0), wl[h]) for h in hs]
            t = [t[h] - tz[h][:c] for h in hs]
            z = [z[h] - tz[h][c:] for h in hs]
        else:
            t = [t[h] - _mm(t[h], wl[h]) for h in hs]
    uw = [_mm(t[h], jnp.concatenate([v[h] * beta[h], kb[h] * e_g[h]], axis=1)) for h in hs]
    st = [st_ref[h] for h in hs]
    ws_qs = [_mm(jnp.concatenate([uw[h][:, HEAD_DIM:], q[h] * e_g[h]], axis=0), st[h]) for h in hs]
    v_new = [uw[h][:, :HEAD_DIM] - ws_qs[h][:c] for h in hs]
    o = [ws_qs[h][c:] + _mm(qk[h], v_new[h]) for h in hs]
    for h in hs:
        st_ref[h] = st[h] * jnp.exp(gc_last[h]) + _mm_tn(k[h] * jnp.exp(gc_last[h] - gc_i[h]), v_new[h])
    for h in hs:
        on = o[h] * lax.rsqrt(jnp.mean(o[h] * o[h], axis=-1, keepdims=True) + NORM_EPS) * nw_ref[...]
        o_ref[:, cols[h]] = (on * _silu(gate_ref[:, cols[h]])).astype(o_ref.dtype)


def _delta_rule(qkv, conv_w, gate, g_col, g_row, norm_w, batch, seq, heads):
    T = batch * seq
    c = DN_CHUNK
    group = _tile(heads, DN_GROUP, 1)
    ng = heads // group
    nc = seq // c
    gw = group * HEAD_DIM
    taps = conv_w.shape[0]
    blk = lambda off: pl.BlockSpec((c, gw), lambda b, g, n: (b * nc + n, off * ng + g))
    cw = lambda off: pl.BlockSpec((taps, gw), lambda b, g, n: (0, off * ng + g))
    kern = functools.partial(_dn_kernel, group=group, heads=heads)
    return pl.pallas_call(
        kern,
        grid=(batch, ng, nc),
        in_specs=[blk(0), blk(1), blk(2), cw(0), cw(1), cw(2),
                  pl.BlockSpec((c, gw), lambda b, g, n: (b * nc + n, g)),
                  pl.BlockSpec((c, 128), lambda b, g, n: (b * nc + n, 0)),
                  pl.BlockSpec((128, c), lambda b, g, n: (0, b * nc + n)),
                  pl.BlockSpec((1, HEAD_DIM), lambda b, g, n: (0, 0))],
        out_specs=pl.BlockSpec((c, gw), lambda b, g, n: (b * nc + n, g)),
        out_shape=jax.ShapeDtypeStruct((T, heads * HEAD_DIM), MXU_DTYPE),
        scratch_shapes=[pltpu.VMEM((group, HEAD_DIM, HEAD_DIM), F32),
                        pltpu.VMEM((3, 8, gw), F32)],
        compiler_params=_cparams(("parallel", "parallel", "arbitrary")),
        name="delta_rule",
    )(qkv, qkv, qkv, conv_w, conv_w, conv_w, gate, g_col, g_row, norm_w.reshape(1, HEAD_DIM))


def _rope_tables(seq):
    pos = jnp.arange(seq, dtype=F32)
    inv_freq = 1.0 / (ROPE_BASE ** jnp.linspace(0.0, 1.0, HEAD_DIM // 2, dtype=F32))
    ang = pos[:, None] * inv_freq[None, :]
    cos, sin = jnp.cos(ang), jnp.sin(ang)
    return jnp.repeat(cos, 2, axis=1), jnp.stack([-sin, sin], axis=-1).reshape(seq, HEAD_DIM)


def _mixer_retention_swa(x, w_in, gn_w, w_out, lnw, lnb, alpha, batch, seq, ffn_cast, early_cast):
    D = x.shape[1]
    width = D // 2
    heads = width // HEAD_DIM
    proj_ret, wgu = _proj(x, w_in, 0, 4 * width, F32, cast=ffn_cast[:1])
    proj_att, wd, *early = _proj(x, w_in, 4 * width, 3 * width, MXU_DTYPE, cast=ffn_cast[1:] + early_cast)
    cos2, sin2 = _rope_tables(seq)
    log_g = jnp.log1p(-jnp.exp2(-5.0 - jnp.arange(heads, dtype=F32)))
    log_g = jnp.broadcast_to(log_g[:, None, None], (heads, 1, HEAD_DIM))
    y_ret = _retention(proj_ret, cos2, sin2, log_g, gn_w, batch, seq, heads)
    y_att = _swa(proj_att, batch, seq, heads)
    return _proj_ln([y_ret, y_att], w_out, x, lnw, lnb, alpha), (wgu, wd), early


def _mixer_gated_deltanet(x, w_in, w_gates, conv_w, a_log, dt_bias, norm_w, w_out, lnw, lnb, alpha, batch, seq,
                          ffn_cast):
    D = x.shape[1]
    heads = D // HEAD_DIM
    qkv, wgu = _proj(x, w_in, 0, 3 * D, F32, cast=ffn_cast[:1])
    gate, wd = _proj(x, w_in, 3 * D, D, F32, cast=ffn_cast[1:])
    g_col, g_row = _dn_gates(x, w_gates, a_log, dt_bias, heads)
    o = _delta_rule(qkv, conv_w, gate, g_col, g_row, norm_w, batch, seq, heads)
    return _proj_ln([o], w_out, x, lnw, lnb, alpha), (wgu, wd)


def kernel(x, ffn_w_gate, ffn_w_up, ffn_w_down, ln_w, ln_b, ab_w_in, ab_gn_w, ab_w_out,
           dn_w_in, dn_conv_w, dn_a_log, dn_dt_bias, dn_norm_w, dn_w_out):
    batch, seq, D = x.shape
    depth = ffn_w_gate.shape[0]
    alpha = (2.0 * depth) ** 0.25
    h = x.reshape(batch * seq, D)
    ffn_items = lambda at: [((ffn_w_gate, ffn_w_up), at), ((ffn_w_down,), at)]
    pre = [_cast_weight(*item) for item in ffn_items((0, 0))]
    for l in range(depth):
        h = _ffn_ln(h, *pre, ln_w[l, 0], ln_b[l, 0], alpha)
        i = l // 2
        next_at = (l + 1, 0) if l + 1 < depth else None
        early = []
        if l % 2 == 0:
            early_cast = ffn_items(next_at)[1:] if next_at else []
            h, post, early = _mixer_retention_swa(h, ab_w_in[i].astype(MXU_DTYPE), ab_gn_w[i],
                                                  ab_w_out[i].astype(MXU_DTYPE), ln_w[l, 1], ln_b[l, 1], alpha,
                                                  batch, seq, ffn_items((l, 1)), early_cast)
        else:
            h, post = _mixer_gated_deltanet(h, dn_w_in[i].astype(MXU_DTYPE), dn_w_in[i][:, 4 * D:], dn_conv_w[i],
                                            dn_a_log[i], dn_dt_bias[i], dn_norm_w[i], dn_w_out[i].astype(MXU_DTYPE),
                                            ln_w[l, 1], ln_b[l, 1], alpha, batch, seq, ffn_items((l, 1)))
        h = _ffn_ln(h, *post, ln_w[l, 2], ln_b[l, 2], alpha)
        if next_at:
            gate_up, down = ffn_items(next_at)
            pre = [_cast_weight(*gate_up), early[0] if early else _cast_weight(*down)]
    return h.reshape(batch, seq, D)
```

```python
import functools
import math

import jax
import jax.numpy as jnp
from jax import lax
from jax.experimental import pallas as pl
from jax.experimental.pallas import tpu as pltpu

F32 = jnp.float32
MXU_DTYPE = jnp.bfloat16
HEAD_DIM = 128
LN_EPS = 1e-5
NORM_EPS = 1e-6
ROPE_BASE = 10000.0
SWA_PATTERNS = ((128, 1), (512, 4), (2048, 16))
SWA_FAR = max(w for w, _ in SWA_PATTERNS)
SWA_ROWS = 128
FFN_TILE = 256
DN_CHUNK = 128
LN_ROWS = 128
DN_GROUP = 32
RET_CHUNK = 128
MASKED = -1e30
V7X_VMEM_LIMIT_BYTES = 56 * 1024 * 1024


def _cparams(sem):
    return pltpu.CompilerParams(dimension_semantics=sem, vmem_limit_bytes=V7X_VMEM_LIMIT_BYTES)


def _mm(a, b):
    return jnp.dot(a.astype(MXU_DTYPE), b.astype(MXU_DTYPE), preferred_element_type=F32)


def _mm_nt(a, b):
    return lax.dot_general(a.astype(MXU_DTYPE), b.astype(MXU_DTYPE), (((1,), (1,)), ((), ())),
                           preferred_element_type=F32)


def _mm_tn(a, b):
    return lax.dot_general(a.astype(MXU_DTYPE), b.astype(MXU_DTYPE), (((0,), (0,)), ((), ())),
                           preferred_element_type=F32)


def _silu(x):
    return x * jax.nn.sigmoid(x)


def _layer_norm(y, w, b):
    mu = jnp.mean(y, axis=-1, keepdims=True)
    yc = y - mu
    var = jnp.mean(yc * yc, axis=-1, keepdims=True)
    return yc * lax.rsqrt(var + LN_EPS) * w + b


def _tile(n, pref, unit):
    if n <= pref:
        return n
    t = (pref // unit) * unit
    while t > unit and n % t:
        t -= unit
    assert n % t == 0, (n, pref, unit)
    return t


def _residual_ln_rows(x_ref, o_ref, ln_ref, alpha, rows):
    tm, d = o_ref.shape
    slabs = [slice(c * 128, (c + 1) * 128) for c in range(d // 128)]

    def body(r, carry):
        sl = pl.ds(pl.multiple_of(r * rows, rows), rows)
        s1 = jnp.zeros((rows, 128), F32)
        for cs in slabs:
            y = alpha * x_ref[sl, cs] + o_ref[sl, cs]
            o_ref[sl, cs] = y
            s1 = s1 + y
        mu = jnp.sum(s1, axis=-1, keepdims=True) * (1.0 / d)
        s2 = jnp.zeros((rows, 128), F32)
        for cs in slabs:
            yc = o_ref[sl, cs] - mu
            s2 = s2 + yc * yc
        rstd = lax.rsqrt(jnp.sum(s2, axis=-1, keepdims=True) * (1.0 / d) + LN_EPS)
        for cs in slabs:
            o_ref[sl, cs] = (o_ref[sl, cs] - mu) * rstd * ln_ref[0:1, cs] + ln_ref[1:2, cs]
        return carry

    lax.fori_loop(0, tm // rows, body, 0)


def _ffn_kernel(x_ref, wgu_ref, wd_ref, ln_ref, o_ref, xb_ref, *, alpha, n_split, rows):
    j = pl.program_id(1)
    tf = wd_ref.shape[0]

    @pl.when(j == 0)
    def _():
        xb_ref[...] = x_ref[...].astype(xb_ref.dtype)
        o_ref[...] = jnp.zeros_like(o_ref)

    gu = jnp.dot(xb_ref[...], wgu_ref[...], preferred_element_type=F32)
    h = (_silu(gu[:, :tf]) * gu[:, tf:] * 0.5).astype(xb_ref.dtype)
    dn = o_ref.shape[1] // n_split
    for n in range(n_split):
        cols = slice(n * dn, (n + 1) * dn)
        o_ref[:, cols] += jnp.dot(h, wd_ref[:, cols], preferred_element_type=F32)

    @pl.when(j == pl.num_programs(1) - 1)
    def _():
        _residual_ln_rows(x_ref, o_ref, ln_ref, alpha, rows)


def _ffn_ln(x, wgu, wd, lnw, lnb, alpha):
    T, D = x.shape
    nf, _, tf2 = wgu.shape
    tf = tf2 // 2
    tm = _tile(T, 1024, 8)
    kern = functools.partial(_ffn_kernel, alpha=alpha, n_split=max(D // 512, 1), rows=LN_ROWS)
    once = pl.Buffered(1)
    return pl.pallas_call(
        kern,
        grid=(T // tm, nf),
        in_specs=[
            pl.BlockSpec((tm, D), lambda i, j: (i, 0), pipeline_mode=once),
            pl.BlockSpec((None, D, tf2), lambda i, j: (j, 0, 0)),
            pl.BlockSpec((tf, D), lambda i, j: (j, 0)),
            pl.BlockSpec((2, D), lambda i, j: (0, 0)),
        ],
        out_specs=pl.BlockSpec((tm, D), lambda i, j: (i, 0), pipeline_mode=once),
        out_shape=jax.ShapeDtypeStruct((T, D), F32),
        scratch_shapes=[pltpu.VMEM((tm, D), MXU_DTYPE)],
        compiler_params=_cparams(("parallel", "arbitrary")),
        name="ffn_ln",
    )(x, wgu, wd, jnp.stack([lnw, lnb]))


def _cast_jobs(cast, n_steps, step):
    in_specs, out_specs, out_shapes = [], [], []
    for arrays, prefix in cast:
        rows, cols = arrays[0].shape[-2:]
        units = rows // 16
        nrb = max(d for d in range(1, min(units, n_steps) + 1) if units % d == 0)
        br = rows // nrb
        blk = lambda *ids, nrb=nrb: jnp.minimum(step(*ids), nrb - 1)
        for _ in arrays:
            in_specs.append(pl.BlockSpec((None,) * len(prefix) + (br, cols),
                                         lambda *ids, prefix=prefix, blk=blk: (*prefix, blk(*ids), 0)))
        if len(arrays) == 1:
            out_specs.append(pl.BlockSpec((br, cols), lambda *ids, blk=blk: (blk(*ids), 0)))
            out_shapes.append(jax.ShapeDtypeStruct((rows, cols), MXU_DTYPE))
        else:
            nf = cols // FFN_TILE
            out_specs.append(pl.BlockSpec((nf, br, 2 * FFN_TILE), lambda *ids, blk=blk: (0, blk(*ids), 0)))
            out_shapes.append(jax.ShapeDtypeStruct((nf, rows, 2 * FFN_TILE), MXU_DTYPE))
    return in_specs, out_specs, out_shapes


def _run_casts(src_refs, dst_refs, arity):
    s = 0
    for dst_ref, n_src in zip(dst_refs, arity):
        if n_src == 1:
            dst_ref[...] = src_refs[s][...].astype(dst_ref.dtype)
        else:
            gate_ref, up_ref = src_refs[s], src_refs[s + 1]
            for j in range(dst_ref.shape[0]):
                cols = slice(j * FFN_TILE, (j + 1) * FFN_TILE)
                dst_ref[j, :, :FFN_TILE] = gate_ref[:, cols].astype(dst_ref.dtype)
                dst_ref[j, :, FFN_TILE:] = up_ref[:, cols].astype(dst_ref.dtype)
        s += n_src


def _cast_kernel(*refs, arity):
    _run_casts(refs[:sum(arity)], refs[sum(arity):], arity)


def _cast_weight(arrays, prefix):
    rows, cols = arrays[0].shape[-2:]
    n_steps = max(1, (rows * cols * 4) // (2 << 20))
    c_in, c_out, c_shape = _cast_jobs([(arrays, prefix)], n_steps, lambda r: r)
    nrb = rows // c_out[0].block_shape[-2]
    return pl.pallas_call(
        functools.partial(_cast_kernel, arity=(len(arrays),)),
        grid=(nrb,),
        in_specs=c_in,
        out_specs=c_out,
        out_shape=c_shape,
        compiler_params=_cparams(("parallel",)),
        name="cast_weight",
    )(*arrays)[0]


def _proj_kernel(*refs, arity):
    n_src = sum(arity)
    x_ref, w_ref = refs[:2]
    o_ref = refs[2 + n_src]
    o_ref[...] = jnp.dot(x_ref[...].astype(w_ref.dtype), w_ref[...], preferred_element_type=F32).astype(o_ref.dtype)
    _run_casts(refs[2:2 + n_src], refs[3 + n_src:], arity)


def _proj(x, w, col0, ncols, out_dtype, cast=()):
    T, K = x.shape
    tm = _tile(T, 512, 8)
    tn = _tile(math.gcd(ncols, col0) if col0 else ncols, 2048, 128)
    j0 = col0 // tn
    n_i = T // tm
    c_in, c_out, c_shape = _cast_jobs(cast, (ncols // tn) * n_i, lambda j, i: j * n_i + i)
    return pl.pallas_call(
        functools.partial(_proj_kernel, arity=tuple(len(arrays) for arrays, _ in cast)),
        grid=(ncols // tn, n_i),
        in_specs=[pl.BlockSpec((tm, K), lambda j, i: (i, 0)),
                  pl.BlockSpec((K, tn), lambda j, i: (0, j0 + j), pipeline_mode=pl.Buffered(1))] + c_in,
        out_specs=[pl.BlockSpec((tm, tn), lambda j, i: (i, j))] + c_out,
        out_shape=[jax.ShapeDtypeStruct((T, ncols), out_dtype)] + c_shape,
        compiler_params=_cparams(("arbitrary", "arbitrary")),
        name="in_proj",
    )(x, w, *(a for arrays, _ in cast for a in arrays))


def _proj_ln_kernel(*refs, alpha, n_parts, n_split, rows):
    a_refs = refs[:n_parts]
    w_ref, x_ref, ln_ref, o_ref = refs[n_parts:]
    dn = o_ref.shape[1] // n_split
    for n in range(n_split):
        cols = slice(n * dn, (n + 1) * dn)
        k0 = 0
        acc = None
        for a_ref in a_refs:
            kp = a_ref.shape[1]
            part = jnp.dot(a_ref[...], w_ref[k0:k0 + kp, cols], preferred_element_type=F32)
            acc = part if acc is None else acc + part
            k0 += kp
        o_ref[:, cols] = acc
    _residual_ln_rows(x_ref, o_ref, ln_ref, alpha, rows)


def _proj_ln(parts, w, x, lnw, lnb, alpha):
    T = x.shape[0]
    K, D = w.shape
    assert sum(p.shape[1] for p in parts) == K
    tm = _tile(T, 256, 8)
    kern = functools.partial(_proj_ln_kernel, alpha=alpha, n_parts=len(parts), n_split=max(D // 512, 1),
                             rows=LN_ROWS)
    return pl.pallas_call(
        kern,
        grid=(T // tm,),
        in_specs=[pl.BlockSpec((tm, p.shape[1]), lambda i: (i, 0)) for p in parts] + [
            pl.BlockSpec((K, D), lambda i: (0, 0), pipeline_mode=pl.Buffered(1)),
            pl.BlockSpec((tm, D), lambda i: (i, 0)),
            pl.BlockSpec((2, D), lambda i: (0, 0)),
        ],
        out_specs=pl.BlockSpec((tm, D), lambda i: (i, 0)),
        out_shape=jax.ShapeDtypeStruct((T, D), F32),
        compiler_params=_cparams(("parallel",)),
        name="out_proj_ln",
    )(*parts, w, x, jnp.stack([lnw, lnb]))


def _retention_kernel(q_ref, k_ref, v_ref, g_ref, cos_ref, sin_ref, lg_ref, gnw_ref, o_ref, st_ref, *, n_chunks):
    c = RET_CHUNK

    @pl.when(pl.program_id(2) == 0)
    def _():
        st_ref[...] = jnp.zeros_like(st_ref)

    lg = lg_ref[0]
    ii = lax.broadcasted_iota(jnp.int32, (c, c), 0)
    jj = lax.broadcasted_iota(jnp.int32, (c, c), 1)
    rel = (ii - jj).astype(F32)
    dmask = jnp.where(rel >= 0, jnp.exp(lg * jnp.maximum(rel, 0.0)), 0.0)
    row = ii.astype(F32)
    zeta = jnp.exp(lg * (c - 1 - row))
    xi = jnp.exp(lg * (row + 1.0))
    chunk_decay = jnp.exp(lg * c)
    even_lane = (jj & 1) == 0

    def rotate(t, cs, sn):
        partner = jnp.where(even_lane, pltpu.roll(t, HEAD_DIM - 1, 1), pltpu.roll(t, 1, 1))
        return t * cs + partner * sn

    chunks = [slice(n * c, (n + 1) * c) for n in range(n_chunks)]
    qr = [rotate(q_ref[sl, :], cos_ref[sl, :], sin_ref[sl, :]) for sl in chunks]
    kr = [rotate(k_ref[sl, :], cos_ref[sl, :], sin_ref[sl, :]) * (HEAD_DIM ** -0.5) for sl in chunks]
    intra = [_mm(_mm_nt(qr[n], kr[n]) * dmask, v_ref[sl, :]) for n, sl in enumerate(chunks)]
    kv = [_mm_tn(kr[n] * zeta, v_ref[sl, :]) for n, sl in enumerate(chunks)]
    st = st_ref[...]
    for n, sl in enumerate(chunks):
        y = intra[n] + _mm(qr[n], st) * xi
        st = st * chunk_decay + kv[n]
        mu = jnp.mean(y, axis=-1, keepdims=True)
        yc = y - mu
        var = jnp.mean(yc * yc, axis=-1, keepdims=True)
        yn = yc * lax.rsqrt(var + LN_EPS) * gnw_ref[...]
        o_ref[sl, :] = (yn * _silu(g_ref[sl, :])).astype(o_ref.dtype)
    st_ref[...] = st


def _retention(proj, cos2, sin2, log_g, gn_w, batch, seq, heads):
    T = batch * seq
    ts = _tile(seq, 2048, RET_CHUNK)
    ns = seq // ts
    blk = lambda off: pl.BlockSpec((ts, HEAD_DIM), lambda b, h, s: (b * ns + s, off * heads + h))
    tab = pl.BlockSpec((ts, HEAD_DIM), lambda b, h, s: (s, 0))
    kern = functools.partial(_retention_kernel, n_chunks=ts // RET_CHUNK)
    return pl.pallas_call(
        kern,
        grid=(batch, heads, ns),
        in_specs=[blk(0), blk(1), blk(2), blk(3), tab, tab,
                  pl.BlockSpec((1, 1, HEAD_DIM), lambda b, h, s: (h, 0, 0)),
                  pl.BlockSpec((1, HEAD_DIM), lambda b, h, s: (0, h))],
        out_specs=pl.BlockSpec((ts, HEAD_DIM), lambda b, h, s: (b * ns + s, h)),
        out_shape=jax.ShapeDtypeStruct((T, heads * HEAD_DIM), MXU_DTYPE),
        scratch_shapes=[pltpu.VMEM((HEAD_DIM, HEAD_DIM), F32)],
        compiler_params=_cparams(("parallel", "parallel", "arbitrary")),
        name="retention",
    )(proj, proj, proj, proj, cos2, sin2, log_g, gn_w.reshape(1, heads * HEAD_DIM))


def _swa_kernel(q_ref, k_ref, v_ref, bias_ref, o_ref, *, tq, span):
    qi = pl.program_id(2)
    start = jnp.maximum(qi * tq - SWA_FAR, 0)
    clamped = qi * tq < SWA_FAR
    groups = range(tq // SWA_ROWS)
    rows = [slice(r * SWA_ROWS, (r + 1) * SWA_ROWS) for r in groups]
    first = [pl.multiple_of(start + jnp.where(clamped, 0, r * SWA_ROWS), SWA_ROWS) for r in groups]
    s = [_mm_nt(q_ref[rows[r], :], k_ref[pl.ds(first[r], span), :]) * (HEAD_DIM ** -0.5) + bias_ref[0, rows[r], :]
         for r in groups]
    p = [jnp.exp(si - jnp.max(si, axis=-1, keepdims=True)) for si in s]
    l = [jnp.sum(pi, axis=-1, keepdims=True) for pi in p]
    for r in groups:
        o_ref[rows[r], :] = (_mm(p[r], v_ref[pl.ds(first[r], span), :]) / l[r]).astype(o_ref.dtype)


def _swa_bias(tq, span):
    nvar = SWA_FAR // tq + 1
    v = lax.broadcasted_iota(jnp.int32, (nvar, tq, span), 0)
    i = lax.broadcasted_iota(jnp.int32, (nvar, tq, span), 1)
    c = lax.broadcasted_iota(jnp.int32, (nvar, tq, span), 2)
    first = jnp.where(v == 0, (i // SWA_ROWS) * SWA_ROWS, 0)
    dist = SWA_FAR - v * tq + i - (first + c)
    mult = jnp.zeros((nvar, tq, span), F32)
    for window, dilation in SWA_PATTERNS:
        mult += ((dist >= 0) & (dist <= window) & (dist % dilation == 0)).astype(F32)
    return jnp.where(mult > 0, jnp.log(jnp.maximum(mult, 1.0)), MASKED)


def _swa(proj, batch, seq, heads):
    T = batch * seq
    tq = _tile(seq, 1024, SWA_ROWS)
    assert SWA_FAR % tq == 0 and seq >= SWA_FAR + tq
    span = SWA_FAR + SWA_ROWS
    nq = seq // tq
    nfar = SWA_FAR // tq
    kern = functools.partial(_swa_kernel, tq=tq, span=span)
    return pl.pallas_call(
        kern,
        grid=(batch, heads, nq),
        in_specs=[
            pl.BlockSpec((tq, HEAD_DIM), lambda b, h, i: (b * nq + i, h)),
            pl.BlockSpec((seq, HEAD_DIM), lambda b, h, i: (b, heads + h)),
            pl.BlockSpec((seq, HEAD_DIM), lambda b, h, i: (b, 2 * heads + h)),
            pl.BlockSpec((1, tq, span), lambda b, h, i: (jnp.maximum(nfar - i, 0), 0, 0)),
        ],
        out_specs=pl.BlockSpec((tq, HEAD_DIM), lambda b, h, i: (b * nq + i, h)),
        out_shape=jax.ShapeDtypeStruct((T, heads * HEAD_DIM), MXU_DTYPE),
        compiler_params=_cparams(("parallel", "parallel", "arbitrary")),
        name="dilated_swa",
    )(proj, proj, proj, _swa_bias(tq, span))


def _softplus(x):
    return jnp.maximum(x, 0.0) + jnp.log1p(jnp.exp(-jnp.abs(x)))


def _dn_gates_kernel(x_ref, w_ref, pc_ref, col_ref, row_ref, *, heads):
    c = DN_CHUNK
    ii = lax.broadcasted_iota(jnp.int32, (c, c), 0)
    jj = lax.broadcasted_iota(jnp.int32, (c, c), 1)
    incl = (ii >= jj).astype(F32)
    lane = lax.broadcasted_iota(jnp.int32, (c, 128), 1)
    for n in range(x_ref.shape[0] // c):
        rows = slice(n * c, (n + 1) * c)
        xb = x_ref[rows, :].astype(MXU_DTYPE)
        col = jnp.dot(xb, w_ref[...], preferred_element_type=F32)
        g_col = -jnp.exp(pc_ref[0:1, :]) * _softplus(col + pc_ref[1:2, :])
        gc_col = jnp.dot(incl, g_col, preferred_element_type=F32, precision=lax.Precision.HIGHEST)
        out = jnp.where(lane < heads, gc_col, jax.nn.sigmoid(col))
        col_ref[rows, :] = out
        row_ref[:, rows] = out.T


def _dn_gates(x, w_small, a_log, dt_bias, heads):
    T, D = x.shape
    assert 2 * heads <= 128
    pad = 128 - 2 * heads
    w = jnp.pad(w_small, ((0, 0), (0, pad))).astype(MXU_DTYPE)
    zeros = jnp.zeros((128 - heads,), F32)
    params = jnp.stack([jnp.concatenate([a_log.astype(F32), zeros]), jnp.concatenate([dt_bias.astype(F32), zeros])])
    ts = _tile(T, 4 * DN_CHUNK, DN_CHUNK)
    kern = functools.partial(_dn_gates_kernel, heads=heads)
    return pl.pallas_call(
        kern,
        grid=(T // ts,),
        in_specs=[
            pl.BlockSpec((ts, D), lambda i: (i, 0)),
            pl.BlockSpec((D, 128), lambda i: (0, 0)),
            pl.BlockSpec((2, 128), lambda i: (0, 0)),
        ],
        out_specs=[pl.BlockSpec((ts, 128), lambda i: (i, 0)), pl.BlockSpec((128, ts), lambda i: (0, i))],
        out_shape=[jax.ShapeDtypeStruct((T, 128), F32), jax.ShapeDtypeStruct((128, T), F32)],
        compiler_params=_cparams(("parallel",)),
        name="dn_gates",
    )(x, w, params)


def _conv_silu(x, prev, w):
    c = x.shape[0]
    taps = w.shape[0]
    xp = jnp.concatenate([prev, x], axis=0)
    acc = x * w[taps - 1:taps, :]
    for i in range(taps - 1):
        off = 8 - (taps - 1) + i
        acc = acc + xp[off:off + c, :] * w[i:i + 1, :]
    return _silu(acc)


def _l2norm(t):
    return t * lax.rsqrt(jnp.sum(t * t, axis=-1, keepdims=True) + NORM_EPS)


def _dn_kernel(q_ref, k_ref, v_ref, cwq_ref, cwk_ref, cwv_ref, gate_ref, col_ref, row_ref, nw_ref, o_ref,
               st_ref, tail_ref, *, group, heads):
    c = DN_CHUNK
    hs = range(group)

    @pl.when(pl.program_id(2) == 0)
    def _():
        st_ref[...] = jnp.zeros_like(st_ref)
        tail_ref[...] = jnp.zeros_like(tail_ref)

    ii = lax.broadcasted_iota(jnp.int32, (c, c), 0)
    jj = lax.broadcasted_iota(jnp.int32, (c, c), 1)
    eye = (ii == jj).astype(F32)
    n_levels = int(math.log2(c))
    level = [(ii >> 1) == (jj >> 1)] + [
        ((ii >> (ls + 1)) == (jj >> (ls + 1))) & (((ii >> ls) & 1) == 1) & (((jj >> ls) & 1) == 0)
        for ls in range(1, n_levels)]
    col = col_ref[...]
    lane = lax.broadcasted_iota(jnp.int32, col.shape, 1)
    cols = [slice(h * HEAD_DIM, (h + 1) * HEAD_DIM) for h in hs]
    head = [pl.program_id(1) * group + h for h in hs]
    q = [_l2norm(_conv_silu(q_ref[:, cols[h]], tail_ref[0, :, cols[h]], cwq_ref[:, cols[h]])) * (HEAD_DIM ** -0.5)
         for h in hs]
    k = [_l2norm(_conv_silu(k_ref[:, cols[h]], tail_ref[1, :, cols[h]], cwk_ref[:, cols[h]])) for h in hs]
    v = [_conv_silu(v_ref[:, cols[h]], tail_ref[2, :, cols[h]], cwv_ref[:, cols[h]]) for h in hs]
    for i, ref in enumerate((q_ref, k_ref, v_ref)):
        tail_ref[i] = ref[c - 8:c, :]
    gc_i =[jnp.sum(jnp.where(lane == head[h], col, 0.0), axis=-1, keepdims=True) for h in hs]
    beta = [jnp.sum(jnp.where(lane == head[h] + heads, col, 0.0), axis=-1, keepdims=True) for h in hs]
    gc_j = [row_ref[pl.ds(head[h], 1), :] for h in hs]
    gc_last = [gc_i[h][c - 1:c, :] for h in hs]
    decay = [jnp.exp(jnp.where(ii >= jj, gc_i[h] - gc_j[h], MASKED)) for h in hs]
    e_g = [jnp.exp(gc_i[h]) for h in hs]
    kb = [k[h] * beta[h] for h in hs]
    kq = [_mm_nt(jnp.concatenate([kb[h], q[h]], axis=0), k[h]) for h in hs]
    a = [jnp.where(ii > jj, kq[h][:c] * decay[h], 0.0) for h in hs]
    qk = [kq[h][c:] * decay[h] for h in hs]
    w0 = [jnp.where(level[0], a[h], 0.0) for h in hs]
    t = [eye - w0[h] for h in hs]
    z = [a[h] - _mm(a[h], w0[h]) for h in hs]
    for ls in range(1, n_levels):
        wl = [jnp.where(level[ls], z[h], 0.0) for h in hs]
        if ls < n_levels - 1:
            tz = [_mm(jnp.concatenate([t[h], z[h]], axis=0), wl[h]) for h in hs]
            t = [t[h] - tz[h][:c] for h in hs]
            z = [z[h] - tz[h][c:] for h in hs]
        else:
            t = [t[h] - _mm(t[h], wl[h]) for h in hs]
    uw = [_mm(t[h], jnp.concatenate([v[h] * beta[h], kb[h] * e_g[h]], axis=1)) for h in hs]
    st = [st_ref[h] for h in hs]
    ws_qs = [_mm(jnp.concatenate([uw[h][:, HEAD_DIM:], q[h] * e_g[h]], axis=0), st[h]) for h in hs]
    v_new = [uw[h][:, :HEAD_DIM] - ws_qs[h][:c] for h in hs]
    o = [ws_qs[h][c:] + _mm(qk[h], v_new[h]) for h in hs]
    for h in hs:
        st_ref[h] = st[h] * jnp.exp(gc_last[h]) + _mm_tn(k[h] * jnp.exp(gc_last[h] - gc_i[h]), v_new[h])
    for h in hs:
        on = o[h] * lax.rsqrt(jnp.mean(o[h] * o[h], axis=-1, keepdims=True) + NORM_EPS) * nw_ref[...]
        o_ref[:, cols[h]] = (on * _silu(gate_ref[:, cols[h]])).astype(o_ref.dtype)


def _delta_rule(qkv, conv_w, gate, g_col, g_row, norm_w, batch, seq, heads):
    T = batch * seq
    c = DN_CHUNK
    group = _tile(heads, DN_GROUP, 1)
    ng = heads // group
    nc = seq // c
    gw = group * HEAD_DIM
    taps = conv_w.shape[0]
    blk = lambda off: pl.BlockSpec((c, gw), lambda b, g, n: (b * nc + n, off * ng + g))
    cw = lambda off: pl.BlockSpec((taps, gw), lambda b, g, n: (0, off * ng + g))
    kern = functools.partial(_dn_kernel, group=group, heads=heads)
    return pl.pallas_call(
        kern,
        grid=(batch, ng, nc),
        in_specs=[blk(0), blk(1), blk(2), cw(0), cw(1), cw(2),
                  pl.BlockSpec((c, gw), lambda b, g, n: (b * nc + n, g)),
                  pl.BlockSpec((c, 128), lambda b, g, n: (b * nc + n, 0)),
                  pl.BlockSpec((128, c), lambda b, g, n: (0, b * nc + n)),
                  pl.BlockSpec((1, HEAD_DIM), lambda b, g, n: (0, 0))],
        out_specs=pl.BlockSpec((c, gw), lambda b, g, n: (b * nc + n, g)),
        out_shape=jax.ShapeDtypeStruct((T, heads * HEAD_DIM), MXU_DTYPE),
        scratch_shapes=[pltpu.VMEM((group, HEAD_DIM, HEAD_DIM), F32),
                        pltpu.VMEM((3, 8, gw), F32)],
        compiler_params=_cparams(("parallel", "parallel", "arbitrary")),
        name="delta_rule",
    )(qkv, qkv, qkv, conv_w, conv_w, conv_w, gate, g_col, g_row, norm_w.reshape(1, HEAD_DIM))


def _rope_tables(seq):
    pos = jnp.arange(seq, dtype=F32)
    inv_freq = 1.0 / (ROPE_BASE ** jnp.linspace(0.0, 1.0, HEAD_DIM // 2, dtype=F32))
    ang = pos[:, None] * inv_freq[None, :]
    cos, sin = jnp.cos(ang), jnp.sin(ang)
    return jnp.repeat(cos, 2, axis=1), jnp.stack([-sin, sin], axis=-1).reshape(seq, HEAD_DIM)


def _mixer_retention_swa(x, w_in, gn_w, w_out, lnw, lnb, alpha, batch, seq, ffn_cast, early_cast):
    D = x.shape[1]
    width = D // 2
    heads = width // HEAD_DIM
    proj_ret, wgu = _proj(x, w_in, 0, 4 * width, F32, cast=ffn_cast[:1])
    proj_att, wd, *early = _proj(x, w_in, 4 * width, 3 * width, MXU_DTYPE, cast=ffn_cast[1:] + early_cast)
    cos2, sin2 = _rope_tables(seq)
    log_g = jnp.log1p(-jnp.exp2(-5.0 - jnp.arange(heads, dtype=F32)))
    log_g = jnp.broadcast_to(log_g[:, None, None], (heads, 1, HEAD_DIM))
    y_ret = _retention(proj_ret, cos2, sin2, log_g, gn_w, batch, seq, heads)
    y_att = _swa(proj_att, batch, seq, heads)
    return _proj_ln([y_ret, y_att], w_out, x, lnw, lnb, alpha), (wgu, wd), early


def _mixer_gated_deltanet(x, w_in, w_gates, conv_w, a_log, dt_bias, norm_w, w_out, lnw, lnb, alpha, batch, seq,
                          ffn_cast):
    D = x.shape[1]
    heads = D // HEAD_DIM
    qkv, wgu = _proj(x, w_in, 0, 3 * D, F32, cast=ffn_cast[:1])
    gate, wd = _proj(x, w_in, 3 * D, D, F32, cast=ffn_cast[1:])
    g_col, g_row = _dn_gates(x, w_gates, a_log, dt_bias, heads)
    o = _delta_rule(qkv, conv_w, gate, g_col, g_row, norm_w, batch, seq, heads)
    return _proj_ln([o], w_out, x, lnw, lnb, alpha), (wgu, wd)


def kernel(x, ffn_w_gate, ffn_w_up, ffn_w_down, ln_w, ln_b, ab_w_in, ab_gn_w, ab_w_out,
           dn_w_in, dn_conv_w, dn_a_log, dn_dt_bias, dn_norm_w, dn_w_out):
    batch, seq, D = x.shape
    depth = ffn_w_gate.shape[0]
    alpha = (2.0 * depth) ** 0.25
    h = x.reshape(batch * seq, D)
    ffn_items = lambda at: [((ffn_w_gate, ffn_w_up), at), ((ffn_w_down,), at)]
    pre = [_cast_weight(*item) for item in ffn_items((0, 0))]
    for l in range(depth):
        h = _ffn_ln(h, *pre, ln_w[l, 0], ln_b[l, 0], alpha)
        i = l // 2
        next_at = (l + 1, 0) if l + 1 < depth else None
        early = []
        if l % 2 == 0:
            early_cast = ffn_items(next_at)[1:] if next_at else []
            h, post, early = _mixer_retention_swa(h, ab_w_in[i].astype(MXU_DTYPE), ab_gn_w[i],
                                                  ab_w_out[i].astype(MXU_DTYPE), ln_w[l, 1], ln_b[l, 1], alpha,
                                                  batch, seq, ffn_items((l, 1)), early_cast)
        else:
            h, post = _mixer_gated_deltanet(h, dn_w_in[i].astype(MXU_DTYPE), dn_w_in[i][:, 4 * D:], dn_conv_w[i],
                                            dn_a_log[i], dn_dt_bias[i], dn_norm_w[i], dn_w_out[i].astype(MXU_DTYPE),
                                            ln_w[l, 1], ln_b[l, 1], alpha, batch, seq, ffn_items((l, 1)))
        h = _ffn_ln(h, *post, ln_w[l, 2], ln_b[l, 2], alpha)
        if next_at:
            gate_up, down = ffn_items(next_at)
            pre = [_cast_weight(*gate_up), early[0] if early else _cast_weight(*down)]
    return h.reshape(batch, seq, D)
```
